```python
import math
import jax, jax.numpy as jnp
from jax import lax
import numpy as np

D_MODEL = 1024
BATCH = 2
SEQ = 16384
DEPTH = 1
DEC_BATCH = 8
DEC_SEQ = 32
PAST_LEN = 2048

CHUNK = 64
A_HEADS = 8
A_NOPE = 64
A_ROPE = 32
A_VDIM = 64
A_QK = A_NOPE + A_ROPE
A_Q_RANK = 384
A_KV_RANK = 256
A_WIDTH = A_HEADS * A_VDIM
A_SCALE = A_QK ** -0.5
B_HEADS = 8
B_HDIM = 64
B_WIDTH = B_HEADS * B_HDIM
B_LEFT_CHUNKS = 8
B_WINDOW = B_LEFT_CHUNKS * CHUNK
B_MAX_REL = 256
B_N_REL = 2 * B_MAX_REL + 1
B_SCALE = B_HDIM ** -0.5
ROPE_BASE = 10000.0
NORM_EPS = 1e-6
Q_BLOCK = 128
NEG_INF = -1e30
IN_SIZES = (A_Q_RANK, A_KV_RANK, A_ROPE, A_WIDTH, B_WIDTH, B_WIDTH, B_WIDTH, B_WIDTH, D_MODEL, D_MODEL)
IN_DIM = A_Q_RANK + A_KV_RANK + A_ROPE + A_WIDTH + 4 * B_WIDTH + 2 * D_MODEL

kernel_name = 'hybrid_stream_mla_chunkband_step'


def rms_norm(x, g):
    xf = x.astype(jnp.float32)
    xf = xf * lax.rsqrt(jnp.mean(xf * xf, axis=-1, keepdims=True) + NORM_EPS)
    return (xf * g.astype(jnp.float32)).astype(x.dtype)


def softmax_f32(s, dtype):
    return jax.nn.softmax(s.astype(jnp.float32), axis=-1).astype(dtype)


def split_in(z):
    offs = []
    acc = 0
    for n in IN_SIZES[:-1]:
        acc += n
        offs.append(acc)
    return jnp.split(z, offs, axis=-1)


def rope_cos_sin(pos):
    inv = ROPE_BASE ** (-jnp.arange(0, A_ROPE, 2, dtype=jnp.float32) / A_ROPE)
    ang = pos.astype(jnp.float32)[:, None] * inv[None, :]
    return jnp.cos(ang), jnp.sin(ang)


def apply_rope(x, cos, sin):
    half = x.shape[-1] // 2
    x1 = x[..., :half].astype(jnp.float32)
    x2 = x[..., half:].astype(jnp.float32)
    return jnp.concatenate([x1 * cos - x2 * sin, x2 * cos + x1 * sin], axis=-1).astype(x.dtype)


def layer_front(x, c, pos, g_norm, w_ada, b_ada, w_in, g_q_lat, w_uq, g_kv_lat,
                g_qn_a, g_qr_a, g_kr_a, g_q_b, g_k_b):
    bn, s, _ = x.shape
    mod = jax.nn.silu(c) @ w_ada + b_ada
    shift, scale, gate = jnp.split(mod, 3, axis=-1)
    h = rms_norm(x, g_norm) * (1 + scale[:, None, :]) + shift[:, None, :]
    q_lat, kv_lat, kr_raw, g_a, q_b, k_b, v_b, g_b, m_a, m_b = split_in(h @ w_in)
    cos, sin = rope_cos_sin(pos)
    q_a = jnp.einsum('bsr,rhd->bshd', rms_norm(q_lat, g_q_lat), w_uq)
    q_nope = rms_norm(q_a[..., :A_NOPE], g_qn_a)
    q_rope = apply_rope(rms_norm(q_a[..., A_NOPE:], g_qr_a), cos[:, None, :], sin[:, None, :])
    latent = rms_norm(kv_lat, g_kv_lat)
    k_rope = apply_rope(rms_norm(kr_raw, g_kr_a), cos, sin)
    q_b = rms_norm(q_b.reshape(bn, s, B_HEADS, B_HDIM), g_q_b)
    k_b = rms_norm(k_b.reshape(bn, s, B_HEADS, B_HDIM), g_k_b)
    v_b = v_b.reshape(bn, s, B_HEADS, B_HDIM)
    return gate[:, None, :], q_nope, q_rope, latent, k_rope, g_a, q_b, k_b, v_b, g_b, m_a, m_b


def mla_expand(latent, w_uk, w_uv, g_kn_a):
    k_nope = rms_norm(jnp.einsum('bsr,rhd->bshd', latent, w_uk), g_kn_a)
    v = jnp.einsum('bsr,rhd->bshd', latent, w_uv)
    return k_nope, v


def mla_scores(q_nope, q_rope, k_nope, k_rope):
    s = (jnp.einsum('bqhd,bkhd->bhqk', q_nope, k_nope)
         + jnp.einsum('bqhd,bkd->bhqk', q_rope, k_rope))
    return s.astype(jnp.float32) * A_SCALE


def mla_prompt(q_nope, q_rope, k_nope, k_rope, v):
    bn, s, h, _ = q_nope.shape
    nb = s // Q_BLOCK
    qn = q_nope.reshape(bn, nb, Q_BLOCK, h, A_NOPE).transpose(1, 0, 2, 3, 4)
    qr = q_rope.reshape(bn, nb, Q_BLOCK, h, A_ROPE).transpose(1, 0, 2, 3, 4)
    k_chunk = jnp.arange(s) // CHUNK

    def block(args):
        i, qn_b, qr_b = args
        sc = mla_scores(qn_b, qr_b, k_nope, k_rope)
        q_chunk = (i * Q_BLOCK + jnp.arange(Q_BLOCK)) // CHUNK
        mask = k_chunk[None, :] <= q_chunk[:, None]
        p = softmax_f32(jnp.where(mask, sc, NEG_INF), v.dtype)
        return jnp.einsum('bhqk,bkhd->bqhd', p, v)

    out = lax.map(block, (jnp.arange(nb), qn, qr))
    return out.transpose(1, 0, 2, 3, 4).reshape(bn, s, h * A_VDIM)


def mla_sample(q_nope, q_rope, k_nope, k_rope, v):
    bn, t, h, _ = q_nope.shape
    p = softmax_f32(mla_scores(q_nope, q_rope, k_nope, k_rope), v.dtype)
    return jnp.einsum('bhqk,bkhd->bqhd', p, v).reshape(bn, t, h * A_VDIM)


def band_bias(rel_bias, q_pos, k_pos):
    idx = jnp.clip(q_pos[:, None] - k_pos[None, :], -B_MAX_REL, B_MAX_REL) + B_MAX_REL
    return rel_bias[:, idx].astype(jnp.float32)


def band_prompt(q, k, v, rel_bias):
    bn, s, h, d = q.shape
    nc = s // CHUNK
    nbc = B_LEFT_CHUNKS + 1
    qc = q.reshape(bn, nc, CHUNK, h, d)
    pad = ((0, 0), (B_LEFT_CHUNKS, 0), (0, 0), (0, 0), (0, 0))
    kp = jnp.pad(k.reshape(bn, nc, CHUNK, h, d), pad)
    vp = jnp.pad(v.reshape(bn, nc, CHUNK, h, d), pad)
    kband = jnp.concatenate([kp[:, j:j + nc] for j in range(nbc)], axis=2)
    vband = jnp.concatenate([vp[:, j:j + nc] for j in range(nbc)], axis=2)
    band_off = jnp.arange(nbc * CHUNK) - B_WINDOW
    bias = band_bias(rel_bias, jnp.arange(CHUNK), band_off)
    valid = (jnp.arange(nc)[:, None] + band_off[None, :] // CHUNK) >= 0
    sc = jnp.einsum('bcqhd,bckhd->bhcqk', qc, kband).astype(jnp.float32) * B_SCALE + bias[None, :, None]
    sc = jnp.where(valid[None, None, :, None, :], sc, NEG_INF)
    p = softmax_f32(sc, v.dtype)
    return jnp.einsum('bhcqk,bckhd->bcqhd', p, vband).reshape(bn, s, h * d)


def band_sample(q, k_all, v_all, rel_bias, q_pos, k_pos):
    bn, t, h, d = q.shape
    sc = jnp.einsum('bqhd,bkhd->bhqk', q, k_all).astype(jnp.float32) * B_SCALE
    sc = sc + band_bias(rel_bias, q_pos, k_pos)[None]
    p = softmax_f32(sc, v_all.dtype)
    return jnp.einsum('bhqk,bkhd->bqhd', p, v_all).reshape(bn, t, h * d)


def layer_back(x, gate_c, attn_a, g_a, attn_b, g_b, m_a, m_b, w_oa, w_ob, w_out):
    u_a = (attn_a * jax.nn.silu(g_a)) @ w_oa
    u_b = (attn_b * jax.nn.silu(g_b)) @ w_ob
    merged = jax.nn.sigmoid(m_a) * u_a + jax.nn.sigmoid(m_b) * u_b
    return x + gate_c * (merged @ w_out)


def setup_inputs(seed: int = 0) -> dict:
    key = jax.random.key(seed)
    ks = jax.random.split(key, 32)
    f32 = jnp.float32

    def nrm(k, shape, scale=1.0):
        return jax.random.normal(k, shape, f32) * scale

    def gain(k, n):
        return 1.0 + 0.02 * jax.random.normal(k, (DEPTH, n), f32)

    bw = min(B_WINDOW, PAST_LEN)
    return {
        'x_prompt': nrm(ks[0], (BATCH, SEQ, D_MODEL)),
        'x_sample': nrm(ks[1], (DEC_BATCH, DEC_SEQ, D_MODEL)),
        'cache_mla_latent': nrm(ks[2], (DEPTH, DEC_BATCH, PAST_LEN, A_KV_RANK)),
        'cache_mla_krope': nrm(ks[3], (DEPTH, DEC_BATCH, PAST_LEN, A_ROPE)),
        'cache_band_k': nrm(ks[4], (DEPTH, DEC_BATCH, bw, B_HEADS, B_HDIM)),
        'cache_band_v': nrm(ks[5], (DEPTH, DEC_BATCH, bw, B_HEADS, B_HDIM)),
        'c_prompt': nrm(ks[6], (BATCH, D_MODEL)),
        'c_sample': nrm(ks[7], (DEC_BATCH, D_MODEL)),
        'g_norm': gain(ks[8], D_MODEL),
        'w_ada': nrm(ks[9], (DEPTH, D_MODEL, 3 * D_MODEL), 0.5 * D_MODEL ** -0.5),
        'b_ada': nrm(ks[10], (DEPTH, 3 * D_MODEL), 0.02),
        'w_in': nrm(ks[11], (DEPTH, D_MODEL, IN_DIM), D_MODEL ** -0.5),
        'g_q_lat': gain(ks[12], A_Q_RANK),
        'w_uq': nrm(ks[13], (DEPTH, A_Q_RANK, A_HEADS, A_QK), A_Q_RANK ** -0.5),
        'g_kv_lat': gain(ks[14], A_KV_RANK),
        'w_uk': nrm(ks[15], (DEPTH, A_KV_RANK, A_HEADS, A_NOPE), A_KV_RANK ** -0.5),
        'w_uv': nrm(ks[16], (DEPTH, A_KV_RANK, A_HEADS, A_VDIM), A_KV_RANK ** -0.5),
        'g_qn_a': gain(ks[17], A_NOPE),
        'g_qr_a': gain(ks[18], A_ROPE),
        'g_kn_a': gain(ks[19], A_NOPE),
        'g_kr_a': gain(ks[20], A_ROPE),
        'g_q_b': gain(ks[21], B_HDIM),
        'g_k_b': gain(ks[22], B_HDIM),
        'rel_bias_b': nrm(ks[23], (DEPTH, B_HEADS, B_N_REL), 0.5),
        'w_oa': nrm(ks[24], (DEPTH, A_WIDTH, D_MODEL), A_WIDTH ** -0.5),
        'w_ob': nrm(ks[25], (DEPTH, B_WIDTH, D_MODEL), B_WIDTH ** -0.5),
        'w_out': nrm(ks[26], (DEPTH, D_MODEL, D_MODEL), D_MODEL ** -0.5),
    }


def reference(x_prompt, x_sample, cache_mla_latent, cache_mla_krope, cache_band_k, cache_band_v,
              c_prompt, c_sample, g_norm, w_ada, b_ada, w_in, g_q_lat, w_uq, g_kv_lat, w_uk, w_uv,
              g_qn_a, g_qr_a, g_kn_a, g_kr_a, g_q_b, g_k_b, rel_bias_b, w_oa, w_ob, w_out):
    s = x_prompt.shape[1]
    t = x_sample.shape[1]
    past = cache_mla_latent.shape[2]
    win_s = cache_band_k.shape[2]
    win_p = min(B_WINDOW, s)
    pos_p = jnp.arange(s)
    pos_s = past + jnp.arange(t)
    band_kpos_s = jnp.concatenate([past - win_s + jnp.arange(win_s), pos_s])
    xp = x_prompt
    xs = x_sample
    lat_p, kr_p, kb_p, vb_p = [], [], [], []
    lat_s, kr_s, kb_s, vb_s = [], [], [], []
    for l in range(DEPTH):
        gc, qn, qr, lat, kr, ga, qb, kb, vb, gb, ma, mb = layer_front(
            xp, c_prompt, pos_p, g_norm[l], w_ada[l], b_ada[l], w_in[l], g_q_lat[l], w_uq[l],
            g_kv_lat[l], g_qn_a[l], g_qr_a[l], g_kr_a[l], g_q_b[l], g_k_b[l])
        kn, va = mla_expand(lat, w_uk[l], w_uv[l], g_kn_a[l])
        attn_a = mla_prompt(qn, qr, kn, kr, va)
        attn_b = band_prompt(qb, kb, vb, rel_bias_b[l])
        xp = layer_back(xp, gc, attn_a, ga, attn_b, gb, ma, mb, w_oa[l], w_ob[l], w_out[l])
        lat_p.append(lat)
        kr_p.append(kr)
        kb_p.append(kb[:, s - win_p:])
        vb_p.append(vb[:, s - win_p:])
        gc2, qn2, qr2, lat2, kr2, ga2, qb2, kb2, vb2, gb2, ma2, mb2 = layer_front(
            xs, c_sample, pos_s, g_norm[l], w_ada[l], b_ada[l], w_in[l], g_q_lat[l], w_uq[l],
            g_kv_lat[l], g_qn_a[l], g_qr_a[l], g_kr_a[l], g_q_b[l], g_k_b[l])
        lat_all = jnp.concatenate([cache_mla_latent[l], lat2], axis=1)
        kr_all = jnp.concatenate([cache_mla_krope[l], kr2], axis=1)
        kn_all, va_all = mla_expand(lat_all, w_uk[l], w_uv[l], g_kn_a[l])
        attn_a2 = mla_sample(qn2, qr2, kn_all, kr_all, va_all)
        kb_all = jnp.concatenate([cache_band_k[l], kb2], axis=1)
        vb_all = jnp.concatenate([cache_band_v[l], vb2], axis=1)
        attn_b2 = band_sample(qb2, kb_all, vb_all, rel_bias_b[l], pos_s, band_kpos_s)
        xs = layer_back(xs, gc2, attn_a2, ga2, attn_b2, gb2, ma2, mb2, w_oa[l], w_ob[l], w_out[l])
        lat_s.append(lat2)
        kr_s.append(kr2)
        kb_s.append(kb2)
        vb_s.append(vb2)
    return (xp, xs, jnp.stack(lat_p), jnp.stack(kr_p), jnp.stack(kb_p), jnp.stack(vb_p),
            jnp.stack(lat_s), jnp.stack(kr_s), jnp.stack(kb_s), jnp.stack(vb_s))
```

```python
import functools
import math

import jax
import jax.numpy as jnp
import numpy as np
from jax import lax
from jax.experimental import pallas as pl
from jax.experimental.pallas import tpu as pltpu

F32 = jnp.float32
BF16 = jnp.bfloat16

D_MODEL = 1024
CHUNK = 64
A_HEADS = 8
A_NOPE = 64
A_ROPE = 32
A_VDIM = 64
A_QK = A_NOPE + A_ROPE
A_Q_RANK = 384
A_KV_RANK = 256
A_WIDTH = A_HEADS * A_VDIM
A_SCALE = A_QK ** -0.5
B_HEADS = 8
B_HDIM = 64
B_WIDTH = B_HEADS * B_HDIM
B_LEFT_CHUNKS = 8
B_WINDOW = B_LEFT_CHUNKS * CHUNK
B_MAX_REL = 256
B_SCALE = B_HDIM ** -0.5
ROPE_BASE = 10000.0
NORM_EPS = 1e-6
NEG_INF = -1e30
LOG2E = math.log2(math.e)

LANES = 128
SLAB = LANES
HALF_ROPE = A_ROPE // 2
ROPE_LANE0 = A_NOPE
VMEM_LIMIT = 48 * 1024 * 1024

C_QLAT = 0
C_KVLAT = C_QLAT + A_Q_RANK
C_KR = C_KVLAT + A_KV_RANK
C_GA = C_KR + SLAB
C_QB = C_GA + A_WIDTH
C_KB = C_QB + B_WIDTH
C_VB = C_KB + B_WIDTH
C_GB = C_VB + B_WIDTH
C_MA = C_GB + B_WIDTH
C_MB = C_MA + D_MODEL
C_END = C_MB + D_MODEL

NT = (((1,), (1,)), ((), ()))


def _dot(a, b):
    return jnp.dot(a, b, preferred_element_type=F32)


def _dot_nt(a, b):
    return lax.dot_general(a, b, NT, preferred_element_type=F32)


def _rms_full(x, g):
    ms = jnp.mean(x * x, axis=-1, keepdims=True)
    return x * lax.rsqrt(ms + NORM_EPS) * g


def _seg_rms(x, seg, g):
    ms = _dot((x * x).astype(BF16), seg)
    return x * lax.rsqrt(ms + NORM_EPS) * g


def _rope_slab(x, cos_t, sin_lo, sin_hi):
    return (x * cos_t
            + pltpu.roll(x, SLAB - HALF_ROPE, axis=1) * sin_lo
            + pltpu.roll(x, HALF_ROPE, axis=1) * sin_hi)


def _adaln_kernel(c_ref, w_ref, b_ref, o_ref):
    c = c_ref[...]
    sc = c * jax.nn.sigmoid(c)
    o_ref[...] = jnp.dot(sc, w_ref[...], preferred_element_type=F32,
                         precision=lax.Precision.HIGHEST) + b_ref[...]


def _adaln(c_rows, w_ada, b_ada):
    n = c_rows.shape[0]
    tn = 1024
    return pl.pallas_call(
        _adaln_kernel,
        out_shape=jax.ShapeDtypeStruct((n, 3 * D_MODEL), F32),
        grid=(3 * D_MODEL // tn,),
        in_specs=[pl.BlockSpec((n, D_MODEL), lambda j: (0, 0)),
                  pl.BlockSpec((D_MODEL, tn), lambda j: (0, j)),
                  pl.BlockSpec((1, tn), lambda j: (0, j))],
        out_specs=pl.BlockSpec((n, tn), lambda j: (0, j)),
        compiler_params=pltpu.CompilerParams(dimension_semantics=("arbitrary",),
                                             vmem_limit_bytes=VMEM_LIMIT),
        name="adaln",
    )(c_rows, w_ada, b_ada.reshape(1, -1))


def _front_kernel(x_ref, scale_ref, shift_ref, cos_ref, slo_ref, shi_ref,
                  gnorm_ref, win_ref, gql_ref, wuq_ref, gkv_ref, gq_ref, gkr_ref,
                  gqb_ref, gkb_ref, segq_ref, seg64_ref,
                  qa_ref, lat_ref, kr_ref, krs_ref, sga_ref, qb_ref, kb_ref, vb_ref,
                  kbf_ref, vbf_ref, sgb_ref, sma_ref, smb_ref, *, per_row):
    x = x_ref[...]
    if per_row:
        scale, shift = scale_ref[...], shift_ref[...]
    else:
        scale, shift = scale_ref[0], shift_ref[0]
    h = _rms_full(x, gnorm_ref[...]) * (1.0 + scale) + shift
    hb = h.astype(BF16)
    cos_t, sin_lo, sin_hi = cos_ref[...], slo_ref[...], shi_ref[...]

    def proj(c0, c1):
        return _dot(hb, win_ref[:, c0:c1])

    r = _rms_full(proj(C_QLAT, C_KVLAT), gql_ref[...]).astype(BF16)
    segq = segq_ref[...]
    for p in range(A_HEADS // 2):
        c0 = 2 * SLAB * p
        qa = _dot(r, wuq_ref[:, c0:c0 + 2 * SLAB])
        qa = _seg_rms(qa, segq, gq_ref[:, c0:c0 + 2 * SLAB])
        for s in range(2):
            slab = _rope_slab(qa[:, s * SLAB:(s + 1) * SLAB], cos_t, sin_lo, sin_hi)
            qa_ref[:, c0 + s * SLAB:c0 + (s + 1) * SLAB] = slab.astype(BF16)

    lat_ref[...] = _rms_full(proj(C_KVLAT, C_KR), gkv_ref[...])
    krs = proj(C_KR, C_GA)
    ms = jnp.sum(krs * krs, axis=-1, keepdims=True) * (1.0 / A_ROPE)
    krs = _rope_slab(krs * lax.rsqrt(ms + NORM_EPS) * gkr_ref[...], cos_t, sin_lo, sin_hi)
    krs_ref[...] = krs.astype(BF16)
    kr_ref[...] = pltpu.roll(krs, SLAB - ROPE_LANE0, axis=1)[:, :A_ROPE]

    ga = proj(C_GA, C_QB)
    sga_ref[...] = (ga * jax.nn.sigmoid(ga)).astype(BF16)
    gb = proj(C_GB, C_MA)
    sgb_ref[...] = (gb * jax.nn.sigmoid(gb)).astype(BF16)
    sma_ref[...] = jax.nn.sigmoid(proj(C_MA, C_MB)).astype(BF16)
    smb_ref[...] = jax.nn.sigmoid(proj(C_MB, C_END)).astype(BF16)

    seg64 = seg64_ref[...]
    for p in range(B_WIDTH // (2 * SLAB)):
        c0 = 2 * SLAB * p
        qb = proj(C_QB + c0, C_QB + c0 + 2 * SLAB)
        qb_ref[:, c0:c0 + 2 * SLAB] = _seg_rms(qb, seg64, gqb_ref[:, c0:c0 + 2 * SLAB]).astype(BF16)
        kb = proj(C_KB + c0, C_KB + c0 + 2 * SLAB)
        kb = _seg_rms(kb, seg64, gkb_ref[:, c0:c0 + 2 * SLAB])
        kbf_ref[:, c0:c0 + 2 * SLAB] = kb
        kb_ref[:, c0:c0 + 2 * SLAB] = kb.astype(BF16)
    vb = proj(C_VB, C_GB)
    vbf_ref[...] = vb
    vb_ref[...] = vb.astype(BF16)


def _front(x2d, scale, shift, rope_tabs, consts, *, tm, rows_per_batch, per_row):
    rows = x2d.shape[0]
    nt = rows // tm
    row = lambda i: (i, 0)
    fixed = lambda i: (0, 0)
    if per_row:
        mod_spec = pl.BlockSpec((tm, D_MODEL), row)
    else:
        tpb = rows_per_batch // tm
        mod_spec = pl.BlockSpec((1, 1, D_MODEL), lambda i: (i // tpb, 0, 0))
        tabs_per_batch = tpb
    if per_row:
        tab_spec = pl.BlockSpec((tm, SLAB), row)
    else:
        tab_spec = pl.BlockSpec((tm, SLAB), lambda i: (i % tabs_per_batch, 0))

    def full(a):
        return pl.BlockSpec(a.shape, fixed)

    def out(width, dtype):
        return jax.ShapeDtypeStruct((rows, width), dtype), pl.BlockSpec((tm, width), row)

    outs = [out(A_HEADS * SLAB, BF16),
            out(A_KV_RANK, F32),
            out(A_ROPE, F32),
            out(SLAB, BF16),
            out(A_WIDTH, BF16),
            out(B_WIDTH, BF16),
            out(B_WIDTH, BF16),
            out(B_WIDTH, BF16),
            out(B_WIDTH, F32),
            out(B_WIDTH, F32),
            out(B_WIDTH, BF16),
            out(D_MODEL, BF16),
            out(D_MODEL, BF16)]
    return pl.pallas_call(
        functools.partial(_front_kernel, per_row=per_row),
        out_shape=[o[0] for o in outs],
        grid=(nt,),
        in_specs=[pl.BlockSpec((tm, D_MODEL), row), mod_spec, mod_spec,
                  tab_spec, tab_spec, tab_spec] + [full(a) for a in consts],
        out_specs=[o[1] for o in outs],
        compiler_params=pltpu.CompilerParams(dimension_semantics=("arbitrary",),
                                             vmem_limit_bytes=VMEM_LIMIT),
        name="front",
    )(x2d, scale, shift, *rope_tabs, *consts)


def _expand_kernel(lat_ref, krs_ref, wuk_ref, wuv_ref, gk_ref, segk_ref, ka_ref, va_ref):
    latb = lat_ref[...].astype(BF16)
    krs = krs_ref[...].astype(F32)
    segk = segk_ref[...]
    for p in range(A_HEADS // 2):
        c0 = 2 * SLAB * p
        kn = _dot(latb, wuk_ref[:, c0:c0 + 2 * SLAB])
        kn = _seg_rms(kn, segk, gk_ref[:, c0:c0 + 2 * SLAB])
        for s in range(2):
            ka_ref[:, c0 + s * SLAB:c0 + (s + 1) * SLAB] = (
                kn[:, s * SLAB:(s + 1) * SLAB] + krs).astype(BF16)
    va_ref[...] = _dot(latb, wuv_ref[...]).astype(BF16)


def _expand(lat2d, krs2d, wuk_p, wuv_b, gk, segk, *, tm):
    rows = lat2d.shape[0]
    row = lambda i: (i, 0)
    fixed = lambda i: (0, 0)
    return pl.pallas_call(
        _expand_kernel,
        out_shape=[jax.ShapeDtypeStruct((rows, A_HEADS * SLAB), BF16),
                   jax.ShapeDtypeStruct((rows, A_WIDTH), BF16)],
        grid=(rows // tm,),
        in_specs=[pl.BlockSpec((tm, A_KV_RANK), row), pl.BlockSpec((tm, SLAB), row),
                  pl.BlockSpec(wuk_p.shape, fixed), pl.BlockSpec(wuv_b.shape, fixed),
                  pl.BlockSpec(gk.shape, fixed), pl.BlockSpec(segk.shape, fixed)],
        out_specs=[pl.BlockSpec((tm, A_HEADS * SLAB), row), pl.BlockSpec((tm, A_WIDTH), row)],
        compiler_params=pltpu.CompilerParams(dimension_semantics=("arbitrary",),
                                             vmem_limit_bytes=VMEM_LIMIT),
        name="expand",
    )(lat2d, krs2d, wuk_p, wuv_b, gk, segk)


def _select_heads(o_even, o_odd):
    lane = lax.broadcasted_iota(jnp.int32, o_even.shape, 1)
    return jnp.where(lane < A_VDIM, o_even, o_odd)


def _mla_prompt_kernel(q_ref, k_ref, v_ref, o_ref, *, tq):
    qi = pl.program_id(2)
    row_chunk = lax.broadcasted_iota(jnp.int32, (tq, tq), 0) // CHUNK
    col_chunk = lax.broadcasted_iota(jnp.int32, (tq, tq), 1) // CHUNK
    diag_mask = col_chunk <= row_chunk
    outs = []
    for e in range(2):
        q = q_ref[0, :, e * SLAB:(e + 1) * SLAB]

        def step(k0, carry, mask):
            m, l, acc = carry
            k = k_ref[0, pl.ds(k0, tq), e * SLAB:(e + 1) * SLAB]
            v = v_ref[0, pl.ds(k0, tq), :]
            s = _dot_nt(q, k)
            if mask is not None:
                s = jnp.where(mask, s, NEG_INF)
            m_new = jnp.maximum(m, jnp.max(s, axis=-1, keepdims=True))
            alpha = jnp.exp2(m - m_new)
            p = jnp.exp2(s - m_new)
            l = alpha * l + jnp.sum(p, axis=-1, keepdims=True)
            acc = alpha * acc + _dot(p.astype(BF16), v)
            return m_new, l, acc

        init = (jnp.full((tq, 1), NEG_INF, F32), jnp.zeros((tq, 1), F32),
                jnp.zeros((tq, SLAB), F32))
        carry = lax.fori_loop(
            0, qi, lambda j, c: step(pl.multiple_of(j * tq, tq), c, None), init)
        m, l, acc = step(pl.multiple_of(qi * tq, tq), carry, diag_mask)
        outs.append(acc / l)
    o_ref[0] = _select_heads(outs[0], outs[1]).astype(o_ref.dtype)


def _mla_prompt(qa, ka, va, *, tq):
    b, s, _ = qa.shape
    return pl.pallas_call(
        functools.partial(_mla_prompt_kernel, tq=tq),
        out_shape=jax.ShapeDtypeStruct((b, s, A_WIDTH), BF16),
        grid=(b, A_HEADS // 2, s // tq),
        in_specs=[pl.BlockSpec((1, tq, 2 * SLAB), lambda bi, hp, qi: (bi, qi, hp)),
                  pl.BlockSpec((1, s, 2 * SLAB), lambda bi, hp, qi: (bi, 0, hp)),
                  pl.BlockSpec((1, s, SLAB), lambda bi, hp, qi: (bi, 0, hp))],
        out_specs=pl.BlockSpec((1, tq, SLAB), lambda bi, hp, qi: (bi, qi, hp)),
        compiler_params=pltpu.CompilerParams(
            dimension_semantics=("arbitrary", "arbitrary", "arbitrary"),
            vmem_limit_bytes=VMEM_LIMIT),
        name="mla_prompt",
    )(qa, ka, va)


def _mla_sample_kernel(q_ref, k_ref, v_ref, o_ref):
    v = v_ref[0]
    outs = []
    for e in range(2):
        s = _dot_nt(q_ref[0, :, e * SLAB:(e + 1) * SLAB], k_ref[0, :, e * SLAB:(e + 1) * SLAB])
        p = jnp.exp2(s - jnp.max(s, axis=-1, keepdims=True))
        l = jnp.sum(p, axis=-1, keepdims=True)
        outs.append(_dot(p.astype(BF16), v) / l)
    o_ref[0] = _select_heads(outs[0], outs[1]).astype(o_ref.dtype)


def _mla_sample(qa, ka, va):
    b, t, _ = qa.shape
    n = ka.shape[1]
    return pl.pallas_call(
        _mla_sample_kernel,
        out_shape=jax.ShapeDtypeStruct((b, t, A_WIDTH), BF16),
        grid=(b, A_HEADS // 2),
        in_specs=[pl.BlockSpec((1, t, 2 * SLAB), lambda bi, hp: (bi, 0, hp)),
                  pl.BlockSpec((1, n, 2 * SLAB), lambda bi, hp: (bi, 0, hp)),
                  pl.BlockSpec((1, n, SLAB), lambda bi, hp: (bi, 0, hp))],
        out_specs=pl.BlockSpec((1, t, SLAB), lambda bi, hp: (bi, 0, hp)),
        compiler_params=pltpu.CompilerParams(dimension_semantics=("arbitrary", "arbitrary"),
                                             vmem_limit_bytes=VMEM_LIMIT),
        name="mla_sample",
    )(qa, ka, va)


def _head_lane_masks(width):
    lane = lax.broadcasted_iota(jnp.int32, (1, width), 1)
    return [(lane < B_HDIM), (lane >= B_HDIM)]


def _band_prompt_kernel(q_ref, k0_ref, k1_ref, k2_ref, v0_ref, v1_ref, v2_ref, t_ref, o_ref,
                        *, tq, nkb):
    qi = pl.program_id(2)
    q = q_ref[0]
    ks = [k0_ref[0], k1_ref[0], k2_ref[0]]
    vs = [v0_ref[0], v1_ref[0], v2_ref[0]]
    masks = _head_lane_masks(SLAB)
    outs = []
    for e in range(2):
        qm = jnp.where(masks[e], q, jnp.zeros_like(q))
        ss = []
        for j in range(nkb):
            s = _dot_nt(qm, ks[j]) + t_ref[e, :, j * tq:(j + 1) * tq]
            if j < nkb - 1:
                s = jnp.where(qi + (j - (nkb - 1)) >= 0, s, NEG_INF)
            ss.append(s)
        m = functools.reduce(jnp.maximum, [jnp.max(s, axis=-1, keepdims=True) for s in ss])
        ps = [jnp.exp2(s - m) for s in ss]
        l = sum(jnp.sum(p, axis=-1, keepdims=True) for p in ps)
        o = sum(_dot(p.astype(BF16), v) for p, v in zip(ps, vs))
        outs.append(o / l)
    o_ref[0] = _select_heads(outs[0], outs[1]).astype(o_ref.dtype)


def _band_prompt(qb, kb, vb, table, *, tq):
    b, s, _ = qb.shape
    nkb = B_WINDOW // tq + 1
    assert nkb == 3

    def kv_spec(j):
        return pl.BlockSpec((1, tq, SLAB),
                            lambda hp, bi, qi: (bi, jnp.maximum(qi + (j - (nkb - 1)), 0), hp))

    return pl.pallas_call(
        functools.partial(_band_prompt_kernel, tq=tq, nkb=nkb),
        out_shape=jax.ShapeDtypeStruct((b, s, B_WIDTH), BF16),
        grid=(B_HEADS // 2, b, s // tq),
        in_specs=[pl.BlockSpec((1, tq, SLAB), lambda hp, bi, qi: (bi, qi, hp))]
                 + [kv_spec(j) for j in range(nkb)] + [kv_spec(j) for j in range(nkb)]
                 + [pl.BlockSpec((2, tq, nkb * tq), lambda hp, bi, qi: (hp, 0, 0))],
        out_specs=pl.BlockSpec((1, tq, SLAB), lambda hp, bi, qi: (bi, qi, hp)),
        compiler_params=pltpu.CompilerParams(
            dimension_semantics=("arbitrary", "arbitrary", "arbitrary"),
            vmem_limit_bytes=VMEM_LIMIT),
        name="band_prompt",
    )(qb, kb, kb, kb, vb, vb, vb, table)


def _band_sample_kernel(q_ref, k_ref, v_ref, t_ref, o_ref):
    q = q_ref[0]
    k = k_ref[0]
    v = v_ref[0]
    masks = _head_lane_masks(SLAB)
    outs = []
    for e in range(2):
        qm = jnp.where(masks[e], q, jnp.zeros_like(q))
        s = _dot_nt(qm, k) + t_ref[e]
        p = jnp.exp2(s - jnp.max(s, axis=-1, keepdims=True))
        l = jnp.sum(p, axis=-1, keepdims=True)
        outs.append(_dot(p.astype(BF16), v) / l)
    o_ref[0] = _select_heads(outs[0], outs[1]).astype(o_ref.dtype)


def _band_sample(qb, kb, vb, table):
    b, t, _ = qb.shape
    n = kb.shape[1]
    return pl.pallas_call(
        _band_sample_kernel,
        out_shape=jax.ShapeDtypeStruct((b, t, B_WIDTH), BF16),
        grid=(B_HEADS // 2, b),
        in_specs=[pl.BlockSpec((1, t, SLAB), lambda hp, bi: (bi, 0, hp)),
                  pl.BlockSpec((1, n, SLAB), lambda hp, bi: (bi, 0, hp)),
                  pl.BlockSpec((1, n, SLAB), lambda hp, bi: (bi, 0, hp)),
                  pl.BlockSpec((2, t, n), lambda hp, bi: (hp, 0, 0))],
        out_specs=pl.BlockSpec((1, t, SLAB), lambda hp, bi: (bi, 0, hp)),
        compiler_params=pltpu.CompilerParams(dimension_semantics=("arbitrary", "arbitrary"),
                                             vmem_limit_bytes=VMEM_LIMIT),
        name="band_sample",
    )(qb, kb, vb, table)


def _back_kernel(x_ref, gate_ref, aa_ref, sga_ref, ab_ref, sgb_ref, sma_ref, smb_ref,
                 woa_ref, wob_ref, wout_ref, o_ref, *, per_row):
    gate = gate_ref[...] if per_row else gate_ref[0]
    ua = _dot(aa_ref[...] * sga_ref[...], woa_ref[...])
    ub = _dot(ab_ref[...] * sgb_ref[...], wob_ref[...])
    merged = sma_ref[...].astype(F32) * ua + smb_ref[...].astype(F32) * ub
    o_ref[...] = x_ref[...] + gate * _dot(merged.astype(BF16), wout_ref[...])


def _back(x2d, gate, aa, sga, ab, sgb, sma, smb, woa, wob, wout, *, tm, rows_per_batch, per_row):
    rows = x2d.shape[0]
    row = lambda i: (i, 0)
    fixed = lambda i: (0, 0)
    if per_row:
        gate_spec = pl.BlockSpec((tm, D_MODEL), row)
    else:
        tpb = rows_per_batch // tm
        gate_spec = pl.BlockSpec((1, 1, D_MODEL), lambda i: (i // tpb, 0, 0))
    return pl.pallas_call(
        functools.partial(_back_kernel, per_row=per_row),
        out_shape=jax.ShapeDtypeStruct((rows, D_MODEL), F32),
        grid=(rows // tm,),
        in_specs=[pl.BlockSpec((tm, D_MODEL), row), gate_spec,
                  pl.BlockSpec((tm, A_WIDTH), row), pl.BlockSpec((tm, A_WIDTH), row),
                  pl.BlockSpec((tm, B_WIDTH), row), pl.BlockSpec((tm, B_WIDTH), row),
                  pl.BlockSpec((tm, D_MODEL), row), pl.BlockSpec((tm, D_MODEL), row),
                  pl.BlockSpec(woa.shape, fixed), pl.BlockSpec(wob.shape, fixed),
                  pl.BlockSpec(wout.shape, fixed)],
        out_specs=pl.BlockSpec((tm, D_MODEL), row),
        compiler_params=pltpu.CompilerParams(dimension_semantics=("arbitrary",),
                                             vmem_limit_bytes=VMEM_LIMIT),
        name="back",
    )(x2d, gate, aa, sga, ab, sgb, sma, smb, woa, wob, wout)


def _seg_matrix(group_of_lane, sizes):
    lane = np.arange(2 * SLAB)
    slab = lane // SLAB
    grp = group_of_lane[lane % SLAB]
    same = (slab[:, None] == slab[None, :]) & (grp[:, None] == grp[None, :])
    return jnp.asarray(np.where(same, 1.0 / sizes[grp][None, :], 0.0), dtype=BF16)


def _rope_tables(pos):
    inv = ROPE_BASE ** (-jnp.arange(0, A_ROPE, 2, dtype=F32) / A_ROPE)
    ang = pos.astype(F32)[:, None] * inv[None, :]
    cos, sin = jnp.cos(ang), jnp.sin(ang)
    n = pos.shape[0]
    ones = jnp.ones((n, A_NOPE), F32)
    zeros = jnp.zeros((n, A_NOPE), F32)
    z16 = jnp.zeros((n, HALF_ROPE), F32)
    pad1 = jnp.ones((n, SLAB - A_QK), F32)
    pad0 = jnp.zeros((n, SLAB - A_QK), F32)
    cos_t = jnp.concatenate([ones, cos, cos, pad1], axis=1)
    sin_lo = jnp.concatenate([zeros, -sin, z16, pad0], axis=1)
    sin_hi = jnp.concatenate([zeros, z16, sin, pad0], axis=1)
    return cos_t, sin_lo, sin_hi


def _band_table(rel_bias, q_pos, k_pos, allowed):
    idx = np.clip(q_pos[:, None] - k_pos[None, :], -B_MAX_REL, B_MAX_REL) + B_MAX_REL
    bias = rel_bias[:, idx].astype(F32) * LOG2E
    return jnp.where(jnp.asarray(allowed)[None], bias, NEG_INF)


def kernel(x_prompt, x_sample, cache_mla_latent, cache_mla_krope, cache_band_k, cache_band_v,
           c_prompt, c_sample, g_norm, w_ada, b_ada, w_in, g_q_lat, w_uq, g_kv_lat, w_uk, w_uv,
           g_qn_a, g_qr_a, g_kn_a, g_kr_a, g_q_b, g_k_b, rel_bias_b, w_oa, w_ob, w_out):
    bp, s, _ = x_prompt.shape
    bs, t, _ = x_sample.shape
    past = cache_mla_latent.shape[2]
    win_s = cache_band_k.shape[2]
    win_p = min(B_WINDOW, s)
    depth = g_norm.shape[0]
    assert depth == 1

    seg_q = _seg_matrix(np.where(np.arange(SLAB) < A_NOPE, 0, np.where(np.arange(SLAB) < A_QK, 1, 2)),
                        np.array([A_NOPE, A_ROPE, SLAB - A_QK], np.float64))
    seg_64 = _seg_matrix(np.arange(SLAB) // B_HDIM, np.array([B_HDIM, B_HDIM], np.float64))
    seg_k = _seg_matrix(np.where(np.arange(SLAB) < A_NOPE, 0, 1),
                        np.array([A_NOPE, SLAB - A_NOPE], np.float64))

    tq_band = 256
    q_loc = np.arange(tq_band)
    k_loc = np.arange(3 * tq_band) - B_WINDOW
    kc, qc = k_loc // CHUNK, q_loc // CHUNK
    allowed_p = (kc[None, :] <= qc[:, None]) & (kc[None, :] >= qc[:, None] - B_LEFT_CHUNKS)
    pos_s = past + np.arange(t)
    kpos_s = np.concatenate([past - win_s + np.arange(win_s), pos_s])
    allowed_s = np.ones((t, win_s + t), bool)

    l = 0
    zpad = jnp.zeros((D_MODEL, SLAB - A_QK), F32)
    z64 = jnp.zeros((D_MODEL, A_NOPE), F32)
    win = w_in[l]
    c_kr0 = A_Q_RANK + A_KV_RANK
    win_p_ = jnp.concatenate([win[:, :c_kr0], z64, win[:, c_kr0:c_kr0 + A_ROPE], zpad,
                              win[:, c_kr0 + A_ROPE:]], axis=1).astype(BF16)
    wuq_p = jnp.pad(w_uq[l], ((0, 0), (0, 0), (0, SLAB - A_QK))).reshape(A_Q_RANK, -1).astype(BF16)
    wuk_p = jnp.pad(w_uk[l], ((0, 0), (0, 0), (0, SLAB - A_NOPE))).reshape(A_KV_RANK, -1).astype(BF16)
    wuv_b = w_uv[l].reshape(A_KV_RANK, -1).astype(BF16)
    qscale = A_SCALE * LOG2E
    gq = jnp.tile(jnp.concatenate([g_qn_a[l], g_qr_a[l], jnp.zeros((SLAB - A_QK,), F32)]) * qscale,
                  A_HEADS)[None]
    gk = jnp.tile(jnp.concatenate([g_kn_a[l], jnp.zeros((SLAB - A_NOPE,), F32)]), A_HEADS)[None]
    gkr = jnp.concatenate([jnp.zeros((A_NOPE,), F32), g_kr_a[l], jnp.zeros((SLAB - A_QK,), F32)])[None]
    gqb = jnp.tile(g_q_b[l] * (B_SCALE * LOG2E), B_HEADS)[None]
    gkb = jnp.tile(g_k_b[l], B_HEADS)[None]
    consts = (g_norm[l][None], win_p_, g_q_lat[l][None], wuq_p, g_kv_lat[l][None], gq, gkr,
              gqb, gkb, seg_q, seg_64)
    woa, wob, wout = w_oa[l].astype(BF16), w_ob[l].astype(BF16), w_out[l].astype(BF16)

    c_rows = jnp.concatenate([c_prompt, jnp.zeros((8 - bp, D_MODEL), F32),
                              jnp.repeat(c_sample, t, axis=0)], axis=0)
    mod = _adaln(c_rows, w_ada[l], b_ada[l])
    shift, scale, gate = mod[:, :D_MODEL], mod[:, D_MODEL:2 * D_MODEL], mod[:, 2 * D_MODEL:]

    tm = 256
    xp2d = x_prompt.reshape(bp * s, D_MODEL)
    (qa, lat, kr, krs, sga, qb, kb, vb, kbf, vbf, sgb, sma, smb) = _front(
        xp2d, scale[:bp, None], shift[:bp, None], _rope_tables(jnp.arange(s)), consts,
        tm=tm, rows_per_batch=s, per_row=False)
    ka, va = _expand(lat, krs, wuk_p, wuv_b, gk, seg_k, tm=512)
    attn_a = _mla_prompt(qa.reshape(bp, s, -1), ka.reshape(bp, s, -1), va.reshape(bp, s, -1), tq=512)
    table_p = _band_table(rel_bias_b[l], q_loc, k_loc, allowed_p)
    attn_b = _band_prompt(qb.reshape(bp, s, -1), kb.reshape(bp, s, -1), vb.reshape(bp, s, -1),
                          table_p, tq=tq_band)
    y_prompt = _back(xp2d, gate[:bp, None], attn_a.reshape(bp * s, -1), sga,
                     attn_b.reshape(bp * s, -1), sgb, sma, smb, woa, wob, wout,
                     tm=512, rows_per_batch=s, per_row=False).reshape(bp, s, D_MODEL)

    xs2d = x_sample.reshape(bs * t, D_MODEL)
    rows_s = bs * t
    pos_tab = _rope_tables(jnp.asarray(pos_s))
    pos_tab = tuple(jnp.tile(a, (bs, 1)) for a in pos_tab)
    (qa2, lat2, kr2, krs2, sga2, qb2, kb2, vb2, kbf2, vbf2, sgb2, sma2, smb2) = _front(
        xs2d, scale[8:], shift[8:], pos_tab, consts, tm=rows_s, rows_per_batch=t, per_row=True)
    lat_all = jnp.concatenate([cache_mla_latent[l], lat2.reshape(bs, t, -1)], axis=1)
    krs_cache = jnp.pad(cache_mla_krope[l], ((0, 0), (0, 0), (A_NOPE, SLAB - A_QK))).astype(BF16)
    krs_all = jnp.concatenate([krs_cache, krs2.reshape(bs, t, -1)], axis=1)
    n_all = past + t
    ka2, va2 = _expand(lat_all.reshape(bs * n_all, -1), krs_all.reshape(bs * n_all, -1),
                       wuk_p, wuv_b, gk, seg_k, tm=256)
    attn_a2 = _mla_sample(qa2.reshape(bs, t, -1), ka2.reshape(bs, n_all, -1), va2.reshape(bs, n_all, -1))
    kb_all = jnp.concatenate([cache_band_k[l].reshape(bs, win_s, -1).astype(BF16),
                              kb2.reshape(bs, t, -1)], axis=1)
    vb_all = jnp.concatenate([cache_band_v[l].reshape(bs, win_s, -1).astype(BF16),
                              vb2.reshape(bs, t, -1)], axis=1)
    table_s = _band_table(rel_bias_b[l], pos_s, kpos_s, allowed_s)
    attn_b2 = _band_sample(qb2.reshape(bs, t, -1), kb_all, vb_all, table_s)
    y_sample = _back(xs2d, gate[8:], attn_a2.reshape(rows_s, -1), sga2, attn_b2.reshape(rows_s, -1),
                     sgb2, sma2, smb2, woa, wob, wout,
                     tm=rows_s, rows_per_batch=t, per_row=True).reshape(bs, t, D_MODEL)

    return (y_prompt, y_sample,
            lat.reshape(1, bp, s, A_KV_RANK), kr.reshape(1, bp, s, A_ROPE),
            kbf.reshape(bp, s, B_HEADS, B_HDIM)[None, :, s - win_p:],
            vbf.reshape(bp, s, B_HEADS, B_HDIM)[None, :, s - win_p:],
            lat2.reshape(1, bs, t, A_KV_RANK), kr2.reshape(1, bs, t, A_ROPE),
            kbf2.reshape(1, bs, t, B_HEADS, B_HDIM), vbf2.reshape(1, bs, t, B_HEADS, B_HDIM))
```

```python
import functools
import math

import jax
import jax.numpy as jnp
import numpy as np
from jax import lax
from jax.experimental import pallas as pl
from jax.experimental.pallas import tpu as pltpu

F32 = jnp.float32
BF16 = jnp.bfloat16

D_MODEL = 1024
CHUNK = 64
A_HEADS = 8
A_NOPE = 64
A_ROPE = 32
A_VDIM = 64
A_QK = A_NOPE + A_ROPE
A_Q_RANK = 384
A_KV_RANK = 256
A_WIDTH = A_HEADS * A_VDIM
A_SCALE = A_QK ** -0.5
B_HEADS = 8
B_HDIM = 64
B_WIDTH = B_HEADS * B_HDIM
B_LEFT_CHUNKS = 8
B_WINDOW = B_LEFT_CHUNKS * CHUNK
B_MAX_REL = 256
B_SCALE = B_HDIM ** -0.5
ROPE_BASE = 10000.0
NORM_EPS = 1e-6
NEG_INF = -1e30
LOG2E = math.log2(math.e)

LANES = 128
SLAB = LANES
HALF_ROPE = A_ROPE // 2
ROPE_LANE0 = A_NOPE
VMEM_LIMIT = 48 * 1024 * 1024

C_QLAT = 0
C_KVLAT = C_QLAT + A_Q_RANK
C_KR = C_KVLAT + A_KV_RANK
C_GA = C_KR + SLAB
C_QB = C_GA + A_WIDTH
C_KB = C_QB + B_WIDTH
C_VB = C_KB + B_WIDTH
C_GB = C_VB + B_WIDTH
C_MA = C_GB + B_WIDTH
C_MB = C_MA + D_MODEL
C_END = C_MB + D_MODEL

NT = (((1,), (1,)), ((), ()))


def _dot(a, b):
    return jnp.dot(a, b, preferred_element_type=F32)


def _dot_nt(a, b):
    return lax.dot_general(a, b, NT, preferred_element_type=F32)


def _rms_full(x, g):
    ms = jnp.mean(x * x, axis=-1, keepdims=True)
    return x * lax.rsqrt(ms + NORM_EPS) * g


def _seg_rms(x, seg, g):
    ms = _dot((x * x).astype(BF16), seg)
    return x * lax.rsqrt(ms + NORM_EPS) * g


def _rope_slab(x, cos_t, sin_lo, sin_hi):
    return (x * cos_t
            + pltpu.roll(x, SLAB - HALF_ROPE, axis=1) * sin_lo
            + pltpu.roll(x, HALF_ROPE, axis=1) * sin_hi)


def _adaln_kernel(c_ref, w_ref, b_ref, o_ref):
    c = c_ref[...]
    sc = c * jax.nn.sigmoid(c)
    o_ref[...] = jnp.dot(sc, w_ref[...], preferred_element_type=F32,
                         precision=lax.Precision.HIGHEST) + b_ref[...]


def _adaln(c_rows, w_ada, b_ada):
    n = c_rows.shape[0]
    tn = 1024
    return pl.pallas_call(
        _adaln_kernel,
        out_shape=jax.ShapeDtypeStruct((n, 3 * D_MODEL), F32),
        grid=(3 * D_MODEL // tn,),
        in_specs=[pl.BlockSpec((n, D_MODEL), lambda j: (0, 0)),
                  pl.BlockSpec((D_MODEL, tn), lambda j: (0, j)),
                  pl.BlockSpec((1, tn), lambda j: (0, j))],
        out_specs=pl.BlockSpec((n, tn), lambda j: (0, j)),
        compiler_params=pltpu.CompilerParams(dimension_semantics=("arbitrary",),
                                             vmem_limit_bytes=VMEM_LIMIT),
        name="adaln",
    )(c_rows, w_ada, b_ada.reshape(1, -1))


def _front_kernel(x_ref, scale_ref, shift_ref, cos_ref, slo_ref, shi_ref,
                  gnorm_ref, win_ref, gql_ref, wuq_ref, gkv_ref, gq_ref, gkr_ref,
                  gqb_ref, gkb_ref, segq_ref, seg64_ref,
                  qa_ref, lat_ref, kr_ref, krs_ref, sga_ref, qb_ref, kb_ref, vb_ref,
                  kbf_ref, vbf_ref, sgb_ref, sma_ref, smb_ref, *, per_row):
    x = x_ref[...]
    if per_row:
        scale, shift = scale_ref[...], shift_ref[...]
    else:
        scale, shift = scale_ref[0], shift_ref[0]
    h = _rms_full(x, gnorm_ref[...]) * (1.0 + scale) + shift
    hb = h.astype(BF16)
    cos_t, sin_lo, sin_hi = cos_ref[...], slo_ref[...], shi_ref[...]

    def proj(c0, c1):
        return _dot(hb, win_ref[:, c0:c1])

    r = _rms_full(proj(C_QLAT, C_KVLAT), gql_ref[...]).astype(BF16)
    segq = segq_ref[...]
    for p in range(A_HEADS // 2):
        c0 = 2 * SLAB * p
        qa = _dot(r, wuq_ref[:, c0:c0 + 2 * SLAB])
        qa = _seg_rms(qa, segq, gq_ref[:, c0:c0 + 2 * SLAB])
        for s in range(2):
            slab = _rope_slab(qa[:, s * SLAB:(s + 1) * SLAB], cos_t, sin_lo, sin_hi)
            qa_ref[:, c0 + s * SLAB:c0 + (s + 1) * SLAB] = slab.astype(BF16)

    lat_ref[...] = _rms_full(proj(C_KVLAT, C_KR), gkv_ref[...])
    krs = proj(C_KR, C_GA)
    ms = jnp.sum(krs * krs, axis=-1, keepdims=True) * (1.0 / A_ROPE)
    krs = _rope_slab(krs * lax.rsqrt(ms + NORM_EPS) * gkr_ref[...], cos_t, sin_lo, sin_hi)
    krs_ref[...] = krs.astype(BF16)
    kr_ref[...] = pltpu.roll(krs, SLAB - ROPE_LANE0, axis=1)[:, :A_ROPE]

    ga = proj(C_GA, C_QB)
    sga_ref[...] = (ga * jax.nn.sigmoid(ga)).astype(BF16)
    gb = proj(C_GB, C_MA)
    sgb_ref[...] = (gb * jax.nn.sigmoid(gb)).astype(BF16)
    sma_ref[...] = jax.nn.sigmoid(proj(C_MA, C_MB)).astype(BF16)
    smb_ref[...] = jax.nn.sigmoid(proj(C_MB, C_END)).astype(BF16)

    seg64 = seg64_ref[...]
    for p in range(B_WIDTH // (2 * SLAB)):
        c0 = 2 * SLAB * p
        qb = proj(C_QB + c0, C_QB + c0 + 2 * SLAB)
        qb_ref[:, c0:c0 + 2 * SLAB] = _seg_rms(qb, seg64, gqb_ref[:, c0:c0 + 2 * SLAB]).astype(BF16)
        kb = proj(C_KB + c0, C_KB + c0 + 2 * SLAB)
        kb = _seg_rms(kb, seg64, gkb_ref[:, c0:c0 + 2 * SLAB])
        kbf_ref[:, c0:c0 + 2 * SLAB] = kb
        kb_ref[:, c0:c0 + 2 * SLAB] = kb.astype(BF16)
    vb = proj(C_VB, C_GB)
    vbf_ref[...] = vb
    vb_ref[...] = vb.astype(BF16)


def _front(x2d, scale, shift, rope_tabs, consts, *, tm, rows_per_batch, per_row, tail_rows):
    rows = x2d.shape[0]
    nt = rows // tm
    row = lambda i: (i, 0)
    fixed = lambda i: (0, 0)
    tpb = rows_per_batch // tm
    if per_row:
        mod_spec = pl.BlockSpec((tm, D_MODEL), row)
        tab_spec = pl.BlockSpec((tm, SLAB), row)
    else:
        mod_spec = pl.BlockSpec((1, 1, D_MODEL), lambda i: (i // tpb, 0, 0))
        tab_spec = pl.BlockSpec((tm, SLAB), lambda i: (i % tpb, 0))

    def full(a):
        return pl.BlockSpec(a.shape, fixed)

    def out(width, dtype):
        return jax.ShapeDtypeStruct((rows, width), dtype), pl.BlockSpec((tm, width), row)

    if tail_rows == rows_per_batch:
        tail = out(B_WIDTH, F32)
    else:
        ntail = tail_rows // tm
        tail = (jax.ShapeDtypeStruct((rows // rows_per_batch * tail_rows, B_WIDTH), F32),
                pl.BlockSpec((tm, B_WIDTH),
                             lambda i: (i // tpb * ntail + jnp.maximum(i % tpb - (tpb - ntail), 0), 0)))

    outs = [out(A_HEADS * SLAB, BF16),
            out(A_KV_RANK, F32),
            out(A_ROPE, F32),
            out(SLAB, BF16),
            out(A_WIDTH, BF16),
            out(B_WIDTH, BF16),
            out(B_WIDTH, BF16),
            out(B_WIDTH, BF16),
            tail,
            tail,
            out(B_WIDTH, BF16),
            out(D_MODEL, BF16),
            out(D_MODEL, BF16)]
    return pl.pallas_call(
        functools.partial(_front_kernel, per_row=per_row),
        out_shape=[o[0] for o in outs],
        grid=(nt,),
        in_specs=[pl.BlockSpec((tm, D_MODEL), row), mod_spec, mod_spec,
                  tab_spec, tab_spec, tab_spec] + [full(a) for a in consts],
        out_specs=[o[1] for o in outs],
        compiler_params=pltpu.CompilerParams(dimension_semantics=("arbitrary",),
                                             vmem_limit_bytes=VMEM_LIMIT),
        name="front",
    )(x2d, scale, shift, *rope_tabs, *consts)


def _expand_kernel(lat_ref, krs_ref, wuk_ref, wuv_ref, gk_ref, segk_ref, ka_ref, va_ref,
                   *, v_transposed):
    latb = lat_ref[...].astype(BF16)
    krs = krs_ref[...].astype(F32)
    segk = segk_ref[...]
    for p in range(A_HEADS // 2):
        c0 = 2 * SLAB * p
        kn = _dot(latb, wuk_ref[:, c0:c0 + 2 * SLAB])
        kn = _seg_rms(kn, segk, gk_ref[:, c0:c0 + 2 * SLAB])
        for s in range(2):
            ka_ref[:, c0 + s * SLAB:c0 + (s + 1) * SLAB] = (
                kn[:, s * SLAB:(s + 1) * SLAB] + krs).astype(BF16)
    if v_transposed:
        vt = _dot_nt(wuv_ref[...], latb).astype(BF16)
        for hp in range(A_HEADS // 2):
            va_ref[0, hp, 0] = vt[hp * SLAB:(hp + 1) * SLAB, :]
    else:
        va_ref[...] = _dot(latb, wuv_ref[...]).astype(BF16)


def _expand(lat2d, krs2d, wuk_p, wuv, gk, segk, *, tm, rows_per_batch, v_transposed):
    rows = lat2d.shape[0]
    row = lambda i: (i, 0)
    fixed = lambda i: (0, 0)
    if v_transposed:
        tpb = rows_per_batch // tm
        v_shape = jax.ShapeDtypeStruct((rows // rows_per_batch, A_HEADS // 2, tpb, SLAB, tm), BF16)
        v_spec = pl.BlockSpec((1, A_HEADS // 2, 1, SLAB, tm), lambda i: (i // tpb, 0, i % tpb, 0, 0))
    else:
        v_shape = jax.ShapeDtypeStruct((rows, A_WIDTH), BF16)
        v_spec = pl.BlockSpec((tm, A_WIDTH), row)
    return pl.pallas_call(
        functools.partial(_expand_kernel, v_transposed=v_transposed),
        out_shape=[jax.ShapeDtypeStruct((rows, A_HEADS * SLAB), BF16), v_shape],
        grid=(rows // tm,),
        in_specs=[pl.BlockSpec((tm, A_KV_RANK), row), pl.BlockSpec((tm, SLAB), row),
                  pl.BlockSpec(wuk_p.shape, fixed), pl.BlockSpec(wuv.shape, fixed),
                  pl.BlockSpec(gk.shape, fixed), pl.BlockSpec(segk.shape, fixed)],
        out_specs=[pl.BlockSpec((tm, A_HEADS * SLAB), row), v_spec],
        compiler_params=pltpu.CompilerParams(dimension_semantics=("arbitrary",),
                                             vmem_limit_bytes=VMEM_LIMIT),
        name="expand",
    )(lat2d, krs2d, wuk_p, wuv, gk, segk)


def _select_heads(o_even, o_odd):
    lane = lax.broadcasted_iota(jnp.int32, o_even.shape, 1)
    return jnp.where(lane < A_VDIM, o_even, o_odd)


def _mla_prompt_kernel(q_ref, k_ref, vt_ref, o_ref, *, tq):
    qi = pl.program_id(2)
    nsub = 2
    chunk_delta = (lax.broadcasted_iota(jnp.int32, (tq, tq), 0) // CHUNK
                   - lax.broadcasted_iota(jnp.int32, (tq, tq), 1) // CHUNK)
    ones = jnp.ones((16, tq), BF16)
    chains = [(e, h) for h in range(nsub) for e in range(2)]

    def step(j, carry, active, masked):
        k0 = pl.multiple_of(j * tq, tq)
        scores = {}
        for c in active:
            e, h = chains[c]
            k = k_ref[0, pl.ds(k0, tq), e * SLAB:(e + 1) * SLAB]
            q = q_ref[0, h * tq:(h + 1) * tq, e * SLAB:(e + 1) * SLAB]
            scores[c] = _dot_nt(k, q)
        probs = {}
        new = list(carry)
        for c in active:
            e, h = chains[c]
            m, acc = carry[c]
            s = scores[c]
            if masked:
                s = jnp.where(chunk_delta <= (nsub * qi + h - j) * (tq // CHUNK), s, NEG_INF)
            m_new = jnp.maximum(m, jnp.max(s, axis=0, keepdims=True))
            probs[c] = (m_new, jnp.exp2(m - m_new), jnp.exp2(s - m_new).astype(BF16))
        for c in active:
            e, h = chains[c]
            m_new, alpha, p = probs[c]
            vt = vt_ref[0, 0, j, e * A_VDIM:(e + 1) * A_VDIM, :]
            new[c] = (m_new, alpha * carry[c][1] + _dot(jnp.concatenate([vt, ones], axis=0), p))
        return tuple(new)

    init = tuple((jnp.full((1, tq), NEG_INF, F32), jnp.zeros((A_VDIM + 16, tq), F32))
                 for _ in chains)
    everyone = list(range(len(chains)))
    carry = lax.fori_loop(0, nsub * qi, lambda j, c: step(j, c, everyone, False), init)
    for h0 in range(nsub):
        carry = step(nsub * qi + h0, carry, [c for c in everyone if chains[c][1] >= h0], True)
    for h in range(nsub):
        outs = [carry[2 * h + e][1] for e in range(2)]
        outs = [acc[:A_VDIM] / acc[A_VDIM:A_VDIM + 1] for acc in outs]
        o_ref[0, h * tq:(h + 1) * tq, :] = jnp.concatenate(outs, axis=0).T.astype(o_ref.dtype)


def _mla_prompt(qa, ka, vt, *, tq):
    b, s, _ = qa.shape
    assert vt.shape == (b, A_HEADS // 2, s // tq, SLAB, tq)
    tqq = 2 * tq
    return pl.pallas_call(
        functools.partial(_mla_prompt_kernel, tq=tq),
        out_shape=jax.ShapeDtypeStruct((b, s, A_WIDTH), BF16),
        grid=(b, A_HEADS // 2, s // tqq),
        in_specs=[pl.BlockSpec((1, tqq, 2 * SLAB), lambda bi, hp, qi: (bi, qi, hp)),
                  pl.BlockSpec((1, s, 2 * SLAB), lambda bi, hp, qi: (bi, 0, hp)),
                  pl.BlockSpec((1, 1, s // tq, SLAB, tq), lambda bi, hp, qi: (bi, hp, 0, 0, 0))],
        out_specs=pl.BlockSpec((1, tqq, SLAB), lambda bi, hp, qi: (bi, qi, hp)),
        compiler_params=pltpu.CompilerParams(
            dimension_semantics=("arbitrary", "arbitrary", "arbitrary"),
            vmem_limit_bytes=VMEM_LIMIT),
        name="mla_prompt",
    )(qa, ka, vt)


def _mla_sample_kernel(q_ref, k_ref, v_ref, o_ref):
    v = v_ref[0]
    outs = []
    for e in range(2):
        s = _dot_nt(q_ref[0, :, e * SLAB:(e + 1) * SLAB], k_ref[0, :, e * SLAB:(e + 1) * SLAB])
        p = jnp.exp2(s - jnp.max(s, axis=-1, keepdims=True))
        l = jnp.sum(p, axis=-1, keepdims=True)
        outs.append(_dot(p.astype(BF16), v) / l)
    o_ref[0] = _select_heads(outs[0], outs[1]).astype(o_ref.dtype)


def _mla_sample(qa, ka, va):
    b, t, _ = qa.shape
    n = ka.shape[1]
    return pl.pallas_call(
        _mla_sample_kernel,
        out_shape=jax.ShapeDtypeStruct((b, t, A_WIDTH), BF16),
        grid=(b, A_HEADS // 2),
        in_specs=[pl.BlockSpec((1, t, 2 * SLAB), lambda bi, hp: (bi, 0, hp)),
                  pl.BlockSpec((1, n, 2 * SLAB), lambda bi, hp: (bi, 0, hp)),
                  pl.BlockSpec((1, n, SLAB), lambda bi, hp: (bi, 0, hp))],
        out_specs=pl.BlockSpec((1, t, SLAB), lambda bi, hp: (bi, 0, hp)),
        compiler_params=pltpu.CompilerParams(dimension_semantics=("arbitrary", "arbitrary"),
                                             vmem_limit_bytes=VMEM_LIMIT),
        name="mla_sample",
    )(qa, ka, va)


BAND_TQ = 256
BAND_KEYS = B_WINDOW + BAND_TQ
TOEPLITZ_W = 1024


def _band_table_kernel(rb_ref, onehot_ref, allowed_ref, tb_ref, tp_ref):
    g = jnp.dot(rb_ref[...], onehot_ref[...], preferred_element_type=F32,
                precision=lax.Precision.HIGHEST) * LOG2E
    allowed = allowed_ref[...] > 0.0
    for h in range(B_HEADS):
        row = jnp.broadcast_to(g[h:h + 1, :], (BAND_TQ, TOEPLITZ_W))
        t = pltpu.roll(row, 0, axis=1, stride=1, stride_axis=0)[:, :BAND_KEYS]
        tb_ref[h] = t
        tp_ref[h] = jnp.where(allowed, t, NEG_INF)


def _band_tables(rel_bias):
    n_rel = rel_bias.shape[1]
    n_pad = -(-n_rel // LANES) * LANES
    x = np.arange(TOEPLITZ_W)
    key_minus_query = np.where(x < BAND_KEYS, x, x - TOEPLITZ_W)
    dist = B_WINDOW - key_minus_query
    idx = np.clip(dist, -B_MAX_REL, B_MAX_REL) + B_MAX_REL
    onehot = np.zeros((n_pad, TOEPLITZ_W), np.float32)
    onehot[idx, x] = 1.0
    q_chunk = np.arange(BAND_TQ) // CHUNK + B_LEFT_CHUNKS
    k_chunk = np.arange(BAND_KEYS) // CHUNK
    allowed = (k_chunk[None, :] <= q_chunk[:, None]) & (k_chunk[None, :] >= q_chunk[:, None] - B_LEFT_CHUNKS)
    rb = jnp.pad(rel_bias, ((0, 0), (0, n_pad - n_rel)))
    shape = jax.ShapeDtypeStruct((B_HEADS, BAND_TQ, BAND_KEYS), F32)
    return pl.pallas_call(
        _band_table_kernel,
        out_shape=[shape, shape],
        compiler_params=pltpu.CompilerParams(vmem_limit_bytes=VMEM_LIMIT),
        name="band_table",
    )(rb, jnp.asarray(onehot), jnp.asarray(allowed.astype(np.float32)))


def _head_lane_masks(width):
    lane = lax.broadcasted_iota(jnp.int32, (1, width), 1)
    return [(lane < B_HDIM), (lane >= B_HDIM)]


def _band_prompt_kernel(q_ref, k0_ref, k1_ref, k2_ref, v0_ref, v1_ref, v2_ref, t_ref, o_ref,
                        *, tq, nkb):
    qi = pl.program_id(2)
    q = q_ref[0]
    ks = [k0_ref[0], k1_ref[0], k2_ref[0]]
    vs = [v0_ref[0], v1_ref[0], v2_ref[0]]
    masks = _head_lane_masks(SLAB)
    outs = []
    for e in range(2):
        qm = jnp.where(masks[e], q, jnp.zeros_like(q))
        ss = []
        for j in range(nkb):
            s = _dot_nt(qm, ks[j]) + t_ref[e, :, j * tq:(j + 1) * tq]
            if j < nkb - 1:
                s = jnp.where(qi + (j - (nkb - 1)) >= 0, s, NEG_INF)
            ss.append(s)
        m = functools.reduce(jnp.maximum, [jnp.max(s, axis=-1, keepdims=True) for s in ss])
        ps = [jnp.exp2(s - m) for s in ss]
        l = sum(jnp.sum(p, axis=-1, keepdims=True) for p in ps)
        o = sum(_dot(p.astype(BF16), v) for p, v in zip(ps, vs))
        outs.append(o / l)
    o_ref[0] = _select_heads(outs[0], outs[1]).astype(o_ref.dtype)


def _band_prompt(qb, kb, vb, table, *, tq):
    b, s, _ = qb.shape
    nkb = B_WINDOW // tq + 1
    assert nkb == 3

    def kv_spec(j):
        return pl.BlockSpec((1, tq, SLAB),
                            lambda hp, bi, qi: (bi, jnp.maximum(qi + (j - (nkb - 1)), 0), hp))

    return pl.pallas_call(
        functools.partial(_band_prompt_kernel, tq=tq, nkb=nkb),
        out_shape=jax.ShapeDtypeStruct((b, s, B_WIDTH), BF16),
        grid=(B_HEADS // 2, b, s // tq),
        in_specs=[pl.BlockSpec((1, tq, SLAB), lambda hp, bi, qi: (bi, qi, hp))]
                 + [kv_spec(j) for j in range(nkb)] + [kv_spec(j) for j in range(nkb)]
                 + [pl.BlockSpec((2, tq, nkb * tq), lambda hp, bi, qi: (hp, 0, 0))],
        out_specs=pl.BlockSpec((1, tq, SLAB), lambda hp, bi, qi: (bi, qi, hp)),
        compiler_params=pltpu.CompilerParams(
            dimension_semantics=("arbitrary", "arbitrary", "arbitrary"),
            vmem_limit_bytes=VMEM_LIMIT),
        name="band_prompt",
    )(qb, kb, kb, kb, vb, vb, vb, table)


def _band_sample_kernel(q_ref, k_ref, v_ref, t_ref, o_ref):
    q = q_ref[0]
    k = k_ref[0]
    v = v_ref[0]
    masks = _head_lane_masks(SLAB)
    outs = []
    for e in range(2):
        qm = jnp.where(masks[e], q, jnp.zeros_like(q))
        s = _dot_nt(qm, k) + t_ref[e, :, :k.shape[0]]
        p = jnp.exp2(s - jnp.max(s, axis=-1, keepdims=True))
        l = jnp.sum(p, axis=-1, keepdims=True)
        outs.append(_dot(p.astype(BF16), v) / l)
    o_ref[0] = _select_heads(outs[0], outs[1]).astype(o_ref.dtype)


def _band_sample(qb, kb, vb, table):
    b, t, _ = qb.shape
    n = kb.shape[1]
    return pl.pallas_call(
        _band_sample_kernel,
        out_shape=jax.ShapeDtypeStruct((b, t, B_WIDTH), BF16),
        grid=(B_HEADS // 2, b),
        in_specs=[pl.BlockSpec((1, t, SLAB), lambda hp, bi: (bi, 0, hp)),
                  pl.BlockSpec((1, n, SLAB), lambda hp, bi: (bi, 0, hp)),
                  pl.BlockSpec((1, n, SLAB), lambda hp, bi: (bi, 0, hp)),
                  pl.BlockSpec((2, t, BAND_KEYS), lambda hp, bi: (hp, 0, 0))],
        out_specs=pl.BlockSpec((1, t, SLAB), lambda hp, bi: (bi, 0, hp)),
        compiler_params=pltpu.CompilerParams(dimension_semantics=("arbitrary", "arbitrary"),
                                             vmem_limit_bytes=VMEM_LIMIT),
        name="band_sample",
    )(qb, kb, vb, table)


def _back_kernel(x_ref, gate_ref, aa_ref, sga_ref, ab_ref, sgb_ref, sma_ref, smb_ref,
                 woa_ref, wob_ref, wout_ref, o_ref, *, per_row):
    gate = gate_ref[...] if per_row else gate_ref[0]
    ua = _dot(aa_ref[...] * sga_ref[...], woa_ref[...])
    ub = _dot(ab_ref[...] * sgb_ref[...], wob_ref[...])
    merged = sma_ref[...].astype(F32) * ua + smb_ref[...].astype(F32) * ub
    o_ref[...] = x_ref[...] + gate * _dot(merged.astype(BF16), wout_ref[...])


def _back(x2d, gate, aa, sga, ab, sgb, sma, smb, woa, wob, wout, *, tm, rows_per_batch, per_row):
    rows = x2d.shape[0]
    row = lambda i: (i, 0)
    fixed = lambda i: (0, 0)
    if per_row:
        gate_spec = pl.BlockSpec((tm, D_MODEL), row)
    else:
        tpb = rows_per_batch // tm
        gate_spec = pl.BlockSpec((1, 1, D_MODEL), lambda i: (i // tpb, 0, 0))
    return pl.pallas_call(
        functools.partial(_back_kernel, per_row=per_row),
        out_shape=jax.ShapeDtypeStruct((rows, D_MODEL), F32),
        grid=(rows // tm,),
        in_specs=[pl.BlockSpec((tm, D_MODEL), row), gate_spec,
                  pl.BlockSpec((tm, A_WIDTH), row), pl.BlockSpec((tm, A_WIDTH), row),
                  pl.BlockSpec((tm, B_WIDTH), row), pl.BlockSpec((tm, B_WIDTH), row),
                  pl.BlockSpec((tm, D_MODEL), row), pl.BlockSpec((tm, D_MODEL), row),
                  pl.BlockSpec(woa.shape, fixed), pl.BlockSpec(wob.shape, fixed),
                  pl.BlockSpec(wout.shape, fixed)],
        out_specs=pl.BlockSpec((tm, D_MODEL), row),
        compiler_params=pltpu.CompilerParams(dimension_semantics=("arbitrary",),
                                             vmem_limit_bytes=VMEM_LIMIT),
        name="back",
    )(x2d, gate, aa, sga, ab, sgb, sma, smb, woa, wob, wout)


def _seg_matrix(group_of_lane, sizes):
    lane = np.arange(2 * SLAB)
    slab = lane // SLAB
    grp = group_of_lane[lane % SLAB]
    same = (slab[:, None] == slab[None, :]) & (grp[:, None] == grp[None, :])
    return jnp.asarray(np.where(same, 1.0 / sizes[grp][None, :], 0.0), dtype=BF16)


def _rope_tables(pos):
    inv = ROPE_BASE ** (-jnp.arange(0, A_ROPE, 2, dtype=F32) / A_ROPE)
    ang = pos.astype(F32)[:, None] * inv[None, :]
    cos, sin = jnp.cos(ang), jnp.sin(ang)
    n = pos.shape[0]
    ones = jnp.ones((n, A_NOPE), F32)
    zeros = jnp.zeros((n, A_NOPE), F32)
    z16 = jnp.zeros((n, HALF_ROPE), F32)
    pad1 = jnp.ones((n, SLAB - A_QK), F32)
    pad0 = jnp.zeros((n, SLAB - A_QK), F32)
    cos_t = jnp.concatenate([ones, cos, cos, pad1], axis=1)
    sin_lo = jnp.concatenate([zeros, -sin, z16, pad0], axis=1)
    sin_hi = jnp.concatenate([zeros, z16, sin, pad0], axis=1)
    return cos_t, sin_lo, sin_hi


def kernel(x_prompt, x_sample, cache_mla_latent, cache_mla_krope, cache_band_k, cache_band_v,
           c_prompt, c_sample, g_norm, w_ada, b_ada, w_in, g_q_lat, w_uq, g_kv_lat, w_uk, w_uv,
           g_qn_a, g_qr_a, g_kn_a, g_kr_a, g_q_b, g_k_b, rel_bias_b, w_oa, w_ob, w_out):
    bp, s, _ = x_prompt.shape
    bs, t, _ = x_sample.shape
    past = cache_mla_latent.shape[2]
    win_s = cache_band_k.shape[2]
    win_p = min(B_WINDOW, s)
    depth = g_norm.shape[0]
    assert depth == 1

    seg_q = _seg_matrix(np.where(np.arange(SLAB) < A_NOPE, 0, np.where(np.arange(SLAB) < A_QK, 1, 2)),
                        np.array([A_NOPE, A_ROPE, SLAB - A_QK], np.float64))
    seg_64 = _seg_matrix(np.arange(SLAB) // B_HDIM, np.array([B_HDIM, B_HDIM], np.float64))
    seg_k = _seg_matrix(np.where(np.arange(SLAB) < A_NOPE, 0, 1),
                        np.array([A_NOPE, SLAB - A_NOPE], np.float64))

    assert win_s == B_WINDOW and t <= BAND_TQ
    pos_s = past + np.arange(t)

    l = 0
    zpad = jnp.zeros((D_MODEL, SLAB - A_QK), F32)
    z64 = jnp.zeros((D_MODEL, A_NOPE), F32)
    win = w_in[l]
    c_kr0 = A_Q_RANK + A_KV_RANK
    win_p_ = jnp.concatenate([win[:, :c_kr0], z64, win[:, c_kr0:c_kr0 + A_ROPE], zpad,
                              win[:, c_kr0 + A_ROPE:]], axis=1).astype(BF16)
    wuq_p = jnp.pad(w_uq[l], ((0, 0), (0, 0), (0, SLAB - A_QK))).reshape(A_Q_RANK, -1).astype(BF16)
    wuk_p = jnp.pad(w_uk[l], ((0, 0), (0, 0), (0, SLAB - A_NOPE))).reshape(A_KV_RANK, -1).astype(BF16)
    wuv_b = w_uv[l].reshape(A_KV_RANK, -1).astype(BF16)
    wuv_t = wuv_b.T
    qscale = A_SCALE * LOG2E
    gq = jnp.tile(jnp.concatenate([g_qn_a[l], g_qr_a[l], jnp.zeros((SLAB - A_QK,), F32)]) * qscale,
                  A_HEADS)[None]
    gk = jnp.tile(jnp.concatenate([g_kn_a[l], jnp.zeros((SLAB - A_NOPE,), F32)]), A_HEADS)[None]
    gkr = jnp.concatenate([jnp.zeros((A_NOPE,), F32), g_kr_a[l], jnp.zeros((SLAB - A_QK,), F32)])[None]
    gqb = jnp.tile(g_q_b[l] * (B_SCALE * LOG2E), B_HEADS)[None]
    gkb = jnp.tile(g_k_b[l], B_HEADS)[None]
    consts = (g_norm[l][None], win_p_, g_q_lat[l][None], wuq_p, g_kv_lat[l][None], gq, gkr,
              gqb, gkb, seg_q, seg_64)
    woa, wob, wout = w_oa[l].astype(BF16), w_ob[l].astype(BF16), w_out[l].astype(BF16)

    c_rows = jnp.concatenate([c_prompt, jnp.zeros((8 - bp, D_MODEL), F32),
                              jnp.repeat(c_sample, t, axis=0)], axis=0)
    mod = _adaln(c_rows, w_ada[l], b_ada[l])
    shift, scale, gate = mod[:, :D_MODEL], mod[:, D_MODEL:2 * D_MODEL], mod[:, 2 * D_MODEL:]

    tm = 256
    xp2d = x_prompt.reshape(bp * s, D_MODEL)
    (qa, lat, kr, krs, sga, qb, kb, vb, kbf, vbf, sgb, sma, smb) = _front(
        xp2d, scale[:bp, None], shift[:bp, None], _rope_tables(jnp.arange(s)), consts,
        tm=tm, rows_per_batch=s, per_row=False, tail_rows=win_p)
    tq_mla = 512
    ka, vt = _expand(lat, krs, wuk_p, wuv_t, gk, seg_k, tm=tq_mla, rows_per_batch=s, v_transposed=True)
    attn_a = _mla_prompt(qa.reshape(bp, s, -1), ka.reshape(bp, s, -1), vt, tq=tq_mla)
    table_plain, table_band = _band_tables(rel_bias_b[l])
    attn_b = _band_prompt(qb.reshape(bp, s, -1), kb.reshape(bp, s, -1), vb.reshape(bp, s, -1),
                          table_band, tq=BAND_TQ)
    y_prompt = _back(xp2d, gate[:bp, None], attn_a.reshape(bp * s, -1), sga,
                     attn_b.reshape(bp * s, -1), sgb, sma, smb, woa, wob, wout,
                     tm=512, rows_per_batch=s, per_row=False).reshape(bp, s, D_MODEL)

    xs2d = x_sample.reshape(bs * t, D_MODEL)
    rows_s = bs * t
    pos_tab = _rope_tables(jnp.asarray(pos_s))
    pos_tab = tuple(jnp.tile(a, (bs, 1)) for a in pos_tab)
    (qa2, lat2, kr2, krs2, sga2, qb2, kb2, vb2, kbf2, vbf2, sgb2, sma2, smb2) = _front(
        xs2d, scale[8:], shift[8:], pos_tab, consts, tm=rows_s, rows_per_batch=t, per_row=True,
        tail_rows=t)
    lat_all = jnp.concatenate([cache_mla_latent[l], lat2.reshape(bs, t, -1)], axis=1)
    krs_cache = jnp.pad(cache_mla_krope[l], ((0, 0), (0, 0), (A_NOPE, SLAB - A_QK))).astype(BF16)
    krs_all = jnp.concatenate([krs_cache, krs2.reshape(bs, t, -1)], axis=1)
    n_all = past + t
    ka2, va2 = _expand(lat_all.reshape(bs * n_all, -1), krs_all.reshape(bs * n_all, -1),
                       wuk_p, wuv_b, gk, seg_k, tm=256, rows_per_batch=n_all, v_transposed=False)
    attn_a2 = _mla_sample(qa2.reshape(bs, t, -1), ka2.reshape(bs, n_all, -1), va2.reshape(bs, n_all, -1))
    kb_all = jnp.concatenate([cache_band_k[l].reshape(bs, win_s, -1).astype(BF16),
                              kb2.reshape(bs, t, -1)], axis=1)
    vb_all = jnp.concatenate([cache_band_v[l].reshape(bs, win_s, -1).astype(BF16),
                              vb2.reshape(bs, t, -1)], axis=1)
    attn_b2 = _band_sample(qb2.reshape(bs, t, -1), kb_all, vb_all, table_plain)
    y_sample = _back(xs2d, gate[8:], attn_a2.reshape(rows_s, -1), sga2, attn_b2.reshape(rows_s, -1),
                     sgb2, sma2, smb2, woa, wob, wout,
                     tm=rows_s, rows_per_batch=t, per_row=True).reshape(bs, t, D_MODEL)

    return (y_prompt, y_sample,
            lat.reshape(1, bp, s, A_KV_RANK), kr.reshape(1, bp, s, A_ROPE),
            kbf.reshape(1, bp, win_p, B_HEADS, B_HDIM), vbf.reshape(1, bp, win_p, B_HEADS, B_HDIM),
            lat2.reshape(1, bs, t, A_KV_RANK), kr2.reshape(1, bs, t, A_ROPE),
            kbf2.reshape(1, bs, t, B_HEADS, B_HDIM), vbf2.reshape(1, bs, t, B_HEADS, B_HDIM))
```

```python
import functools
import math

import jax
import jax.numpy as jnp
import numpy as np
from jax import lax
from jax.experimental import pallas as pl
from jax.experimental.pallas import tpu as pltpu

F32 = jnp.float32
BF16 = jnp.bfloat16

D_MODEL = 1024
CHUNK = 64
A_HEADS = 8
A_NOPE = 64
A_ROPE = 32
A_VDIM = 64
A_QK = A_NOPE + A_ROPE
A_Q_RANK = 384
A_KV_RANK = 256
A_WIDTH = A_HEADS * A_VDIM
A_SCALE = A_QK ** -0.5
B_HEADS = 8
B_HDIM = 64
B_WIDTH = B_HEADS * B_HDIM
B_LEFT_CHUNKS = 8
B_WINDOW = B_LEFT_CHUNKS * CHUNK
B_MAX_REL = 256
B_SCALE = B_HDIM ** -0.5
ROPE_BASE = 10000.0
NORM_EPS = 1e-6
NEG_INF = -1e30
LOG2E = math.log2(math.e)

LANES = 128
SLAB = LANES
HALF_ROPE = A_ROPE // 2
ROPE_LANE0 = A_NOPE
VMEM_LIMIT = 48 * 1024 * 1024

C_QLAT = 0
C_KVLAT = C_QLAT + A_Q_RANK
C_KR = C_KVLAT + A_KV_RANK
C_GA = C_KR + SLAB
C_QB = C_GA + A_WIDTH
C_KB = C_QB + B_WIDTH
C_VB = C_KB + B_WIDTH
C_GB = C_VB + B_WIDTH
C_MA = C_GB + B_WIDTH
C_MB = C_MA + D_MODEL
C_END = C_MB + D_MODEL

NT = (((1,), (1,)), ((), ()))


def _dot(a, b):
    return jnp.dot(a, b, preferred_element_type=F32)


def _dot_nt(a, b):
    return lax.dot_general(a, b, NT, preferred_element_type=F32)


def _rms_full(x, g):
    ms = jnp.mean(x * x, axis=-1, keepdims=True)
    return x * lax.rsqrt(ms + NORM_EPS) * g


def _seg_rms(x, seg, g):
    ms = _dot((x * x).astype(BF16), seg)
    return x * lax.rsqrt(ms + NORM_EPS) * g


def _rope_slab(x, cos_t, sin_lo, sin_hi):
    return (x * cos_t
            + pltpu.roll(x, SLAB - HALF_ROPE, axis=1) * sin_lo
            + pltpu.roll(x, HALF_ROPE, axis=1) * sin_hi)


def _adaln_kernel(c_ref, w_ref, b_ref, o_ref):
    c = c_ref[...]
    sc = c * jax.nn.sigmoid(c)
    o_ref[...] = jnp.dot(sc, w_ref[...], preferred_element_type=F32,
                         precision=lax.Precision.HIGHEST) + b_ref[...]


def _adaln(c_rows, w_ada, b_ada):
    n = c_rows.shape[0]
    tn = 1024
    return pl.pallas_call(
        _adaln_kernel,
        out_shape=jax.ShapeDtypeStruct((n, 3 * D_MODEL), F32),
        grid=(3 * D_MODEL // tn,),
        in_specs=[pl.BlockSpec((n, D_MODEL), lambda j: (0, 0)),
                  pl.BlockSpec((D_MODEL, tn), lambda j: (0, j)),
                  pl.BlockSpec((1, tn), lambda j: (0, j))],
        out_specs=pl.BlockSpec((n, tn), lambda j: (0, j)),
        compiler_params=pltpu.CompilerParams(dimension_semantics=("arbitrary",),
                                             vmem_limit_bytes=VMEM_LIMIT),
        name="adaln",
    )(c_rows, w_ada, b_ada.reshape(1, -1))


def _front_kernel(x_ref, scale_ref, shift_ref, cos_ref, slo_ref, shi_ref,
                  gnorm_ref, win_ref, gql_ref, wuq_ref, gkv_ref, gq_ref, gkr_ref,
                  gqb_ref, gkb_ref, segq_ref, seg64_ref,
                  qa_ref, lat_ref, kr_ref, krs_ref, sga_ref, qb_ref, kb_ref, vb_ref,
                  kbf_ref, vbf_ref, sgb_ref, sma_ref, smb_ref, *, per_row):
    x = x_ref[...]
    if per_row:
        scale, shift = scale_ref[...], shift_ref[...]
    else:
        scale, shift = scale_ref[0], shift_ref[0]
    h = _rms_full(x, gnorm_ref[...]) * (1.0 + scale) + shift
    hb = h.astype(BF16)
    cos_t, sin_lo, sin_hi = cos_ref[...], slo_ref[...], shi_ref[...]

    def proj(c0, c1):
        return _dot(hb, win_ref[:, c0:c1])

    r = _rms_full(proj(C_QLAT, C_KVLAT), gql_ref[...]).astype(BF16)
    segq = segq_ref[...]
    for p in range(A_HEADS // 2):
        c0 = 2 * SLAB * p
        qa = _dot(r, wuq_ref[:, c0:c0 + 2 * SLAB])
        qa = _seg_rms(qa, segq, gq_ref[:, c0:c0 + 2 * SLAB])
        for s in range(2):
            slab = _rope_slab(qa[:, s * SLAB:(s + 1) * SLAB], cos_t, sin_lo, sin_hi)
            qa_ref[:, c0 + s * SLAB:c0 + (s + 1) * SLAB] = slab.astype(BF16)

    lat_ref[...] = _rms_full(proj(C_KVLAT, C_KR), gkv_ref[...])
    krs = proj(C_KR, C_GA)
    ms = jnp.sum(krs * krs, axis=-1, keepdims=True) * (1.0 / A_ROPE)
    krs = _rope_slab(krs * lax.rsqrt(ms + NORM_EPS) * gkr_ref[...], cos_t, sin_lo, sin_hi)
    krs_ref[...] = krs.astype(BF16)
    kr_ref[...] = pltpu.roll(krs, SLAB - ROPE_LANE0, axis=1)[:, :A_ROPE]

    ga = proj(C_GA, C_QB)
    sga_ref[...] = (ga * jax.nn.sigmoid(ga)).astype(BF16)
    gb = proj(C_GB, C_MA)
    sgb_ref[...] = (gb * jax.nn.sigmoid(gb)).astype(BF16)
    sma_ref[...] = jax.nn.sigmoid(proj(C_MA, C_MB)).astype(BF16)
    smb_ref[...] = jax.nn.sigmoid(proj(C_MB, C_END)).astype(BF16)

    seg64 = seg64_ref[...]
    for p in range(B_WIDTH // (2 * SLAB)):
        c0 = 2 * SLAB * p
        qb = proj(C_QB + c0, C_QB + c0 + 2 * SLAB)
        qb_ref[:, c0:c0 + 2 * SLAB] = _seg_rms(qb, seg64, gqb_ref[:, c0:c0 + 2 * SLAB]).astype(BF16)
        kb = proj(C_KB + c0, C_KB + c0 + 2 * SLAB)
        kb = _seg_rms(kb, seg64, gkb_ref[:, c0:c0 + 2 * SLAB])
        kbf_ref[:, c0:c0 + 2 * SLAB] = kb
        kb_ref[:, c0:c0 + 2 * SLAB] = kb.astype(BF16)
    vb = proj(C_VB, C_GB)
    vbf_ref[...] = vb
    vb_ref[...] = vb.astype(BF16)


def _front(x2d, scale, shift, rope_tabs, consts, *, tm, rows_per_batch, per_row, tail_rows):
    rows = x2d.shape[0]
    nt = rows // tm
    row = lambda i: (i, 0)
    fixed = lambda i: (0, 0)
    tpb = rows_per_batch // tm
    if per_row:
        mod_spec = pl.BlockSpec((tm, D_MODEL), row)
        tab_spec = pl.BlockSpec((tm, SLAB), row)
    else:
        mod_spec = pl.BlockSpec((1, 1, D_MODEL), lambda i: (i // tpb, 0, 0))
        tab_spec = pl.BlockSpec((tm, SLAB), lambda i: (i % tpb, 0))

    def full(a):
        return pl.BlockSpec(a.shape, fixed)

    def out(width, dtype):
        return jax.ShapeDtypeStruct((rows, width), dtype), pl.BlockSpec((tm, width), row)

    if tail_rows == rows_per_batch:
        tail = out(B_WIDTH, F32)
    else:
        ntail = tail_rows // tm
        tail = (jax.ShapeDtypeStruct((rows // rows_per_batch * tail_rows, B_WIDTH), F32),
                pl.BlockSpec((tm, B_WIDTH),
                             lambda i: (i // tpb * ntail + jnp.maximum(i % tpb - (tpb - ntail), 0), 0)))

    outs = [out(A_HEADS * SLAB, BF16),
            out(A_KV_RANK, F32),
            out(A_ROPE, F32),
            out(SLAB, BF16),
            out(A_WIDTH, BF16),
            out(B_WIDTH, BF16),
            out(B_WIDTH, BF16),
            out(B_WIDTH, BF16),
            tail,
            tail,
            out(B_WIDTH, BF16),
            out(D_MODEL, BF16),
            out(D_MODEL, BF16)]
    return pl.pallas_call(
        functools.partial(_front_kernel, per_row=per_row),
        out_shape=[o[0] for o in outs],
        grid=(nt,),
        in_specs=[pl.BlockSpec((tm, D_MODEL), row), mod_spec, mod_spec,
                  tab_spec, tab_spec, tab_spec] + [full(a) for a in consts],
        out_specs=[o[1] for o in outs],
        compiler_params=pltpu.CompilerParams(dimension_semantics=("arbitrary",),
                                             vmem_limit_bytes=VMEM_LIMIT),
        name="front",
    )(x2d, scale, shift, *rope_tabs, *consts)


def _expand_kernel(lat_ref, krs_ref, wuk_ref, wuv_ref, gk_ref, segk_ref, ka_ref, va_ref,
                   *, v_transposed):
    latb = lat_ref[...].astype(BF16)
    krs = krs_ref[...].astype(F32)
    segk = segk_ref[...]
    for p in range(A_HEADS // 2):
        c0 = 2 * SLAB * p
        kn = _dot(latb, wuk_ref[:, c0:c0 + 2 * SLAB])
        kn = _seg_rms(kn, segk, gk_ref[:, c0:c0 + 2 * SLAB])
        for s in range(2):
            ka_ref[:, c0 + s * SLAB:c0 + (s + 1) * SLAB] = (
                kn[:, s * SLAB:(s + 1) * SLAB] + krs).astype(BF16)
    if v_transposed:
        vt = _dot_nt(wuv_ref[...], latb).astype(BF16)
        for hp in range(A_HEADS // 2):
            va_ref[0, hp, 0] = vt[hp * SLAB:(hp + 1) * SLAB, :]
    else:
        va_ref[...] = _dot(latb, wuv_ref[...]).astype(BF16)


def _expand(lat2d, krs2d, wuk_p, wuv, gk, segk, *, tm, rows_per_batch, v_transposed):
    rows = lat2d.shape[0]
    row = lambda i: (i, 0)
    fixed = lambda i: (0, 0)
    if v_transposed:
        tpb = rows_per_batch // tm
        v_shape = jax.ShapeDtypeStruct((rows // rows_per_batch, A_HEADS // 2, tpb, SLAB, tm), BF16)
        v_spec = pl.BlockSpec((1, A_HEADS // 2, 1, SLAB, tm), lambda i: (i // tpb, 0, i % tpb, 0, 0))
    else:
        v_shape = jax.ShapeDtypeStruct((rows, A_WIDTH), BF16)
        v_spec = pl.BlockSpec((tm, A_WIDTH), row)
    return pl.pallas_call(
        functools.partial(_expand_kernel, v_transposed=v_transposed),
        out_shape=[jax.ShapeDtypeStruct((rows, A_HEADS * SLAB), BF16), v_shape],
        grid=(rows // tm,),
        in_specs=[pl.BlockSpec((tm, A_KV_RANK), row), pl.BlockSpec((tm, SLAB), row),
                  pl.BlockSpec(wuk_p.shape, fixed), pl.BlockSpec(wuv.shape, fixed),
                  pl.BlockSpec(gk.shape, fixed), pl.BlockSpec(segk.shape, fixed)],
        out_specs=[pl.BlockSpec((tm, A_HEADS * SLAB), row), v_spec],
        compiler_params=pltpu.CompilerParams(dimension_semantics=("arbitrary",),
                                             vmem_limit_bytes=VMEM_LIMIT),
        name="expand",
    )(lat2d, krs2d, wuk_p, wuv, gk, segk)


def _select_heads(o_even, o_odd):
    lane = lax.broadcasted_iota(jnp.int32, o_even.shape, 1)
    return jnp.where(lane < A_VDIM, o_even, o_odd)


def _mla_prompt_kernel(q_ref, k_ref, vt_ref, o_ref, s_scr, p_scr, *, tq, nsub):
    qi = pl.program_id(2)
    assert nsub == 2
    n_full = nsub * qi
    chunk_delta = (lax.broadcasted_iota(jnp.int32, (tq, tq), 0) // CHUNK
                   - lax.broadcasted_iota(jnp.int32, (tq, tq), 1) // CHUNK)
    ones = jnp.ones((16, tq), BF16)
    chains = [(e, h) for h in range(nsub) for e in range(2)]

    def qk(j, c):
        e, h = chains[c]
        k0 = pl.multiple_of(j * tq, tq)
        return _dot_nt(k_ref[0, pl.ds(k0, tq), e * SLAB:(e + 1) * SLAB],
                       q_ref[0, h * tq:(h + 1) * tq, e * SLAB:(e + 1) * SLAB])

    def softmax(s, m, max_delta=None):
        if max_delta is not None:
            s = jnp.where(chunk_delta <= max_delta, s, NEG_INF)
        m_new = jnp.maximum(m, jnp.max(s, axis=0, keepdims=True))
        return m_new, jnp.exp2(m - m_new), jnp.exp2(s - m_new).astype(BF16)

    def pv(j, c, p):
        e, _ = chains[c]
        vt = vt_ref[0, 0, j, e * A_VDIM:(e + 1) * A_VDIM, :]
        return _dot(jnp.concatenate([vt, ones], axis=0), p)

    def stages(i, slot, state):
        new_state = []
        for c in range(len(chains)):
            m, alpha, acc, bmax = state[c]
            acc = alpha * acc + pv(jnp.maximum(i - 1, 0), c, p_scr[1 - slot, c])
            m_new = jnp.maximum(m, bmax)
            alpha = jnp.exp2(m - m_new)
            p_scr[slot, c] = jnp.exp2(s_scr[slot, c] - m_new).astype(BF16)
            s_next = qk(i + 1, c)
            s_scr[1 - slot, c] = s_next
            new_state.append((m_new, alpha, acc, jnp.max(s_next, axis=0, keepdims=True)))
        return tuple(new_state)

    init = []
    for c in range(len(chains)):
        s0 = qk(0, c)
        s_scr[0, c] = s0
        init.append((jnp.full((1, tq), NEG_INF, F32), jnp.ones((1, tq), F32),
                     jnp.zeros((A_VDIM + 16, tq), F32), jnp.max(s0, axis=0, keepdims=True)))
    p_scr[1] = jnp.zeros(p_scr.shape[1:], BF16)
    state = lax.fori_loop(0, qi, lambda t, st: stages(2 * t + 1, 1, stages(2 * t, 0, st)),
                          tuple(init))

    final = []
    for c, (e, h) in enumerate(chains):
        m, alpha, acc, _ = state[c]
        acc = alpha * acc + pv(jnp.maximum(n_full - 1, 0), c, p_scr[1, c])
        s = s_scr[0, c]
        for d in range(h + 1):
            if d > 0:
                s = qk(n_full + d, c)
            m, alpha, p = softmax(s, m, 0 if d == h else None)
            acc = alpha * acc + pv(n_full + d, c, p)
        final.append(acc[:A_VDIM] / acc[A_VDIM:A_VDIM + 1])
    for h in range(nsub):
        o_ref[0, h * tq:(h + 1) * tq, :] = jnp.concatenate(
            final[2 * h:2 * h + 2], axis=0).T.astype(o_ref.dtype)


def _mla_prompt(qa, ka, vt, *, tq, nsub):
    b, s, _ = qa.shape
    assert vt.shape == (b, A_HEADS // 2, s // tq, SLAB, tq)
    tqq = nsub * tq
    return pl.pallas_call(
        functools.partial(_mla_prompt_kernel, tq=tq, nsub=nsub),
        scratch_shapes=[pltpu.VMEM((2, 2 * nsub, tq, tq), F32),
                        pltpu.VMEM((2, 2 * nsub, tq, tq), BF16)],
        out_shape=jax.ShapeDtypeStruct((b, s, A_WIDTH), BF16),
        grid=(b, A_HEADS // 2, s // tqq),
        in_specs=[pl.BlockSpec((1, tqq, 2 * SLAB), lambda bi, hp, qi: (bi, qi, hp)),
                  pl.BlockSpec((1, s, 2 * SLAB), lambda bi, hp, qi: (bi, 0, hp)),
                  pl.BlockSpec((1, 1, s // tq, SLAB, tq), lambda bi, hp, qi: (bi, hp, 0, 0, 0))],
        out_specs=pl.BlockSpec((1, tqq, SLAB), lambda bi, hp, qi: (bi, qi, hp)),
        compiler_params=pltpu.CompilerParams(
            dimension_semantics=("arbitrary", "arbitrary", "arbitrary"),
            vmem_limit_bytes=VMEM_LIMIT),
        name="mla_prompt",
    )(qa, ka, vt)


def _mla_sample_kernel(q_ref, k_ref, v_ref, o_ref):
    v = v_ref[0]
    outs = []
    for e in range(2):
        s = _dot_nt(q_ref[0, :, e * SLAB:(e + 1) * SLAB], k_ref[0, :, e * SLAB:(e + 1) * SLAB])
        p = jnp.exp2(s - jnp.max(s, axis=-1, keepdims=True))
        l = jnp.sum(p, axis=-1, keepdims=True)
        outs.append(_dot(p.astype(BF16), v) / l)
    o_ref[0] = _select_heads(outs[0], outs[1]).astype(o_ref.dtype)


def _mla_sample(qa, ka, va):
    b, t, _ = qa.shape
    n = ka.shape[1]
    return pl.pallas_call(
        _mla_sample_kernel,
        out_shape=jax.ShapeDtypeStruct((b, t, A_WIDTH), BF16),
        grid=(b, A_HEADS // 2),
        in_specs=[pl.BlockSpec((1, t, 2 * SLAB), lambda bi, hp: (bi, 0, hp)),
                  pl.BlockSpec((1, n, 2 * SLAB), lambda bi, hp: (bi, 0, hp)),
                  pl.BlockSpec((1, n, SLAB), lambda bi, hp: (bi, 0, hp))],
        out_specs=pl.BlockSpec((1, t, SLAB), lambda bi, hp: (bi, 0, hp)),
        compiler_params=pltpu.CompilerParams(dimension_semantics=("arbitrary", "arbitrary"),
                                             vmem_limit_bytes=VMEM_LIMIT),
        name="mla_sample",
    )(qa, ka, va)


BAND_TQ = 256
BAND_KEYS = B_WINDOW + BAND_TQ
TOEPLITZ_W = 1024


def _band_table_kernel(rb_ref, onehot_ref, allowed_ref, tb_ref, tp_ref):
    g = jnp.dot(rb_ref[...], onehot_ref[...], preferred_element_type=F32,
                precision=lax.Precision.HIGHEST) * LOG2E
    allowed = allowed_ref[...] > 0.0
    for h in range(B_HEADS):
        row = jnp.broadcast_to(g[h:h + 1, :], (BAND_TQ, TOEPLITZ_W))
        t = pltpu.roll(row, 0, axis=1, stride=1, stride_axis=0)[:, :BAND_KEYS]
        tb_ref[h] = t
        tp_ref[h] = jnp.where(allowed, t, NEG_INF)


def _band_tables(rel_bias):
    n_rel = rel_bias.shape[1]
    n_pad = -(-n_rel // LANES) * LANES
    x = np.arange(TOEPLITZ_W)
    key_minus_query = np.where(x < BAND_KEYS, x, x - TOEPLITZ_W)
    dist = B_WINDOW - key_minus_query
    idx = np.clip(dist, -B_MAX_REL, B_MAX_REL) + B_MAX_REL
    onehot = np.zeros((n_pad, TOEPLITZ_W), np.float32)
    onehot[idx, x] = 1.0
    q_chunk = np.arange(BAND_TQ) // CHUNK + B_LEFT_CHUNKS
    k_chunk = np.arange(BAND_KEYS) // CHUNK
    allowed = (k_chunk[None, :] <= q_chunk[:, None]) & (k_chunk[None, :] >= q_chunk[:, None] - B_LEFT_CHUNKS)
    rb = jnp.pad(rel_bias, ((0, 0), (0, n_pad - n_rel)))
    shape = jax.ShapeDtypeStruct((B_HEADS, BAND_TQ, BAND_KEYS), F32)
    return pl.pallas_call(
        _band_table_kernel,
        out_shape=[shape, shape],
        compiler_params=pltpu.CompilerParams(vmem_limit_bytes=VMEM_LIMIT),
        name="band_table",
    )(rb, jnp.asarray(onehot), jnp.asarray(allowed.astype(np.float32)))


def _head_lane_masks(width):
    lane = lax.broadcasted_iota(jnp.int32, (1, width), 1)
    return [(lane < B_HDIM), (lane >= B_HDIM)]


def _band_prompt_kernel(q_ref, k0_ref, k1_ref, k2_ref, v0_ref, v1_ref, v2_ref, t_ref, o_ref,
                        *, tq, nkb):
    qi = pl.program_id(2)
    q = q_ref[0]
    ks = [k0_ref[0], k1_ref[0], k2_ref[0]]
    vs = [v0_ref[0], v1_ref[0], v2_ref[0]]
    masks = _head_lane_masks(SLAB)
    outs = []
    for e in range(2):
        qm = jnp.where(masks[e], q, jnp.zeros_like(q))
        ss = []
        for j in range(nkb):
            s = _dot_nt(qm, ks[j]) + t_ref[e, :, j * tq:(j + 1) * tq]
            if j < nkb - 1:
                s = jnp.where(qi + (j - (nkb - 1)) >= 0, s, NEG_INF)
            ss.append(s)
        m = functools.reduce(jnp.maximum, [jnp.max(s, axis=-1, keepdims=True) for s in ss])
        ps = [jnp.exp2(s - m) for s in ss]
        l = sum(jnp.sum(p, axis=-1, keepdims=True) for p in ps)
        o = sum(_dot(p.astype(BF16), v) for p, v in zip(ps, vs))
        outs.append(o / l)
    o_ref[0] = _select_heads(outs[0], outs[1]).astype(o_ref.dtype)


def _band_prompt(qb, kb, vb, table, *, tq):
    b, s, _ = qb.shape
    nkb = B_WINDOW // tq + 1
    assert nkb == 3

    def kv_spec(j):
        return pl.BlockSpec((1, tq, SLAB),
                            lambda hp, bi, qi: (bi, jnp.maximum(qi + (j - (nkb - 1)), 0), hp))

    return pl.pallas_call(
        functools.partial(_band_prompt_kernel, tq=tq, nkb=nkb),
        out_shape=jax.ShapeDtypeStruct((b, s, B_WIDTH), BF16),
        grid=(B_HEADS // 2, b, s // tq),
        in_specs=[pl.BlockSpec((1, tq, SLAB), lambda hp, bi, qi: (bi, qi, hp))]
                 + [kv_spec(j) for j in range(nkb)] + [kv_spec(j) for j in range(nkb)]
                 + [pl.BlockSpec((2, tq, nkb * tq), lambda hp, bi, qi: (hp, 0, 0))],
        out_specs=pl.BlockSpec((1, tq, SLAB), lambda hp, bi, qi: (bi, qi, hp)),
        compiler_params=pltpu.CompilerParams(
            dimension_semantics=("arbitrary", "arbitrary", "arbitrary"),
            vmem_limit_bytes=VMEM_LIMIT),
        name="band_prompt",
    )(qb, kb, kb, kb, vb, vb, vb, table)


def _band_sample_kernel(q_ref, k_ref, v_ref, t_ref, o_ref):
    q = q_ref[0]
    k = k_ref[0]
    v = v_ref[0]
    masks = _head_lane_masks(SLAB)
    outs = []
    for e in range(2):
        qm = jnp.where(masks[e], q, jnp.zeros_like(q))
        s = _dot_nt(qm, k) + t_ref[e, :, :k.shape[0]]
        p = jnp.exp2(s - jnp.max(s, axis=-1, keepdims=True))
        l = jnp.sum(p, axis=-1, keepdims=True)
        outs.append(_dot(p.astype(BF16), v) / l)
    o_ref[0] = _select_heads(outs[0], outs[1]).astype(o_ref.dtype)


def _band_sample(qb, kb, vb, table):
    b, t, _ = qb.shape
    n = kb.shape[1]
    return pl.pallas_call(
        _band_sample_kernel,
        out_shape=jax.ShapeDtypeStruct((b, t, B_WIDTH), BF16),
        grid=(B_HEADS // 2, b),
        in_specs=[pl.BlockSpec((1, t, SLAB), lambda hp, bi: (bi, 0, hp)),
                  pl.BlockSpec((1, n, SLAB), lambda hp, bi: (bi, 0, hp)),
                  pl.BlockSpec((1, n, SLAB), lambda hp, bi: (bi, 0, hp)),
                  pl.BlockSpec((2, t, BAND_KEYS), lambda hp, bi: (hp, 0, 0))],
        out_specs=pl.BlockSpec((1, t, SLAB), lambda hp, bi: (bi, 0, hp)),
        compiler_params=pltpu.CompilerParams(dimension_semantics=("arbitrary", "arbitrary"),
                                             vmem_limit_bytes=VMEM_LIMIT),
        name="band_sample",
    )(qb, kb, vb, table)


def _back_kernel(x_ref, gate_ref, aa_ref, sga_ref, ab_ref, sgb_ref, sma_ref, smb_ref,
                 woa_ref, wob_ref, wout_ref, o_ref, *, per_row):
    gate = gate_ref[...] if per_row else gate_ref[0]
    ua = _dot(aa_ref[...] * sga_ref[...], woa_ref[...])
    ub = _dot(ab_ref[...] * sgb_ref[...], wob_ref[...])
    merged = sma_ref[...].astype(F32) * ua + smb_ref[...].astype(F32) * ub
    o_ref[...] = x_ref[...] + gate * _dot(merged.astype(BF16), wout_ref[...])


def _back(x2d, gate, aa, sga, ab, sgb, sma, smb, woa, wob, wout, *, tm, rows_per_batch, per_row):
    rows = x2d.shape[0]
    row = lambda i: (i, 0)
    fixed = lambda i: (0, 0)
    if per_row:
        gate_spec = pl.BlockSpec((tm, D_MODEL), row)
    else:
        tpb = rows_per_batch // tm
        gate_spec = pl.BlockSpec((1, 1, D_MODEL), lambda i: (i // tpb, 0, 0))
    return pl.pallas_call(
        functools.partial(_back_kernel, per_row=per_row),
        out_shape=jax.ShapeDtypeStruct((rows, D_MODEL), F32),
        grid=(rows // tm,),
        in_specs=[pl.BlockSpec((tm, D_MODEL), row), gate_spec,
                  pl.BlockSpec((tm, A_WIDTH), row), pl.BlockSpec((tm, A_WIDTH), row),
                  pl.BlockSpec((tm, B_WIDTH), row), pl.BlockSpec((tm, B_WIDTH), row),
                  pl.BlockSpec((tm, D_MODEL), row), pl.BlockSpec((tm, D_MODEL), row),
                  pl.BlockSpec(woa.shape, fixed), pl.BlockSpec(wob.shape, fixed),
                  pl.BlockSpec(wout.shape, fixed)],
        out_specs=pl.BlockSpec((tm, D_MODEL), row),
        compiler_params=pltpu.CompilerParams(dimension_semantics=("arbitrary",),
                                             vmem_limit_bytes=VMEM_LIMIT),
        name="back",
    )(x2d, gate, aa, sga, ab, sgb, sma, smb, woa, wob, wout)


def _seg_matrix(group_of_lane, sizes):
    lane = np.arange(2 * SLAB)
    slab = lane // SLAB
    grp = group_of_lane[lane % SLAB]
    same = (slab[:, None] == slab[None, :]) & (grp[:, None] == grp[None, :])
    return jnp.asarray(np.where(same, 1.0 / sizes[grp][None, :], 0.0), dtype=BF16)


def _rope_tables(pos):
    inv = ROPE_BASE ** (-jnp.arange(0, A_ROPE, 2, dtype=F32) / A_ROPE)
    ang = pos.astype(F32)[:, None] * inv[None, :]
    cos, sin = jnp.cos(ang), jnp.sin(ang)
    n = pos.shape[0]
    ones = jnp.ones((n, A_NOPE), F32)
    zeros = jnp.zeros((n, A_NOPE), F32)
    z16 = jnp.zeros((n, HALF_ROPE), F32)
    pad1 = jnp.ones((n, SLAB - A_QK), F32)
    pad0 = jnp.zeros((n, SLAB - A_QK), F32)
    cos_t = jnp.concatenate([ones, cos, cos, pad1], axis=1)
    sin_lo = jnp.concatenate([zeros, -sin, z16, pad0], axis=1)
    sin_hi = jnp.concatenate([zeros, z16, sin, pad0], axis=1)
    return cos_t, sin_lo, sin_hi


def kernel(x_prompt, x_sample, cache_mla_latent, cache_mla_krope, cache_band_k, cache_band_v,
           c_prompt, c_sample, g_norm, w_ada, b_ada, w_in, g_q_lat, w_uq, g_kv_lat, w_uk, w_uv,
           g_qn_a, g_qr_a, g_kn_a, g_kr_a, g_q_b, g_k_b, rel_bias_b, w_oa, w_ob, w_out):
    bp, s, _ = x_prompt.shape
    bs, t, _ = x_sample.shape
    past = cache_mla_latent.shape[2]
    win_s = cache_band_k.shape[2]
    win_p = min(B_WINDOW, s)
    depth = g_norm.shape[0]
    assert depth == 1

    seg_q = _seg_matrix(np.where(np.arange(SLAB) < A_NOPE, 0, np.where(np.arange(SLAB) < A_QK, 1, 2)),
                        np.array([A_NOPE, A_ROPE, SLAB - A_QK], np.float64))
    seg_64 = _seg_matrix(np.arange(SLAB) // B_HDIM, np.array([B_HDIM, B_HDIM], np.float64))
    seg_k = _seg_matrix(np.where(np.arange(SLAB) < A_NOPE, 0, 1),
                        np.array([A_NOPE, SLAB - A_NOPE], np.float64))

    assert win_s == B_WINDOW and t <= BAND_TQ
    pos_s = past + np.arange(t)

    l = 0
    zpad = jnp.zeros((D_MODEL, SLAB - A_QK), F32)
    z64 = jnp.zeros((D_MODEL, A_NOPE), F32)
    win = w_in[l]
    c_kr0 = A_Q_RANK + A_KV_RANK
    win_p_ = jnp.concatenate([win[:, :c_kr0], z64, win[:, c_kr0:c_kr0 + A_ROPE], zpad,
                              win[:, c_kr0 + A_ROPE:]], axis=1).astype(BF16)
    wuq_p = jnp.pad(w_uq[l], ((0, 0), (0, 0), (0, SLAB - A_QK))).reshape(A_Q_RANK, -1).astype(BF16)
    wuk_p = jnp.pad(w_uk[l], ((0, 0), (0, 0), (0, SLAB - A_NOPE))).reshape(A_KV_RANK, -1).astype(BF16)
    wuv_b = w_uv[l].reshape(A_KV_RANK, -1).astype(BF16)
    wuv_t = wuv_b.T
    qscale = A_SCALE * LOG2E
    gq = jnp.tile(jnp.concatenate([g_qn_a[l], g_qr_a[l], jnp.zeros((SLAB - A_QK,), F32)]) * qscale,
                  A_HEADS)[None]
    gk = jnp.tile(jnp.concatenate([g_kn_a[l], jnp.zeros((SLAB - A_NOPE,), F32)]), A_HEADS)[None]
    gkr = jnp.concatenate([jnp.zeros((A_NOPE,), F32), g_kr_a[l], jnp.zeros((SLAB - A_QK,), F32)])[None]
    gqb = jnp.tile(g_q_b[l] * (B_SCALE * LOG2E), B_HEADS)[None]
    gkb = jnp.tile(g_k_b[l], B_HEADS)[None]
    consts = (g_norm[l][None], win_p_, g_q_lat[l][None], wuq_p, g_kv_lat[l][None], gq, gkr,
              gqb, gkb, seg_q, seg_64)
    woa, wob, wout = w_oa[l].astype(BF16), w_ob[l].astype(BF16), w_out[l].astype(BF16)

    c_rows = jnp.concatenate([c_prompt, jnp.zeros((8 - bp, D_MODEL), F32),
                              jnp.repeat(c_sample, t, axis=0)], axis=0)
    mod = _adaln(c_rows, w_ada[l], b_ada[l])
    shift, scale, gate = mod[:, :D_MODEL], mod[:, D_MODEL:2 * D_MODEL], mod[:, 2 * D_MODEL:]

    tm = 512
    xp2d = x_prompt.reshape(bp * s, D_MODEL)
    (qa, lat, kr, krs, sga, qb, kb, vb, kbf, vbf, sgb, sma, smb) = _front(
        xp2d, scale[:bp, None], shift[:bp, None], _rope_tables(jnp.arange(s)), consts,
        tm=tm, rows_per_batch=s, per_row=False, tail_rows=win_p)
    tq_mla = 512
    ka, vt = _expand(lat, krs, wuk_p, wuv_t, gk, seg_k, tm=tq_mla, rows_per_batch=s, v_transposed=True)
    attn_a = _mla_prompt(qa.reshape(bp, s, -1), ka.reshape(bp, s, -1), vt, tq=tq_mla, nsub=2)
    table_plain, table_band = _band_tables(rel_bias_b[l])
    attn_b = _band_prompt(qb.reshape(bp, s, -1), kb.reshape(bp, s, -1), vb.reshape(bp, s, -1),
                          table_band, tq=BAND_TQ)
    y_prompt = _back(xp2d, gate[:bp, None], attn_a.reshape(bp * s, -1), sga,
                     attn_b.reshape(bp * s, -1), sgb, sma, smb, woa, wob, wout,
                     tm=512, rows_per_batch=s, per_row=False).reshape(bp, s, D_MODEL)

    xs2d = x_sample.reshape(bs * t, D_MODEL)
    rows_s = bs * t
    pos_tab = _rope_tables(jnp.asarray(pos_s))
    pos_tab = tuple(jnp.tile(a, (bs, 1)) for a in pos_tab)
    (qa2, lat2, kr2, krs2, sga2, qb2, kb2, vb2, kbf2, vbf2, sgb2, sma2, smb2) = _front(
        xs2d, scale[8:], shift[8:], pos_tab, consts, tm=rows_s, rows_per_batch=t, per_row=True,
        tail_rows=t)
    lat_all = jnp.concatenate([cache_mla_latent[l], lat2.reshape(bs, t, -1)], axis=1)
    krs_cache = jnp.pad(cache_mla_krope[l], ((0, 0), (0, 0), (A_NOPE, SLAB - A_QK))).astype(BF16)
    krs_all = jnp.concatenate([krs_cache, krs2.reshape(bs, t, -1)], axis=1)
    n_all = past + t
    ka2, va2 = _expand(lat_all.reshape(bs * n_all, -1), krs_all.reshape(bs * n_all, -1),
                       wuk_p, wuv_b, gk, seg_k, tm=256, rows_per_batch=n_all, v_transposed=False)
    attn_a2 = _mla_sample(qa2.reshape(bs, t, -1), ka2.reshape(bs, n_all, -1), va2.reshape(bs, n_all, -1))
    kb_all = jnp.concatenate([cache_band_k[l].reshape(bs, win_s, -1).astype(BF16),
                              kb2.reshape(bs, t, -1)], axis=1)
    vb_all = jnp.concatenate([cache_band_v[l].reshape(bs, win_s, -1).astype(BF16),
                              vb2.reshape(bs, t, -1)], axis=1)
    attn_b2 = _band_sample(qb2.reshape(bs, t, -1), kb_all, vb_all, table_plain)
    y_sample = _back(xs2d, gate[8:], attn_a2.reshape(rows_s, -1), sga2, attn_b2.reshape(rows_s, -1),
                     sgb2, sma2, smb2, woa, wob, wout,
                     tm=rows_s, rows_per_batch=t, per_row=True).reshape(bs, t, D_MODEL)

    return (y_prompt, y_sample,
            lat.reshape(1, bp, s, A_KV_RANK), kr.reshape(1, bp, s, A_ROPE),
            kbf.reshape(1, bp, win_p, B_HEADS, B_HDIM), vbf.reshape(1, bp, win_p, B_HEADS, B_HDIM),
            lat2.reshape(1, bs, t, A_KV_RANK), kr2.reshape(1, bs, t, A_ROPE),
            kbf2.reshape(1, bs, t, B_HEADS, B_HDIM), vbf2.reshape(1, bs, t, B_HEADS, B_HDIM))
```

```python
import functools
import math

import jax
import jax.numpy as jnp
import numpy as np
from jax import lax
from jax.experimental import pallas as pl
from jax.experimental.pallas import tpu as pltpu

F32 = jnp.float32
BF16 = jnp.bfloat16

D_MODEL = 1024
CHUNK = 64
A_HEADS = 8
A_NOPE = 64
A_ROPE = 32
A_VDIM = 64
A_QK = A_NOPE + A_ROPE
A_Q_RANK = 384
A_KV_RANK = 256
A_WIDTH = A_HEADS * A_VDIM
A_SCALE = A_QK ** -0.5
B_HEADS = 8
B_HDIM = 64
B_WIDTH = B_HEADS * B_HDIM
B_LEFT_CHUNKS = 8
B_WINDOW = B_LEFT_CHUNKS * CHUNK
B_MAX_REL = 256
B_SCALE = B_HDIM ** -0.5
ROPE_BASE = 10000.0
NORM_EPS = 1e-6
NEG_INF = -1e30
LOG2E = math.log2(math.e)
MAX_EXP2_DRIFT = 60.0

LANES = 128
SLAB = LANES
HALF_ROPE = A_ROPE // 2
ROPE_LANE0 = A_NOPE
VMEM_LIMIT = 48 * 1024 * 1024

C_QLAT = 0
C_KVLAT = C_QLAT + A_Q_RANK
C_KR = C_KVLAT + A_KV_RANK
C_GA = C_KR + SLAB
C_QB = C_GA + A_WIDTH
C_KB = C_QB + B_WIDTH
C_VB = C_KB + B_WIDTH
C_GB = C_VB + B_WIDTH
C_MA = C_GB + B_WIDTH
C_MB = C_MA + D_MODEL
C_END = C_MB + D_MODEL

NT = (((1,), (1,)), ((), ()))


def _dot(a, b):
    return jnp.dot(a, b, preferred_element_type=F32)


def _dot_nt(a, b):
    return lax.dot_general(a, b, NT, preferred_element_type=F32)


def _rms_full(x, g):
    ms = jnp.mean(x * x, axis=-1, keepdims=True)
    return x * lax.rsqrt(ms + NORM_EPS) * g


def _seg_rms(x, seg, g):
    ms = _dot((x * x).astype(BF16), seg)
    return x * lax.rsqrt(ms + NORM_EPS) * g


def _rope_slab(x, cos_t, sin_lo, sin_hi):
    return (x * cos_t
            + pltpu.roll(x, SLAB - HALF_ROPE, axis=1) * sin_lo
            + pltpu.roll(x, HALF_ROPE, axis=1) * sin_hi)


def _adaln_kernel(c_ref, w_ref, b_ref, o_ref):
    c = c_ref[...]
    sc = c * jax.nn.sigmoid(c)
    o_ref[...] = jnp.dot(sc, w_ref[...], preferred_element_type=F32,
                         precision=lax.Precision.HIGHEST) + b_ref[...]


def _adaln(c_rows, w_ada, b_ada):
    n = c_rows.shape[0]
    tn = 1024
    return pl.pallas_call(
        _adaln_kernel,
        out_shape=jax.ShapeDtypeStruct((n, 3 * D_MODEL), F32),
        grid=(3 * D_MODEL // tn,),
        in_specs=[pl.BlockSpec((n, D_MODEL), lambda j: (0, 0)),
                  pl.BlockSpec((D_MODEL, tn), lambda j: (0, j)),
                  pl.BlockSpec((1, tn), lambda j: (0, j))],
        out_specs=pl.BlockSpec((n, tn), lambda j: (0, j)),
        compiler_params=pltpu.CompilerParams(dimension_semantics=("arbitrary",),
                                             vmem_limit_bytes=VMEM_LIMIT),
        name="adaln",
    )(c_rows, w_ada, b_ada.reshape(1, -1))


def _front_kernel(x_ref, scale_ref, shift_ref, cos_ref, slo_ref, shi_ref,
                  gnorm_ref, win_ref, gql_ref, wuq_ref, gkv_ref, gq_ref, gkr_ref,
                  gqb_ref, gkb_ref, segq_ref, seg64_ref,
                  qa_ref, lat_ref, kr_ref, krs_ref, sga_ref, qb_ref, kb_ref, vb_ref,
                  kbf_ref, vbf_ref, sgb_ref, sma_ref, smb_ref, *, per_row):
    x = x_ref[...]
    if per_row:
        scale, shift = scale_ref[...], shift_ref[...]
    else:
        scale, shift = scale_ref[0], shift_ref[0]
    h = _rms_full(x, gnorm_ref[...]) * (1.0 + scale) + shift
    hb = h.astype(BF16)
    cos_t, sin_lo, sin_hi = cos_ref[...], slo_ref[...], shi_ref[...]

    def proj(c0, c1):
        return _dot(hb, win_ref[:, c0:c1])

    r = _rms_full(proj(C_QLAT, C_KVLAT), gql_ref[...]).astype(BF16)
    segq = segq_ref[...]
    for p in range(A_HEADS // 2):
        c0 = 2 * SLAB * p
        qa = _dot(r, wuq_ref[:, c0:c0 + 2 * SLAB])
        qa = _seg_rms(qa, segq, gq_ref[:, c0:c0 + 2 * SLAB])
        for s in range(2):
            slab = _rope_slab(qa[:, s * SLAB:(s + 1) * SLAB], cos_t, sin_lo, sin_hi)
            qa_ref[:, c0 + s * SLAB:c0 + (s + 1) * SLAB] = slab.astype(BF16)

    lat_ref[...] = _rms_full(proj(C_KVLAT, C_KR), gkv_ref[...])
    krs = proj(C_KR, C_GA)
    ms = jnp.sum(krs * krs, axis=-1, keepdims=True) * (1.0 / A_ROPE)
    krs = _rope_slab(krs * lax.rsqrt(ms + NORM_EPS) * gkr_ref[...], cos_t, sin_lo, sin_hi)
    krs_ref[...] = krs.astype(BF16)
    kr_ref[...] = pltpu.roll(krs, SLAB - ROPE_LANE0, axis=1)[:, :A_ROPE]

    ga = proj(C_GA, C_QB)
    sga_ref[...] = (ga * jax.nn.sigmoid(ga)).astype(BF16)
    gb = proj(C_GB, C_MA)
    sgb_ref[...] = (gb * jax.nn.sigmoid(gb)).astype(BF16)
    sma_ref[...] = jax.nn.sigmoid(proj(C_MA, C_MB)).astype(BF16)
    smb_ref[...] = jax.nn.sigmoid(proj(C_MB, C_END)).astype(BF16)

    seg64 = seg64_ref[...]
    for p in range(B_WIDTH // (2 * SLAB)):
        c0 = 2 * SLAB * p
        qb = proj(C_QB + c0, C_QB + c0 + 2 * SLAB)
        qb_ref[:, c0:c0 + 2 * SLAB] = _seg_rms(qb, seg64, gqb_ref[:, c0:c0 + 2 * SLAB]).astype(BF16)
        kb = proj(C_KB + c0, C_KB + c0 + 2 * SLAB)
        kb = _seg_rms(kb, seg64, gkb_ref[:, c0:c0 + 2 * SLAB])
        kbf_ref[:, c0:c0 + 2 * SLAB] = kb
        kb_ref[:, c0:c0 + 2 * SLAB] = kb.astype(BF16)
    vb = proj(C_VB, C_GB)
    vbf_ref[...] = vb
    vb_ref[...] = vb.astype(BF16)


def _front(x2d, scale, shift, rope_tabs, consts, *, tm, rows_per_batch, per_row, tail_rows):
    rows = x2d.shape[0]
    nt = rows // tm
    row = lambda i: (i, 0)
    fixed = lambda i: (0, 0)
    tpb = rows_per_batch // tm
    if per_row:
        mod_spec = pl.BlockSpec((tm, D_MODEL), row)
        tab_spec = pl.BlockSpec((tm, SLAB), row)
    else:
        mod_spec = pl.BlockSpec((1, 1, D_MODEL), lambda i: (i // tpb, 0, 0))
        tab_spec = pl.BlockSpec((tm, SLAB), lambda i: (i % tpb, 0))

    def full(a):
        return pl.BlockSpec(a.shape, fixed)

    def out(width, dtype):
        return jax.ShapeDtypeStruct((rows, width), dtype), pl.BlockSpec((tm, width), row)

    if tail_rows == rows_per_batch:
        tail = out(B_WIDTH, F32)
    else:
        ntail = tail_rows // tm
        tail = (jax.ShapeDtypeStruct((rows // rows_per_batch * tail_rows, B_WIDTH), F32),
                pl.BlockSpec((tm, B_WIDTH),
                             lambda i: (i // tpb * ntail + jnp.maximum(i % tpb - (tpb - ntail), 0), 0)))

    outs = [out(A_HEADS * SLAB, BF16),
            out(A_KV_RANK, F32),
            out(A_ROPE, F32),
            out(SLAB, BF16),
            out(A_WIDTH, BF16),
            out(B_WIDTH, BF16),
            out(B_WIDTH, BF16),
            out(B_WIDTH, BF16),
            tail,
            tail,
            out(B_WIDTH, BF16),
            out(D_MODEL, BF16),
            out(D_MODEL, BF16)]
    return pl.pallas_call(
        functools.partial(_front_kernel, per_row=per_row),
        out_shape=[o[0] for o in outs],
        grid=(nt,),
        in_specs=[pl.BlockSpec((tm, D_MODEL), row), mod_spec, mod_spec,
                  tab_spec, tab_spec, tab_spec] + [full(a) for a in consts],
        out_specs=[o[1] for o in outs],
        compiler_params=pltpu.CompilerParams(dimension_semantics=("arbitrary",),
                                             vmem_limit_bytes=VMEM_LIMIT),
        name="front",
    )(x2d, scale, shift, *rope_tabs, *consts)


def _expand_kernel(lat_ref, krs_ref, wuk_ref, wuv_ref, gk_ref, segk_ref, ka_ref, va_ref,
                   *, v_transposed):
    latb = lat_ref[...].astype(BF16)
    krs = krs_ref[...].astype(F32)
    segk = segk_ref[...]
    for p in range(A_HEADS // 2):
        c0 = 2 * SLAB * p
        kn = _dot(latb, wuk_ref[:, c0:c0 + 2 * SLAB])
        kn = _seg_rms(kn, segk, gk_ref[:, c0:c0 + 2 * SLAB])
        for s in range(2):
            ka_ref[:, c0 + s * SLAB:c0 + (s + 1) * SLAB] = (
                kn[:, s * SLAB:(s + 1) * SLAB] + krs).astype(BF16)
    if v_transposed:
        vt = _dot_nt(wuv_ref[...], latb).astype(BF16)
        for hp in range(A_HEADS // 2):
            va_ref[0, hp, 0] = vt[hp * SLAB:(hp + 1) * SLAB, :]
    else:
        va_ref[...] = _dot(latb, wuv_ref[...]).astype(BF16)


def _expand(lat2d, krs2d, wuk_p, wuv, gk, segk, *, tm, rows_per_batch, v_transposed):
    rows = lat2d.shape[0]
    row = lambda i: (i, 0)
    fixed = lambda i: (0, 0)
    if v_transposed:
        tpb = rows_per_batch // tm
        v_shape = jax.ShapeDtypeStruct((rows // rows_per_batch, A_HEADS // 2, tpb, SLAB, tm), BF16)
        v_spec = pl.BlockSpec((1, A_HEADS // 2, 1, SLAB, tm), lambda i: (i // tpb, 0, i % tpb, 0, 0))
    else:
        v_shape = jax.ShapeDtypeStruct((rows, A_WIDTH), BF16)
        v_spec = pl.BlockSpec((tm, A_WIDTH), row)
    return pl.pallas_call(
        functools.partial(_expand_kernel, v_transposed=v_transposed),
        out_shape=[jax.ShapeDtypeStruct((rows, A_HEADS * SLAB), BF16), v_shape],
        grid=(rows // tm,),
        in_specs=[pl.BlockSpec((tm, A_KV_RANK), row), pl.BlockSpec((tm, SLAB), row),
                  pl.BlockSpec(wuk_p.shape, fixed), pl.BlockSpec(wuv.shape, fixed),
                  pl.BlockSpec(gk.shape, fixed), pl.BlockSpec(segk.shape, fixed)],
        out_specs=[pl.BlockSpec((tm, A_HEADS * SLAB), row), v_spec],
        compiler_params=pltpu.CompilerParams(dimension_semantics=("arbitrary",),
                                             vmem_limit_bytes=VMEM_LIMIT),
        name="expand",
    )(lat2d, krs2d, wuk_p, wuv, gk, segk)


def _select_heads(o_even, o_odd):
    lane = lax.broadcasted_iota(jnp.int32, o_even.shape, 1)
    return jnp.where(lane < A_VDIM, o_even, o_odd)


def _mla_prompt_kernel(q_ref, k_ref, vt_ref, o_ref, *, tq, nsub):
    qi = pl.program_id(2)
    tk = vt_ref.shape[-1]
    ratio = tq // tk
    n_full = nsub * ratio * qi
    chunk_delta = (lax.broadcasted_iota(jnp.int32, (tk, tq), 0) // CHUNK
                   - lax.broadcasted_iota(jnp.int32, (tk, tq), 1) // CHUNK)
    ones = jnp.ones((16, tk), BF16)
    chains = [(e, h) for h in range(nsub) for e in range(2)]

    def step(j, carry, work, lagged):
        k0 = pl.multiple_of(j * tk, tk)
        scores = {}
        new = list(carry)

        def qk(c):
            e, h = chains[c]
            k = k_ref[0, pl.ds(k0, tk), e * SLAB:(e + 1) * SLAB]
            q = q_ref[0, h * tq:(h + 1) * tq, e * SLAB:(e + 1) * SLAB]
            scores[c] = _dot_nt(k, q)

        def update(c, max_delta):
            e, h = chains[c]
            m, acc, drift = carry[c]
            s = scores.pop(c)
            if max_delta is not None:
                s = jnp.where(chunk_delta <= max_delta, s, NEG_INF)
            vt1 = jnp.concatenate([vt_ref[0, 0, j, e * A_VDIM:(e + 1) * A_VDIM, :], ones], axis=0)
            bmax = jnp.max(s, axis=0, keepdims=True)
            m_new = jnp.maximum(m, bmax)
            if lagged:
                p = jnp.exp2(s - m).astype(BF16)
                new[c] = (m_new, jnp.exp2(m - m_new) * (acc + _dot(vt1, p)),
                          jnp.maximum(drift, bmax - m))
            else:
                p = jnp.exp2(s - m_new).astype(BF16)
                new[c] = (m_new, jnp.exp2(m - m_new) * acc + _dot(vt1, p), drift)

        for c, _ in work:
            qk(c)
        for c, max_delta in work:
            update(c, max_delta)
        return tuple(new)

    everyone = [(c, None) for c in range(len(chains))]
    diagonal = []
    for d in range(nsub * ratio):
        work = []
        for c, (e, h) in enumerate(chains):
            if d * tk >= (h + 1) * tq:
                continue
            before = (d + 1) * tk <= h * tq
            work.append((c, None if before else (h * tq - d * tk) // CHUNK))
        diagonal.append(work)

    def attend(lagged):
        carry = tuple((jnp.full((1, tq), NEG_INF, F32), jnp.zeros((A_VDIM + 16, tq), F32),
                       jnp.full((1, tq), NEG_INF, F32)) for _ in chains)
        first = jnp.minimum(n_full, 1) if lagged else 0
        carry = lax.fori_loop(0, first, lambda j, cr: step(j, cr, everyone, False), carry)
        carry = lax.fori_loop(first, n_full, lambda j, cr: step(j, cr, everyone, lagged), carry)
        for d, work in enumerate(diagonal):
            carry = step(n_full + d, carry, work, False)
        for h in range(nsub):
            outs = [carry[2 * h + e][1] for e in range(2)]
            outs = [acc[:A_VDIM] / acc[A_VDIM:A_VDIM + 1] for acc in outs]
            o_ref[0, h * tq:(h + 1) * tq, :] = jnp.concatenate(outs, axis=0).T.astype(o_ref.dtype)
        return functools.reduce(jnp.maximum, [jnp.max(cr[2]) for cr in carry])

    worst_drift = attend(True)

    @pl.when(worst_drift > MAX_EXP2_DRIFT)
    def _():
        attend(False)


def _mla_prompt(qa, ka, vt, *, tq, nsub):
    b, s, _ = qa.shape
    tk = vt.shape[-1]
    assert vt.shape == (b, A_HEADS // 2, s // tk, SLAB, tk) and tq % tk == 0
    tqq = nsub * tq
    return pl.pallas_call(
        functools.partial(_mla_prompt_kernel, tq=tq, nsub=nsub),
        out_shape=jax.ShapeDtypeStruct((b, s, A_WIDTH), BF16),
        grid=(b, A_HEADS // 2, s // tqq),
        in_specs=[pl.BlockSpec((1, tqq, 2 * SLAB), lambda bi, hp, qi: (bi, qi, hp)),
                  pl.BlockSpec((1, s, 2 * SLAB), lambda bi, hp, qi: (bi, 0, hp)),
                  pl.BlockSpec((1, 1, s // tk, SLAB, tk), lambda bi, hp, qi: (bi, hp, 0, 0, 0))],
        out_specs=pl.BlockSpec((1, tqq, SLAB), lambda bi, hp, qi: (bi, qi, hp)),
        compiler_params=pltpu.CompilerParams(
            dimension_semantics=("arbitrary", "arbitrary", "arbitrary"),
            vmem_limit_bytes=VMEM_LIMIT),
        name="mla_prompt",
    )(qa, ka, vt)


def _mla_sample_kernel(q_ref, k_ref, v_ref, o_ref):
    v = v_ref[0]
    outs = []
    for e in range(2):
        s = _dot_nt(q_ref[0, :, e * SLAB:(e + 1) * SLAB], k_ref[0, :, e * SLAB:(e + 1) * SLAB])
        p = jnp.exp2(s - jnp.max(s, axis=-1, keepdims=True))
        l = jnp.sum(p, axis=-1, keepdims=True)
        outs.append(_dot(p.astype(BF16), v) / l)
    o_ref[0] = _select_heads(outs[0], outs[1]).astype(o_ref.dtype)


def _mla_sample(qa, ka, va):
    b, t, _ = qa.shape
    n = ka.shape[1]
    return pl.pallas_call(
        _mla_sample_kernel,
        out_shape=jax.ShapeDtypeStruct((b, t, A_WIDTH), BF16),
        grid=(b, A_HEADS // 2),
        in_specs=[pl.BlockSpec((1, t, 2 * SLAB), lambda bi, hp: (bi, 0, hp)),
                  pl.BlockSpec((1, n, 2 * SLAB), lambda bi, hp: (bi, 0, hp)),
                  pl.BlockSpec((1, n, SLAB), lambda bi, hp: (bi, 0, hp))],
        out_specs=pl.BlockSpec((1, t, SLAB), lambda bi, hp: (bi, 0, hp)),
        compiler_params=pltpu.CompilerParams(dimension_semantics=("arbitrary", "arbitrary"),
                                             vmem_limit_bytes=VMEM_LIMIT),
        name="mla_sample",
    )(qa, ka, va)


BAND_TQ = 256
BAND_KEYS = B_WINDOW + BAND_TQ
TOEPLITZ_W = 1024


def _band_table_kernel(rb_ref, onehot_ref, allowed_ref, tb_ref, tp_ref):
    g = jnp.dot(rb_ref[...], onehot_ref[...], preferred_element_type=F32,
                precision=lax.Precision.HIGHEST) * LOG2E
    allowed = allowed_ref[...] > 0.0
    for h in range(B_HEADS):
        row = jnp.broadcast_to(g[h:h + 1, :], (BAND_TQ, TOEPLITZ_W))
        t = pltpu.roll(row, 0, axis=1, stride=1, stride_axis=0)[:, :BAND_KEYS]
        tb_ref[h] = t
        tp_ref[h] = jnp.where(allowed, t, NEG_INF)


def _band_tables(rel_bias):
    n_rel = rel_bias.shape[1]
    n_pad = -(-n_rel // LANES) * LANES
    x = np.arange(TOEPLITZ_W)
    key_minus_query = np.where(x < BAND_KEYS, x, x - TOEPLITZ_W)
    dist = B_WINDOW - key_minus_query
    idx = np.clip(dist, -B_MAX_REL, B_MAX_REL) + B_MAX_REL
    onehot = np.zeros((n_pad, TOEPLITZ_W), np.float32)
    onehot[idx, x] = 1.0
    q_chunk = np.arange(BAND_TQ) // CHUNK + B_LEFT_CHUNKS
    k_chunk = np.arange(BAND_KEYS) // CHUNK
    allowed = (k_chunk[None, :] <= q_chunk[:, None]) & (k_chunk[None, :] >= q_chunk[:, None] - B_LEFT_CHUNKS)
    rb = jnp.pad(rel_bias, ((0, 0), (0, n_pad - n_rel)))
    shape = jax.ShapeDtypeStruct((B_HEADS, BAND_TQ, BAND_KEYS), F32)
    return pl.pallas_call(
        _band_table_kernel,
        out_shape=[shape, shape],
        compiler_params=pltpu.CompilerParams(vmem_limit_bytes=VMEM_LIMIT),
        name="band_table",
    )(rb, jnp.asarray(onehot), jnp.asarray(allowed.astype(np.float32)))


def _head_lane_masks(width):
    lane = lax.broadcasted_iota(jnp.int32, (1, width), 1)
    return [(lane < B_HDIM), (lane >= B_HDIM)]


def _band_prompt_kernel(q_ref, k0_ref, k1_ref, k2_ref, v0_ref, v1_ref, v2_ref, t_ref, o_ref,
                        *, tq, nkb):
    qi = pl.program_id(2)
    q = q_ref[0]
    ks = [k0_ref[0], k1_ref[0], k2_ref[0]]
    vs = [v0_ref[0], v1_ref[0], v2_ref[0]]
    masks = _head_lane_masks(SLAB)
    outs = []
    for e in range(2):
        qm = jnp.where(masks[e], q, jnp.zeros_like(q))
        ss = []
        for j in range(nkb):
            s = _dot_nt(qm, ks[j]) + t_ref[e, :, j * tq:(j + 1) * tq]
            if j < nkb - 1:
                s = jnp.where(qi + (j - (nkb - 1)) >= 0, s, NEG_INF)
            ss.append(s)
        m = functools.reduce(jnp.maximum, [jnp.max(s, axis=-1, keepdims=True) for s in ss])
        ps = [jnp.exp2(s - m) for s in ss]
        l = sum(jnp.sum(p, axis=-1, keepdims=True) for p in ps)
        o = sum(_dot(p.astype(BF16), v) for p, v in zip(ps, vs))
        outs.append(o / l)
    o_ref[0] = _select_heads(outs[0], outs[1]).astype(o_ref.dtype)


def _band_prompt(qb, kb, vb, table, *, tq):
    b, s, _ = qb.shape
    nkb = B_WINDOW // tq + 1
    assert nkb == 3

    def kv_spec(j):
        return pl.BlockSpec((1, tq, SLAB),
                            lambda hp, bi, qi: (bi, jnp.maximum(qi + (j - (nkb - 1)), 0), hp))

    return pl.pallas_call(
        functools.partial(_band_prompt_kernel, tq=tq, nkb=nkb),
        out_shape=jax.ShapeDtypeStruct((b, s, B_WIDTH), BF16),
        grid=(B_HEADS // 2, b, s // tq),
        in_specs=[pl.BlockSpec((1, tq, SLAB), lambda hp, bi, qi: (bi, qi, hp))]
                 + [kv_spec(j) for j in range(nkb)] + [kv_spec(j) for j in range(nkb)]
                 + [pl.BlockSpec((2, tq, nkb * tq), lambda hp, bi, qi: (hp, 0, 0))],
        out_specs=pl.BlockSpec((1, tq, SLAB), lambda hp, bi, qi: (bi, qi, hp)),
        compiler_params=pltpu.CompilerParams(
            dimension_semantics=("arbitrary", "arbitrary", "arbitrary"),
            vmem_limit_bytes=VMEM_LIMIT),
        name="band_prompt",
    )(qb, kb, kb, kb, vb, vb, vb, table)


def _band_sample_kernel(q_ref, k_ref, v_ref, t_ref, o_ref):
    q = q_ref[0]
    k = k_ref[0]
    v = v_ref[0]
    masks = _head_lane_masks(SLAB)
    outs = []
    for e in range(2):
        qm = jnp.where(masks[e], q, jnp.zeros_like(q))
        s = _dot_nt(qm, k) + t_ref[e, :, :k.shape[0]]
        p = jnp.exp2(s - jnp.max(s, axis=-1, keepdims=True))
        l = jnp.sum(p, axis=-1, keepdims=True)
        outs.append(_dot(p.astype(BF16), v) / l)
    o_ref[0] = _select_heads(outs[0], outs[1]).astype(o_ref.dtype)


def _band_sample(qb, kb, vb, table):
    b, t, _ = qb.shape
    n = kb.shape[1]
    return pl.pallas_call(
        _band_sample_kernel,
        out_shape=jax.ShapeDtypeStruct((b, t, B_WIDTH), BF16),
        grid=(B_HEADS // 2, b),
        in_specs=[pl.BlockSpec((1, t, SLAB), lambda hp, bi: (bi, 0, hp)),
                  pl.BlockSpec((1, n, SLAB), lambda hp, bi: (bi, 0, hp)),
                  pl.BlockSpec((1, n, SLAB), lambda hp, bi: (bi, 0, hp)),
                  pl.BlockSpec((2, t, BAND_KEYS), lambda hp, bi: (hp, 0, 0))],
        out_specs=pl.BlockSpec((1, t, SLAB), lambda hp, bi: (bi, 0, hp)),
        compiler_params=pltpu.CompilerParams(dimension_semantics=("arbitrary", "arbitrary"),
                                             vmem_limit_bytes=VMEM_LIMIT),
        name="band_sample",
    )(qb, kb, vb, table)


def _back_kernel(x_ref, gate_ref, aa_ref, sga_ref, ab_ref, sgb_ref, sma_ref, smb_ref,
                 woa_ref, wob_ref, wout_ref, o_ref, *, per_row):
    gate = gate_ref[...] if per_row else gate_ref[0]
    ua = _dot(aa_ref[...] * sga_ref[...], woa_ref[...])
    ub = _dot(ab_ref[...] * sgb_ref[...], wob_ref[...])
    merged = sma_ref[...].astype(F32) * ua + smb_ref[...].astype(F32) * ub
    o_ref[...] = x_ref[...] + gate * _dot(merged.astype(BF16), wout_ref[...])


def _back(x2d, gate, aa, sga, ab, sgb, sma, smb, woa, wob, wout, *, tm, rows_per_batch, per_row):
    rows = x2d.shape[0]
    row = lambda i: (i, 0)
    fixed = lambda i: (0, 0)
    if per_row:
        gate_spec = pl.BlockSpec((tm, D_MODEL), row)
    else:
        tpb = rows_per_batch // tm
        gate_spec = pl.BlockSpec((1, 1, D_MODEL), lambda i: (i // tpb, 0, 0))
    return pl.pallas_call(
        functools.partial(_back_kernel, per_row=per_row),
        out_shape=jax.ShapeDtypeStruct((rows, D_MODEL), F32),
        grid=(rows // tm,),
        in_specs=[pl.BlockSpec((tm, D_MODEL), row), gate_spec,
                  pl.BlockSpec((tm, A_WIDTH), row), pl.BlockSpec((tm, A_WIDTH), row),
                  pl.BlockSpec((tm, B_WIDTH), row), pl.BlockSpec((tm, B_WIDTH), row),
                  pl.BlockSpec((tm, D_MODEL), row), pl.BlockSpec((tm, D_MODEL), row),
                  pl.BlockSpec(woa.shape, fixed), pl.BlockSpec(wob.shape, fixed),
                  pl.BlockSpec(wout.shape, fixed)],
        out_specs=pl.BlockSpec((tm, D_MODEL), row),
        compiler_params=pltpu.CompilerParams(dimension_semantics=("arbitrary",),
                                             vmem_limit_bytes=VMEM_LIMIT),
        name="back",
    )(x2d, gate, aa, sga, ab, sgb, sma, smb, woa, wob, wout)


def _seg_matrix(group_of_lane, sizes):
    lane = np.arange(2 * SLAB)
    slab = lane // SLAB
    grp = group_of_lane[lane % SLAB]
    same = (slab[:, None] == slab[None, :]) & (grp[:, None] == grp[None, :])
    return jnp.asarray(np.where(same, 1.0 / sizes[grp][None, :], 0.0), dtype=BF16)


def _rope_tables(pos):
    inv = ROPE_BASE ** (-jnp.arange(0, A_ROPE, 2, dtype=F32) / A_ROPE)
    ang = pos.astype(F32)[:, None] * inv[None, :]
    cos, sin = jnp.cos(ang), jnp.sin(ang)
    n = pos.shape[0]
    ones = jnp.ones((n, A_NOPE), F32)
    zeros = jnp.zeros((n, A_NOPE), F32)
    z16 = jnp.zeros((n, HALF_ROPE), F32)
    pad1 = jnp.ones((n, SLAB - A_QK), F32)
    pad0 = jnp.zeros((n, SLAB - A_QK), F32)
    cos_t = jnp.concatenate([ones, cos, cos, pad1], axis=1)
    sin_lo = jnp.concatenate([zeros, -sin, z16, pad0], axis=1)
    sin_hi = jnp.concatenate([zeros, z16, sin, pad0], axis=1)
    return cos_t, sin_lo, sin_hi


def kernel(x_prompt, x_sample, cache_mla_latent, cache_mla_krope, cache_band_k, cache_band_v,
           c_prompt, c_sample, g_norm, w_ada, b_ada, w_in, g_q_lat, w_uq, g_kv_lat, w_uk, w_uv,
           g_qn_a, g_qr_a, g_kn_a, g_kr_a, g_q_b, g_k_b, rel_bias_b, w_oa, w_ob, w_out):
    bp, s, _ = x_prompt.shape
    bs, t, _ = x_sample.shape
    past = cache_mla_latent.shape[2]
    win_s = cache_band_k.shape[2]
    win_p = min(B_WINDOW, s)
    depth = g_norm.shape[0]
    assert depth == 1

    seg_q = _seg_matrix(np.where(np.arange(SLAB) < A_NOPE, 0, np.where(np.arange(SLAB) < A_QK, 1, 2)),
                        np.array([A_NOPE, A_ROPE, SLAB - A_QK], np.float64))
    seg_64 = _seg_matrix(np.arange(SLAB) // B_HDIM, np.array([B_HDIM, B_HDIM], np.float64))
    seg_k = _seg_matrix(np.where(np.arange(SLAB) < A_NOPE, 0, 1),
                        np.array([A_NOPE, SLAB - A_NOPE], np.float64))

    assert win_s == B_WINDOW and t <= BAND_TQ
    pos_s = past + np.arange(t)

    l = 0
    zpad = jnp.zeros((D_MODEL, SLAB - A_QK), F32)
    z64 = jnp.zeros((D_MODEL, A_NOPE), F32)
    win = w_in[l]
    c_kr0 = A_Q_RANK + A_KV_RANK
    win_p_ = jnp.concatenate([win[:, :c_kr0], z64, win[:, c_kr0:c_kr0 + A_ROPE], zpad,
                              win[:, c_kr0 + A_ROPE:]], axis=1).astype(BF16)
    wuq_p = jnp.pad(w_uq[l], ((0, 0), (0, 0), (0, SLAB - A_QK))).reshape(A_Q_RANK, -1).astype(BF16)
    wuk_p = jnp.pad(w_uk[l], ((0, 0), (0, 0), (0, SLAB - A_NOPE))).reshape(A_KV_RANK, -1).astype(BF16)
    wuv_b = w_uv[l].reshape(A_KV_RANK, -1).astype(BF16)
    wuv_t = wuv_b.T
    qscale = A_SCALE * LOG2E
    gq = jnp.tile(jnp.concatenate([g_qn_a[l], g_qr_a[l], jnp.zeros((SLAB - A_QK,), F32)]) * qscale,
                  A_HEADS)[None]
    gk = jnp.tile(jnp.concatenate([g_kn_a[l], jnp.zeros((SLAB - A_NOPE,), F32)]), A_HEADS)[None]
    gkr = jnp.concatenate([jnp.zeros((A_NOPE,), F32), g_kr_a[l], jnp.zeros((SLAB - A_QK,), F32)])[None]
    gqb = jnp.tile(g_q_b[l] * (B_SCALE * LOG2E), B_HEADS)[None]
    gkb = jnp.tile(g_k_b[l], B_HEADS)[None]
    consts = (g_norm[l][None], win_p_, g_q_lat[l][None], wuq_p, g_kv_lat[l][None], gq, gkr,
              gqb, gkb, seg_q, seg_64)
    woa, wob, wout = w_oa[l].astype(BF16), w_ob[l].astype(BF16), w_out[l].astype(BF16)

    c_rows = jnp.concatenate([c_prompt, jnp.zeros((8 - bp, D_MODEL), F32),
                              jnp.repeat(c_sample, t, axis=0)], axis=0)
    mod = _adaln(c_rows, w_ada[l], b_ada[l])
    shift, scale, gate = mod[:, :D_MODEL], mod[:, D_MODEL:2 * D_MODEL], mod[:, 2 * D_MODEL:]

    tm = 512
    xp2d = x_prompt.reshape(bp * s, D_MODEL)
    (qa, lat, kr, krs, sga, qb, kb, vb, kbf, vbf, sgb, sma, smb) = _front(
        xp2d, scale[:bp, None], shift[:bp, None], _rope_tables(jnp.arange(s)), consts,
        tm=tm, rows_per_batch=s, per_row=False, tail_rows=win_p)
    tk_mla = 512
    ka, vt = _expand(lat, krs, wuk_p, wuv_t, gk, seg_k, tm=tk_mla, rows_per_batch=s, v_transposed=True)
    attn_a = _mla_prompt(qa.reshape(bp, s, -1), ka.reshape(bp, s, -1), vt, tq=512, nsub=2)
    table_plain, table_band = _band_tables(rel_bias_b[l])
    attn_b = _band_prompt(qb.reshape(bp, s, -1), kb.reshape(bp, s, -1), vb.reshape(bp, s, -1),
                          table_band, tq=BAND_TQ)
    y_prompt = _back(xp2d, gate[:bp, None], attn_a.reshape(bp * s, -1), sga,
                     attn_b.reshape(bp * s, -1), sgb, sma, smb, woa, wob, wout,
                     tm=512, rows_per_batch=s, per_row=False).reshape(bp, s, D_MODEL)

    xs2d = x_sample.reshape(bs * t, D_MODEL)
    rows_s = bs * t
    pos_tab = _rope_tables(jnp.asarray(pos_s))
    pos_tab = tuple(jnp.tile(a, (bs, 1)) for a in pos_tab)
    (qa2, lat2, kr2, krs2, sga2, qb2, kb2, vb2, kbf2, vbf2, sgb2, sma2, smb2) = _front(
        xs2d, scale[8:], shift[8:], pos_tab, consts, tm=rows_s, rows_per_batch=t, per_row=True,
        tail_rows=t)
    lat_all = jnp.concatenate([cache_mla_latent[l], lat2.reshape(bs, t, -1)], axis=1)
    krs_cache = jnp.pad(cache_mla_krope[l], ((0, 0), (0, 0), (A_NOPE, SLAB - A_QK))).astype(BF16)
    krs_all = jnp.concatenate([krs_cache, krs2.reshape(bs, t, -1)], axis=1)
    n_all = past + t
    ka2, va2 = _expand(lat_all.reshape(bs * n_all, -1), krs_all.reshape(bs * n_all, -1),
                       wuk_p, wuv_b, gk, seg_k, tm=256, rows_per_batch=n_all, v_transposed=False)
    attn_a2 = _mla_sample(qa2.reshape(bs, t, -1), ka2.reshape(bs, n_all, -1), va2.reshape(bs, n_all, -1))
    kb_all = jnp.concatenate([cache_band_k[l].reshape(bs, win_s, -1).astype(BF16),
                              kb2.reshape(bs, t, -1)], axis=1)
    vb_all = jnp.concatenate([cache_band_v[l].reshape(bs, win_s, -1).astype(BF16),
                              vb2.reshape(bs, t, -1)], axis=1)
    attn_b2 = _band_sample(qb2.reshape(bs, t, -1), kb_all, vb_all, table_plain)
    y_sample = _back(xs2d, gate[8:], attn_a2.reshape(rows_s, -1), sga2, attn_b2.reshape(rows_s, -1),
                     sgb2, sma2, smb2, woa, wob, wout,
                     tm=rows_s, rows_per_batch=t, per_row=True).reshape(bs, t, D_MODEL)

    return (y_prompt, y_sample,
            lat.reshape(1, bp, s, A_KV_RANK), kr.reshape(1, bp, s, A_ROPE),
            kbf.reshape(1, bp, win_p, B_HEADS, B_HDIM), vbf.reshape(1, bp, win_p, B_HEADS, B_HDIM),
            lat2.reshape(1, bs, t, A_KV_RANK), kr2.reshape(1, bs, t, A_ROPE),
            kbf2.reshape(1, bs, t, B_HEADS, B_HDIM), vbf2.reshape(1, bs, t, B_HEADS, B_HDIM))
```

```python
import functools
import math

import jax
import jax.numpy as jnp
import numpy as np
from jax import lax
from jax.experimental import pallas as pl
from jax.experimental.pallas import tpu as pltpu

F32 = jnp.float32
BF16 = jnp.bfloat16

D_MODEL = 1024
CHUNK = 64
A_HEADS = 8
A_NOPE = 64
A_ROPE = 32
A_VDIM = 64
A_QK = A_NOPE + A_ROPE
A_Q_RANK = 384
A_KV_RANK = 256
A_WIDTH = A_HEADS * A_VDIM
A_SCALE = A_QK ** -0.5
B_HEADS = 8
B_HDIM = 64
B_WIDTH = B_HEADS * B_HDIM
B_LEFT_CHUNKS = 8
B_WINDOW = B_LEFT_CHUNKS * CHUNK
B_MAX_REL = 256
B_SCALE = B_HDIM ** -0.5
ROPE_BASE = 10000.0
NORM_EPS = 1e-6
NEG_INF = -1e30
LOG2E = math.log2(math.e)
MAX_EXP2_DRIFT = 60.0

LANES = 128
SLAB = LANES
HALF_ROPE = A_ROPE // 2
ROPE_LANE0 = A_NOPE
VMEM_LIMIT = 48 * 1024 * 1024

C_QLAT = 0
C_KVLAT = C_QLAT + A_Q_RANK
C_KR = C_KVLAT + A_KV_RANK
C_GA = C_KR + SLAB
C_QB = C_GA + A_WIDTH
C_KB = C_QB + B_WIDTH
C_VB = C_KB + B_WIDTH
C_GB = C_VB + B_WIDTH
C_MA = C_GB + B_WIDTH
C_MB = C_MA + D_MODEL
C_END = C_MB + D_MODEL

NT = (((1,), (1,)), ((), ()))


def _dot(a, b):
    return jnp.dot(a, b, preferred_element_type=F32)


def _dot_nt(a, b):
    return lax.dot_general(a, b, NT, preferred_element_type=F32)


def _rms_full(x, g):
    ms = jnp.mean(x * x, axis=-1, keepdims=True)
    return x * lax.rsqrt(ms + NORM_EPS) * g


def _seg_rms(x, seg, g):
    ms = _dot((x * x).astype(BF16), seg)
    return x * lax.rsqrt(ms + NORM_EPS) * g


def _rope_slab(x, cos_t, sin_lo, sin_hi):
    return (x * cos_t
            + pltpu.roll(x, SLAB - HALF_ROPE, axis=1) * sin_lo
            + pltpu.roll(x, HALF_ROPE, axis=1) * sin_hi)


def _adaln_kernel(c_ref, w_ref, b_ref, o_ref):
    c = c_ref[...]
    sc = c * jax.nn.sigmoid(c)
    o_ref[...] = jnp.dot(sc, w_ref[...], preferred_element_type=F32,
                         precision=lax.Precision.HIGHEST) + b_ref[...]


def _adaln(c_rows, w_ada, b_ada):
    n = c_rows.shape[0]
    tn = 1024
    return pl.pallas_call(
        _adaln_kernel,
        out_shape=jax.ShapeDtypeStruct((n, 3 * D_MODEL), F32),
        grid=(3 * D_MODEL // tn,),
        in_specs=[pl.BlockSpec((n, D_MODEL), lambda j: (0, 0)),
                  pl.BlockSpec((D_MODEL, tn), lambda j: (0, j)),
                  pl.BlockSpec((1, tn), lambda j: (0, j))],
        out_specs=pl.BlockSpec((n, tn), lambda j: (0, j)),
        compiler_params=pltpu.CompilerParams(dimension_semantics=("arbitrary",),
                                             vmem_limit_bytes=VMEM_LIMIT),
        name="adaln",
    )(c_rows, w_ada, b_ada.reshape(1, -1))


def _front_kernel(x_ref, scale_ref, shift_ref, cos_ref, slo_ref, shi_ref,
                  gnorm_ref, win_ref, gql_ref, wuq_ref, gkv_ref, gq_ref, gkr_ref,
                  gqb_ref, gkb_ref, segq_ref, seg64_ref,
                  qa_ref, lat_ref, kr_ref, krs_ref, sga_ref, qb_ref, kb_ref, vb_ref,
                  kbf_ref, vbf_ref, sgb_ref, sma_ref, smb_ref, *, per_row):
    x = x_ref[...]
    if per_row:
        scale, shift = scale_ref[...], shift_ref[...]
    else:
        scale, shift = scale_ref[0], shift_ref[0]
    h = _rms_full(x, gnorm_ref[...]) * (1.0 + scale) + shift
    hb = h.astype(BF16)
    cos_t, sin_lo, sin_hi = cos_ref[...], slo_ref[...], shi_ref[...]

    def proj(c0, c1):
        return _dot(hb, win_ref[:, c0:c1])

    segq, seg64 = segq_ref[...], seg64_ref[...]
    pair = 2 * SLAB

    def finish_qa(p, qa, ms):
        qa = qa * lax.rsqrt(ms + NORM_EPS) * gq_ref[:, p * pair:(p + 1) * pair]
        for s in range(2):
            slab = _rope_slab(qa[:, s * SLAB:(s + 1) * SLAB], cos_t, sin_lo, sin_hi)
            qa_ref[:, p * pair + s * SLAB:p * pair + (s + 1) * SLAB] = slab.astype(BF16)

    def finish_qb(p, qb, ms):
        qb = qb * lax.rsqrt(ms + NORM_EPS) * gqb_ref[:, p * pair:(p + 1) * pair]
        qb_ref[:, p * pair:(p + 1) * pair] = qb.astype(BF16)

    def finish_kb(p, kb, ms):
        kb = kb * lax.rsqrt(ms + NORM_EPS) * gkb_ref[:, p * pair:(p + 1) * pair]
        kbf_ref[:, p * pair:(p + 1) * pair] = kb
        kb_ref[:, p * pair:(p + 1) * pair] = kb.astype(BF16)

    def seg_ms(x, seg):
        return _dot((x * x).astype(BF16), seg)

    low_rank = proj(C_QLAT, C_GA)
    ga = proj(C_GA, C_QB)
    sga_ref[...] = (ga * jax.nn.sigmoid(ga)).astype(BF16)

    r = _rms_full(low_rank[:, C_QLAT:C_KVLAT], gql_ref[...]).astype(BF16)
    qa = [None] * (A_HEADS // 2)
    qa[0] = _dot(r, wuq_ref[:, 0:pair])
    qb0 = proj(C_QB, C_QB + pair)
    qa[1] = _dot(r, wuq_ref[:, pair:2 * pair])
    finish_qa(0, qa[0], seg_ms(qa[0], segq))
    kb0 = proj(C_KB, C_KB + pair)
    qa[2] = _dot(r, wuq_ref[:, 2 * pair:3 * pair])
    finish_qa(1, qa[1], seg_ms(qa[1], segq))
    gb = proj(C_GB, C_MA)
    sgb_ref[...] = (gb * jax.nn.sigmoid(gb)).astype(BF16)
    qa[3] = _dot(r, wuq_ref[:, 3 * pair:4 * pair])
    finish_qa(2, qa[2], seg_ms(qa[2], segq))
    finish_qb(0, qb0, seg_ms(qb0, seg64))
    sma_ref[...] = jax.nn.sigmoid(proj(C_MA, C_MB)).astype(BF16)
    finish_qa(3, qa[3], seg_ms(qa[3], segq))
    finish_kb(0, kb0, seg_ms(kb0, seg64))
    qb1 = proj(C_QB + pair, C_QB + 2 * pair)
    kb1 = proj(C_KB + pair, C_KB + 2 * pair)
    smb_ref[...] = jax.nn.sigmoid(proj(C_MB, C_END)).astype(BF16)
    finish_qb(1, qb1, seg_ms(qb1, seg64))
    finish_kb(1, kb1, seg_ms(kb1, seg64))
    vb = proj(C_VB, C_GB)
    vbf_ref[...] = vb
    vb_ref[...] = vb.astype(BF16)

    lat_ref[...] = _rms_full(low_rank[:, C_KVLAT:C_KR], gkv_ref[...])
    krs = low_rank[:, C_KR:C_GA]
    ms = jnp.sum(krs * krs, axis=-1, keepdims=True) * (1.0 / A_ROPE)
    krs = _rope_slab(krs * lax.rsqrt(ms + NORM_EPS) * gkr_ref[...], cos_t, sin_lo, sin_hi)
    krs_ref[...] = krs.astype(BF16)
    kr_ref[...] = pltpu.roll(krs, SLAB - ROPE_LANE0, axis=1)[:, :A_ROPE]


def _front(x2d, scale, shift, rope_tabs, consts, *, tm, rows_per_batch, per_row, tail_rows):
    rows = x2d.shape[0]
    nt = rows // tm
    row = lambda i: (i, 0)
    fixed = lambda i: (0, 0)
    tpb = rows_per_batch // tm
    if per_row:
        mod_spec = pl.BlockSpec((tm, D_MODEL), row)
        tab_spec = pl.BlockSpec((tm, SLAB), row)
    else:
        mod_spec = pl.BlockSpec((1, 1, D_MODEL), lambda i: (i // tpb, 0, 0))
        tab_spec = pl.BlockSpec((tm, SLAB), lambda i: (i % tpb, 0))

    def full(a):
        return pl.BlockSpec(a.shape, fixed)

    def out(width, dtype):
        return jax.ShapeDtypeStruct((rows, width), dtype), pl.BlockSpec((tm, width), row)

    if tail_rows == rows_per_batch:
        tail = out(B_WIDTH, F32)
    else:
        ntail = tail_rows // tm
        tail = (jax.ShapeDtypeStruct((rows // rows_per_batch * tail_rows, B_WIDTH), F32),
                pl.BlockSpec((tm, B_WIDTH),
                             lambda i: (i // tpb * ntail + jnp.maximum(i % tpb - (tpb - ntail), 0), 0)))

    outs = [out(A_HEADS * SLAB, BF16),
            out(A_KV_RANK, F32),
            out(A_ROPE, F32),
            out(SLAB, BF16),
            out(A_WIDTH, BF16),
            out(B_WIDTH, BF16),
            out(B_WIDTH, BF16),
            out(B_WIDTH, BF16),
            tail,
            tail,
            out(B_WIDTH, BF16),
            out(D_MODEL, BF16),
            out(D_MODEL, BF16)]
    return pl.pallas_call(
        functools.partial(_front_kernel, per_row=per_row),
        out_shape=[o[0] for o in outs],
        grid=(nt,),
        in_specs=[pl.BlockSpec((tm, D_MODEL), row), mod_spec, mod_spec,
                  tab_spec, tab_spec, tab_spec] + [full(a) for a in consts],
        out_specs=[o[1] for o in outs],
        compiler_params=pltpu.CompilerParams(dimension_semantics=("arbitrary",),
                                             vmem_limit_bytes=VMEM_LIMIT),
        name="front",
    )(x2d, scale, shift, *rope_tabs, *consts)


def _expand_kernel(lat_ref, krs_ref, wuk_ref, wuv_ref, gk_ref, segk_ref, ka_ref, va_ref,
                   *, v_transposed):
    latb = lat_ref[...].astype(BF16)
    krs = krs_ref[...].astype(F32)
    segk = segk_ref[...]
    for p in range(A_HEADS // 2):
        c0 = 2 * SLAB * p
        kn = _dot(latb, wuk_ref[:, c0:c0 + 2 * SLAB])
        kn = _seg_rms(kn, segk, gk_ref[:, c0:c0 + 2 * SLAB])
        for s in range(2):
            ka_ref[:, c0 + s * SLAB:c0 + (s + 1) * SLAB] = (
                kn[:, s * SLAB:(s + 1) * SLAB] + krs).astype(BF16)
    if v_transposed:
        vt = _dot_nt(wuv_ref[...], latb).astype(BF16)
        for hp in range(A_HEADS // 2):
            va_ref[0, hp, 0] = vt[hp * SLAB:(hp + 1) * SLAB, :]
    else:
        va_ref[...] = _dot(latb, wuv_ref[...]).astype(BF16)


def _expand(lat2d, krs2d, wuk_p, wuv, gk, segk, *, tm, rows_per_batch, v_transposed):
    rows = lat2d.shape[0]
    row = lambda i: (i, 0)
    fixed = lambda i: (0, 0)
    if v_transposed:
        tpb = rows_per_batch // tm
        v_shape = jax.ShapeDtypeStruct((rows // rows_per_batch, A_HEADS // 2, tpb, SLAB, tm), BF16)
        v_spec = pl.BlockSpec((1, A_HEADS // 2, 1, SLAB, tm), lambda i: (i // tpb, 0, i % tpb, 0, 0))
    else:
        v_shape = jax.ShapeDtypeStruct((rows, A_WIDTH), BF16)
        v_spec = pl.BlockSpec((tm, A_WIDTH), row)
    return pl.pallas_call(
        functools.partial(_expand_kernel, v_transposed=v_transposed),
        out_shape=[jax.ShapeDtypeStruct((rows, A_HEADS * SLAB), BF16), v_shape],
        grid=(rows // tm,),
        in_specs=[pl.BlockSpec((tm, A_KV_RANK), row), pl.BlockSpec((tm, SLAB), row),
                  pl.BlockSpec(wuk_p.shape, fixed), pl.BlockSpec(wuv.shape, fixed),
                  pl.BlockSpec(gk.shape, fixed), pl.BlockSpec(segk.shape, fixed)],
        out_specs=[pl.BlockSpec((tm, A_HEADS * SLAB), row), v_spec],
        compiler_params=pltpu.CompilerParams(dimension_semantics=("arbitrary",),
                                             vmem_limit_bytes=VMEM_LIMIT),
        name="expand",
    )(lat2d, krs2d, wuk_p, wuv, gk, segk)


def _select_heads(o_even, o_odd):
    lane = lax.broadcasted_iota(jnp.int32, o_even.shape, 1)
    return jnp.where(lane < A_VDIM, o_even, o_odd)


def _mla_prompt_kernel(q_ref, k_ref, vt_ref, o_ref, *, tq, nsub):
    qi = pl.program_id(2)
    tk = vt_ref.shape[-1]
    ratio = tq // tk
    n_full = nsub * ratio * qi
    chunk_delta = (lax.broadcasted_iota(jnp.int32, (tk, tq), 0) // CHUNK
                   - lax.broadcasted_iota(jnp.int32, (tk, tq), 1) // CHUNK)
    ones = jnp.ones((16, tk), BF16)
    chains = [(e, h) for h in range(nsub) for e in range(2)]

    def step(j, carry, work, lagged):
        k0 = pl.multiple_of(j * tk, tk)
        scores = {}
        new = list(carry)

        def qk(c):
            e, h = chains[c]
            k = k_ref[0, pl.ds(k0, tk), e * SLAB:(e + 1) * SLAB]
            q = q_ref[0, h * tq:(h + 1) * tq, e * SLAB:(e + 1) * SLAB]
            scores[c] = _dot_nt(k, q)

        def update(c, max_delta):
            e, h = chains[c]
            m, acc, drift = carry[c]
            s = scores.pop(c)
            if max_delta is not None:
                s = jnp.where(chunk_delta <= max_delta, s, NEG_INF)
            vt1 = jnp.concatenate([vt_ref[0, 0, j, e * A_VDIM:(e + 1) * A_VDIM, :], ones], axis=0)
            bmax = jnp.max(s, axis=0, keepdims=True)
            m_new = jnp.maximum(m, bmax)
            if lagged:
                p = jnp.exp2(s - m).astype(BF16)
                new[c] = (m_new, jnp.exp2(m - m_new) * (acc + _dot(vt1, p)),
                          jnp.maximum(drift, bmax - m))
            else:
                p = jnp.exp2(s - m_new).astype(BF16)
                new[c] = (m_new, jnp.exp2(m - m_new) * acc + _dot(vt1, p), drift)

        for c, _ in work:
            qk(c)
        for c, max_delta in work:
            update(c, max_delta)
        return tuple(new)

    everyone = [(c, None) for c in range(len(chains))]
    diagonal = []
    for d in range(nsub * ratio):
        work = []
        for c, (e, h) in enumerate(chains):
            if d * tk >= (h + 1) * tq:
                continue
            before = (d + 1) * tk <= h * tq
            work.append((c, None if before else (h * tq - d * tk) // CHUNK))
        diagonal.append(work)

    def attend(lagged):
        carry = tuple((jnp.full((1, tq), NEG_INF, F32), jnp.zeros((A_VDIM + 16, tq), F32),
                       jnp.full((1, tq), NEG_INF, F32)) for _ in chains)
        first = jnp.minimum(n_full, 1) if lagged else 0
        carry = lax.fori_loop(0, first, lambda j, cr: step(j, cr, everyone, False), carry)
        carry = lax.fori_loop(first, n_full, lambda j, cr: step(j, cr, everyone, lagged), carry)
        for d, work in enumerate(diagonal):
            carry = step(n_full + d, carry, work, False)
        for h in range(nsub):
            outs = [carry[2 * h + e][1] for e in range(2)]
            outs = [acc[:A_VDIM] / acc[A_VDIM:A_VDIM + 1] for acc in outs]
            o_ref[0, h * tq:(h + 1) * tq, :] = jnp.concatenate(outs, axis=0).T.astype(o_ref.dtype)
        return functools.reduce(jnp.maximum, [jnp.max(cr[2]) for cr in carry])

    worst_drift = attend(True)

    @pl.when(worst_drift > MAX_EXP2_DRIFT)
    def _():
        attend(False)


def _mla_prompt(qa, ka, vt, *, tq, nsub):
    b, s, _ = qa.shape
    tk = vt.shape[-1]
    assert vt.shape == (b, A_HEADS // 2, s // tk, SLAB, tk) and tq % tk == 0
    tqq = nsub * tq
    return pl.pallas_call(
        functools.partial(_mla_prompt_kernel, tq=tq, nsub=nsub),
        out_shape=jax.ShapeDtypeStruct((b, s, A_WIDTH), BF16),
        grid=(b, A_HEADS // 2, s // tqq),
        in_specs=[pl.BlockSpec((1, tqq, 2 * SLAB), lambda bi, hp, qi: (bi, qi, hp)),
                  pl.BlockSpec((1, s, 2 * SLAB), lambda bi, hp, qi: (bi, 0, hp)),
                  pl.BlockSpec((1, 1, s // tk, SLAB, tk), lambda bi, hp, qi: (bi, hp, 0, 0, 0))],
        out_specs=pl.BlockSpec((1, tqq, SLAB), lambda bi, hp, qi: (bi, qi, hp)),
        compiler_params=pltpu.CompilerParams(
            dimension_semantics=("arbitrary", "arbitrary", "arbitrary"),
            vmem_limit_bytes=VMEM_LIMIT),
        name="mla_prompt",
    )(qa, ka, vt)


def _mla_sample_kernel(q_ref, k_ref, v_ref, o_ref):
    v = v_ref[0]
    outs = []
    for e in range(2):
        s = _dot_nt(q_ref[0, :, e * SLAB:(e + 1) * SLAB], k_ref[0, :, e * SLAB:(e + 1) * SLAB])
        p = jnp.exp2(s - jnp.max(s, axis=-1, keepdims=True))
        l = jnp.sum(p, axis=-1, keepdims=True)
        outs.append(_dot(p.astype(BF16), v) / l)
    o_ref[0] = _select_heads(outs[0], outs[1]).astype(o_ref.dtype)


def _mla_sample(qa, ka, va):
    b, t, _ = qa.shape
    n = ka.shape[1]
    return pl.pallas_call(
        _mla_sample_kernel,
        out_shape=jax.ShapeDtypeStruct((b, t, A_WIDTH), BF16),
        grid=(b, A_HEADS // 2),
        in_specs=[pl.BlockSpec((1, t, 2 * SLAB), lambda bi, hp: (bi, 0, hp)),
                  pl.BlockSpec((1, n, 2 * SLAB), lambda bi, hp: (bi, 0, hp)),
                  pl.BlockSpec((1, n, SLAB), lambda bi, hp: (bi, 0, hp))],
        out_specs=pl.BlockSpec((1, t, SLAB), lambda bi, hp: (bi, 0, hp)),
        compiler_params=pltpu.CompilerParams(dimension_semantics=("arbitrary", "arbitrary"),
                                             vmem_limit_bytes=VMEM_LIMIT),
        name="mla_sample",
    )(qa, ka, va)


BAND_TQ = 256
BAND_KEYS = B_WINDOW + BAND_TQ
TOEPLITZ_W = 1024


def _band_table_kernel(rb_ref, onehot_ref, allowed_ref, tb_ref, tp_ref):
    g = jnp.dot(rb_ref[...], onehot_ref[...], preferred_element_type=F32,
                precision=lax.Precision.HIGHEST) * LOG2E
    allowed = allowed_ref[...] > 0.0
    for h in range(B_HEADS):
        row = jnp.broadcast_to(g[h:h + 1, :], (BAND_TQ, TOEPLITZ_W))
        t = pltpu.roll(row, 0, axis=1, stride=1, stride_axis=0)[:, :BAND_KEYS]
        tb_ref[h] = t
        tp_ref[h] = jnp.where(allowed, t, NEG_INF)


def _band_tables(rel_bias):
    n_rel = rel_bias.shape[1]
    n_pad = -(-n_rel // LANES) * LANES
    x = np.arange(TOEPLITZ_W)
    key_minus_query = np.where(x < BAND_KEYS, x, x - TOEPLITZ_W)
    dist = B_WINDOW - key_minus_query
    idx = np.clip(dist, -B_MAX_REL, B_MAX_REL) + B_MAX_REL
    onehot = np.zeros((n_pad, TOEPLITZ_W), np.float32)
    onehot[idx, x] = 1.0
    q_chunk = np.arange(BAND_TQ) // CHUNK + B_LEFT_CHUNKS
    k_chunk = np.arange(BAND_KEYS) // CHUNK
    allowed = (k_chunk[None, :] <= q_chunk[:, None]) & (k_chunk[None, :] >= q_chunk[:, None] - B_LEFT_CHUNKS)
    rb = jnp.pad(rel_bias, ((0, 0), (0, n_pad - n_rel)))
    shape = jax.ShapeDtypeStruct((B_HEADS, BAND_TQ, BAND_KEYS), F32)
    return pl.pallas_call(
        _band_table_kernel,
        out_shape=[shape, shape],
        compiler_params=pltpu.CompilerParams(vmem_limit_bytes=VMEM_LIMIT),
        name="band_table",
    )(rb, jnp.asarray(onehot), jnp.asarray(allowed.astype(np.float32)))


def _head_lane_masks(width):
    lane = lax.broadcasted_iota(jnp.int32, (1, width), 1)
    return [(lane < B_HDIM), (lane >= B_HDIM)]


def _band_prompt_kernel(q_ref, kp_ref, kc_ref, vp_ref, vc_ref, t_ref, o_ref, *, tq, nsub):
    qi = pl.program_id(2)
    nprev = B_WINDOW // tq
    nkb = nprev + 1

    def window(prev_ref, cur_ref, w):
        if w < nprev:
            return prev_ref[0, w * tq:(w + 1) * tq, :]
        return cur_ref[0, (w - nprev) * tq:(w - nprev + 1) * tq, :]

    masks = _head_lane_masks(SLAB)
    units = [(t, e) for t in range(nsub) for e in range(2)]
    scores = {}
    for t, e in units:
        q = q_ref[0, t * tq:(t + 1) * tq, :]
        qm = jnp.where(masks[e], q, jnp.zeros_like(q))
        ss = []
        for j in range(nkb):
            s = _dot_nt(qm, window(kp_ref, kc_ref, t + j)) + t_ref[e, :, j * tq:(j + 1) * tq]
            if t + j < nprev:
                s = jnp.where(qi > 0, s, NEG_INF)
            ss.append(s)
        scores[t, e] = ss
    outs = {}
    for t, e in units:
        ss = scores.pop((t, e))
        m = functools.reduce(jnp.maximum, [jnp.max(s, axis=-1, keepdims=True) for s in ss])
        ps = [jnp.exp2(s - m) for s in ss]
        l = sum(jnp.sum(p, axis=-1, keepdims=True) for p in ps)
        o = sum(_dot(p.astype(BF16), window(vp_ref, vc_ref, t + j)) for j, p in enumerate(ps))
        outs[t, e] = o / l
    for t in range(nsub):
        o_ref[0, t * tq:(t + 1) * tq, :] = _select_heads(outs[t, 0], outs[t, 1]).astype(o_ref.dtype)


def _band_prompt(qb, kb, vb, table, *, tq, nsub):
    b, s, _ = qb.shape
    tqq = nsub * tq
    per_step = tqq // B_WINDOW
    assert tqq % B_WINDOW == 0 and B_WINDOW % tq == 0
    cur = pl.BlockSpec((1, tqq, SLAB), lambda hp, bi, qi: (bi, qi, hp))
    prev = pl.BlockSpec((1, B_WINDOW, SLAB),
                        lambda hp, bi, qi: (bi, jnp.maximum(qi * per_step - 1, 0), hp))
    return pl.pallas_call(
        functools.partial(_band_prompt_kernel, tq=tq, nsub=nsub),
        out_shape=jax.ShapeDtypeStruct((b, s, B_WIDTH), BF16),
        grid=(B_HEADS // 2, b, s // tqq),
        in_specs=[cur, prev, cur, prev, cur,
                  pl.BlockSpec((2, tq, B_WINDOW + tq), lambda hp, bi, qi: (hp, 0, 0))],
        out_specs=cur,
        compiler_params=pltpu.CompilerParams(
            dimension_semantics=("arbitrary", "arbitrary", "arbitrary"),
            vmem_limit_bytes=VMEM_LIMIT),
        name="band_prompt",
    )(qb, kb, kb, vb, vb, table)


def _band_sample_kernel(q_ref, k_ref, v_ref, t_ref, o_ref):
    q = q_ref[0]
    k = k_ref[0]
    v = v_ref[0]
    masks = _head_lane_masks(SLAB)
    outs = []
    for e in range(2):
        qm = jnp.where(masks[e], q, jnp.zeros_like(q))
        s = _dot_nt(qm, k) + t_ref[e, :, :k.shape[0]]
        p = jnp.exp2(s - jnp.max(s, axis=-1, keepdims=True))
        l = jnp.sum(p, axis=-1, keepdims=True)
        outs.append(_dot(p.astype(BF16), v) / l)
    o_ref[0] = _select_heads(outs[0], outs[1]).astype(o_ref.dtype)


def _band_sample(qb, kb, vb, table):
    b, t, _ = qb.shape
    n = kb.shape[1]
    return pl.pallas_call(
        _band_sample_kernel,
        out_shape=jax.ShapeDtypeStruct((b, t, B_WIDTH), BF16),
        grid=(B_HEADS // 2, b),
        in_specs=[pl.BlockSpec((1, t, SLAB), lambda hp, bi: (bi, 0, hp)),
                  pl.BlockSpec((1, n, SLAB), lambda hp, bi: (bi, 0, hp)),
                  pl.BlockSpec((1, n, SLAB), lambda hp, bi: (bi, 0, hp)),
                  pl.BlockSpec((2, t, BAND_KEYS), lambda hp, bi: (hp, 0, 0))],
        out_specs=pl.BlockSpec((1, t, SLAB), lambda hp, bi: (bi, 0, hp)),
        compiler_params=pltpu.CompilerParams(dimension_semantics=("arbitrary", "arbitrary"),
                                             vmem_limit_bytes=VMEM_LIMIT),
        name="band_sample",
    )(qb, kb, vb, table)


def _back_kernel(x_ref, gate_ref, aa_ref, sga_ref, ab_ref, sgb_ref, sma_ref, smb_ref,
                 woa_ref, wob_ref, wout_ref, o_ref, *, per_row):
    gate = gate_ref[...] if per_row else gate_ref[0]
    ua = _dot(aa_ref[...] * sga_ref[...], woa_ref[...])
    ub = _dot(ab_ref[...] * sgb_ref[...], wob_ref[...])
    merged = sma_ref[...].astype(F32) * ua + smb_ref[...].astype(F32) * ub
    o_ref[...] = x_ref[...] + gate * _dot(merged.astype(BF16), wout_ref[...])


def _back(x2d, gate, aa, sga, ab, sgb, sma, smb, woa, wob, wout, *, tm, rows_per_batch, per_row):
    rows = x2d.shape[0]
    row = lambda i: (i, 0)
    fixed = lambda i: (0, 0)
    if per_row:
        gate_spec = pl.BlockSpec((tm, D_MODEL), row)
    else:
        tpb = rows_per_batch // tm
        gate_spec = pl.BlockSpec((1, 1, D_MODEL), lambda i: (i // tpb, 0, 0))
    return pl.pallas_call(
        functools.partial(_back_kernel, per_row=per_row),
        out_shape=jax.ShapeDtypeStruct((rows, D_MODEL), F32),
        grid=(rows // tm,),
        in_specs=[pl.BlockSpec((tm, D_MODEL), row), gate_spec,
                  pl.BlockSpec((tm, A_WIDTH), row), pl.BlockSpec((tm, A_WIDTH), row),
                  pl.BlockSpec((tm, B_WIDTH), row), pl.BlockSpec((tm, B_WIDTH), row),
                  pl.BlockSpec((tm, D_MODEL), row), pl.BlockSpec((tm, D_MODEL), row),
                  pl.BlockSpec(woa.shape, fixed), pl.BlockSpec(wob.shape, fixed),
                  pl.BlockSpec(wout.shape, fixed)],
        out_specs=pl.BlockSpec((tm, D_MODEL), row),
        compiler_params=pltpu.CompilerParams(dimension_semantics=("arbitrary",),
                                             vmem_limit_bytes=VMEM_LIMIT),
        name="back",
    )(x2d, gate, aa, sga, ab, sgb, sma, smb, woa, wob, wout)


def _seg_matrix(group_of_lane, sizes):
    lane = np.arange(2 * SLAB)
    slab = lane // SLAB
    grp = group_of_lane[lane % SLAB]
    same = (slab[:, None] == slab[None, :]) & (grp[:, None] == grp[None, :])
    return jnp.asarray(np.where(same, 1.0 / sizes[grp][None, :], 0.0), dtype=BF16)


def _rope_tables(pos):
    inv = ROPE_BASE ** (-jnp.arange(0, A_ROPE, 2, dtype=F32) / A_ROPE)
    ang = pos.astype(F32)[:, None] * inv[None, :]
    cos, sin = jnp.cos(ang), jnp.sin(ang)
    n = pos.shape[0]
    ones = jnp.ones((n, A_NOPE), F32)
    zeros = jnp.zeros((n, A_NOPE), F32)
    z16 = jnp.zeros((n, HALF_ROPE), F32)
    pad1 = jnp.ones((n, SLAB - A_QK), F32)
    pad0 = jnp.zeros((n, SLAB - A_QK), F32)
    cos_t = jnp.concatenate([ones, cos, cos, pad1], axis=1)
    sin_lo = jnp.concatenate([zeros, -sin, z16, pad0], axis=1)
    sin_hi = jnp.concatenate([zeros, z16, sin, pad0], axis=1)
    return cos_t, sin_lo, sin_hi


def kernel(x_prompt, x_sample, cache_mla_latent, cache_mla_krope, cache_band_k, cache_band_v,
           c_prompt, c_sample, g_norm, w_ada, b_ada, w_in, g_q_lat, w_uq, g_kv_lat, w_uk, w_uv,
           g_qn_a, g_qr_a, g_kn_a, g_kr_a, g_q_b, g_k_b, rel_bias_b, w_oa, w_ob, w_out):
    bp, s, _ = x_prompt.shape
    bs, t, _ = x_sample.shape
    past = cache_mla_latent.shape[2]
    win_s = cache_band_k.shape[2]
    win_p = min(B_WINDOW, s)
    depth = g_norm.shape[0]
    assert depth == 1

    seg_q = _seg_matrix(np.where(np.arange(SLAB) < A_NOPE, 0, np.where(np.arange(SLAB) < A_QK, 1, 2)),
                        np.array([A_NOPE, A_ROPE, SLAB - A_QK], np.float64))
    seg_64 = _seg_matrix(np.arange(SLAB) // B_HDIM, np.array([B_HDIM, B_HDIM], np.float64))
    seg_k = _seg_matrix(np.where(np.arange(SLAB) < A_NOPE, 0, 1),
                        np.array([A_NOPE, SLAB - A_NOPE], np.float64))

    assert win_s == B_WINDOW and t <= BAND_TQ
    pos_s = past + np.arange(t)

    l = 0
    zpad = jnp.zeros((D_MODEL, SLAB - A_QK), F32)
    z64 = jnp.zeros((D_MODEL, A_NOPE), F32)
    win = w_in[l]
    c_kr0 = A_Q_RANK + A_KV_RANK
    win_p_ = jnp.concatenate([win[:, :c_kr0], z64, win[:, c_kr0:c_kr0 + A_ROPE], zpad,
                              win[:, c_kr0 + A_ROPE:]], axis=1).astype(BF16)
    wuq_p = jnp.pad(w_uq[l], ((0, 0), (0, 0), (0, SLAB - A_QK))).reshape(A_Q_RANK, -1).astype(BF16)
    wuk_p = jnp.pad(w_uk[l], ((0, 0), (0, 0), (0, SLAB - A_NOPE))).reshape(A_KV_RANK, -1).astype(BF16)
    wuv_b = w_uv[l].reshape(A_KV_RANK, -1).astype(BF16)
    wuv_t = wuv_b.T
    qscale = A_SCALE * LOG2E
    gq = jnp.tile(jnp.concatenate([g_qn_a[l], g_qr_a[l], jnp.zeros((SLAB - A_QK,), F32)]) * qscale,
                  A_HEADS)[None]
    gk = jnp.tile(jnp.concatenate([g_kn_a[l], jnp.zeros((SLAB - A_NOPE,), F32)]), A_HEADS)[None]
    gkr = jnp.concatenate([jnp.zeros((A_NOPE,), F32), g_kr_a[l], jnp.zeros((SLAB - A_QK,), F32)])[None]
    gqb = jnp.tile(g_q_b[l] * (B_SCALE * LOG2E), B_HEADS)[None]
    gkb = jnp.tile(g_k_b[l], B_HEADS)[None]
    consts = (g_norm[l][None], win_p_, g_q_lat[l][None], wuq_p, g_kv_lat[l][None], gq, gkr,
              gqb, gkb, seg_q, seg_64)
    woa, wob, wout = w_oa[l].astype(BF16), w_ob[l].astype(BF16), w_out[l].astype(BF16)

    c_rows = jnp.concatenate([c_prompt, jnp.zeros((8 - bp, D_MODEL), F32),
                              jnp.repeat(c_sample, t, axis=0)], axis=0)
    mod = _adaln(c_rows, w_ada[l], b_ada[l])
    shift, scale, gate = mod[:, :D_MODEL], mod[:, D_MODEL:2 * D_MODEL], mod[:, 2 * D_MODEL:]

    tm = 512
    xp2d = x_prompt.reshape(bp * s, D_MODEL)
    (qa, lat, kr, krs, sga, qb, kb, vb, kbf, vbf, sgb, sma, smb) = _front(
        xp2d, scale[:bp, None], shift[:bp, None], _rope_tables(jnp.arange(s)), consts,
        tm=tm, rows_per_batch=s, per_row=False, tail_rows=win_p)
    tk_mla = 512
    ka, vt = _expand(lat, krs, wuk_p, wuv_t, gk, seg_k, tm=tk_mla, rows_per_batch=s, v_transposed=True)
    attn_a = _mla_prompt(qa.reshape(bp, s, -1), ka.reshape(bp, s, -1), vt, tq=512, nsub=2)
    table_plain, table_band = _band_tables(rel_bias_b[l])
    attn_b = _band_prompt(qb.reshape(bp, s, -1), kb.reshape(bp, s, -1), vb.reshape(bp, s, -1),
                          table_band, tq=BAND_TQ, nsub=4)
    y_prompt = _back(xp2d, gate[:bp, None], attn_a.reshape(bp * s, -1), sga,
                     attn_b.reshape(bp * s, -1), sgb, sma, smb, woa, wob, wout,
                     tm=512, rows_per_batch=s, per_row=False).reshape(bp, s, D_MODEL)

    xs2d = x_sample.reshape(bs * t, D_MODEL)
    rows_s = bs * t
    pos_tab = _rope_tables(jnp.asarray(pos_s))
    pos_tab = tuple(jnp.tile(a, (bs, 1)) for a in pos_tab)
    (qa2, lat2, kr2, krs2, sga2, qb2, kb2, vb2, kbf2, vbf2, sgb2, sma2, smb2) = _front(
        xs2d, scale[8:], shift[8:], pos_tab, consts, tm=rows_s, rows_per_batch=t, per_row=True,
        tail_rows=t)
    lat_all = jnp.concatenate([cache_mla_latent[l], lat2.reshape(bs, t, -1)], axis=1)
    krs_cache = jnp.pad(cache_mla_krope[l], ((0, 0), (0, 0), (A_NOPE, SLAB - A_QK))).astype(BF16)
    krs_all = jnp.concatenate([krs_cache, krs2.reshape(bs, t, -1)], axis=1)
    n_all = past + t
    ka2, va2 = _expand(lat_all.reshape(bs * n_all, -1), krs_all.reshape(bs * n_all, -1),
                       wuk_p, wuv_b, gk, seg_k, tm=n_all // 2, rows_per_batch=n_all, v_transposed=False)
    attn_a2 = _mla_sample(qa2.reshape(bs, t, -1), ka2.reshape(bs, n_all, -1), va2.reshape(bs, n_all, -1))
    kb_all = jnp.concatenate([cache_band_k[l].reshape(bs, win_s, -1).astype(BF16),
                              kb2.reshape(bs, t, -1)], axis=1)
    vb_all = jnp.concatenate([cache_band_v[l].reshape(bs, win_s, -1).astype(BF16),
                              vb2.reshape(bs, t, -1)], axis=1)
    attn_b2 = _band_sample(qb2.reshape(bs, t, -1), kb_all, vb_all, table_plain)
    y_sample = _back(xs2d, gate[8:], attn_a2.reshape(rows_s, -1), sga2, attn_b2.reshape(rows_s, -1),
                     sgb2, sma2, smb2, woa, wob, wout,
                     tm=rows_s, rows_per_batch=t, per_row=True).reshape(bs, t, D_MODEL)

    return (y_prompt, y_sample,
            lat.reshape(1, bp, s, A_KV_RANK), kr.reshape(1, bp, s, A_ROPE),
            kbf.reshape(1, bp, win_p, B_HEADS, B_HDIM), vbf.reshape(1, bp, win_p, B_HEADS, B_HDIM),
            lat2.reshape(1, bs, t, A_KV_RANK), kr2.reshape(1, bs, t, A_ROPE),
            kbf2.reshape(1, bs, t, B_HEADS, B_HDIM), vbf2.reshape(1, bs, t, B_HEADS, B_HDIM))
```

```python
import functools
import math

import jax
import jax.numpy as jnp
import numpy as np
from jax import lax
from jax.experimental import pallas as pl
from jax.experimental.pallas import tpu as pltpu

F32 = jnp.float32
BF16 = jnp.bfloat16

D_MODEL = 1024
CHUNK = 64
A_HEADS = 8
A_NOPE = 64
A_ROPE = 32
A_VDIM = 64
A_QK = A_NOPE + A_ROPE
A_Q_RANK = 384
A_KV_RANK = 256
A_WIDTH = A_HEADS * A_VDIM
A_SCALE = A_QK ** -0.5
B_HEADS = 8
B_HDIM = 64
B_WIDTH = B_HEADS * B_HDIM
B_LEFT_CHUNKS = 8
B_WINDOW = B_LEFT_CHUNKS * CHUNK
B_MAX_REL = 256
B_SCALE = B_HDIM ** -0.5
ROPE_BASE = 10000.0
NORM_EPS = 1e-6
NEG_INF = -1e30
LOG2E = math.log2(math.e)
MAX_EXP2_DRIFT = 60.0

LANES = 128
SLAB = LANES
HALF_ROPE = A_ROPE // 2
ROPE_LANE0 = A_NOPE
VMEM_LIMIT = 56 * 1024 * 1024

C_QLAT = 0
C_KVLAT = C_QLAT + A_Q_RANK
C_KR = C_KVLAT + A_KV_RANK
C_GA = C_KR + SLAB
C_QB = C_GA + A_WIDTH
C_KB = C_QB + B_WIDTH
C_VB = C_KB + B_WIDTH
C_GB = C_VB + B_WIDTH
C_MA = C_GB + B_WIDTH
C_MB = C_MA + D_MODEL
C_END = C_MB + D_MODEL

NT = (((1,), (1,)), ((), ()))


def _dot(a, b):
    return jnp.dot(a, b, preferred_element_type=F32)


def _dot_nt(a, b):
    return lax.dot_general(a, b, NT, preferred_element_type=F32)


def _rms_full(x, g):
    ms = jnp.mean(x * x, axis=-1, keepdims=True)
    return x * lax.rsqrt(ms + NORM_EPS) * g


def _seg_rms(x, seg, g):
    ms = _dot((x * x).astype(BF16), seg)
    return x * lax.rsqrt(ms + NORM_EPS) * g


def _rope_slab(x, cos_t, sin_lo, sin_hi):
    return (x * cos_t
            + pltpu.roll(x, SLAB - HALF_ROPE, axis=1) * sin_lo
            + pltpu.roll(x, HALF_ROPE, axis=1) * sin_hi)


def _adaln_kernel(c_ref, w_ref, b_ref, o_ref):
    c = c_ref[...]
    sc = c * jax.nn.sigmoid(c)
    o_ref[...] = jnp.dot(sc, w_ref[...], preferred_element_type=F32,
                         precision=lax.Precision.HIGHEST) + b_ref[...]


def _adaln(c_rows, w_ada, b_ada):
    n = c_rows.shape[0]
    tn = 1024
    return pl.pallas_call(
        _adaln_kernel,
        out_shape=jax.ShapeDtypeStruct((n, 3 * D_MODEL), F32),
        grid=(3 * D_MODEL // tn,),
        in_specs=[pl.BlockSpec((n, D_MODEL), lambda j: (0, 0)),
                  pl.BlockSpec((D_MODEL, tn), lambda j: (0, j)),
                  pl.BlockSpec((1, tn), lambda j: (0, j))],
        out_specs=pl.BlockSpec((n, tn), lambda j: (0, j)),
        compiler_params=pltpu.CompilerParams(dimension_semantics=("arbitrary",),
                                             vmem_limit_bytes=VMEM_LIMIT),
        name="adaln",
    )(c_rows, w_ada, b_ada.reshape(1, -1))


def _front_kernel(x_ref, scale_ref, shift_ref, cos_ref, slo_ref, shi_ref,
                  gnorm_ref, win_ref, gql_ref, wuq_ref, gkv_ref, gq_ref, gkr_ref,
                  gqb_ref, gkb_ref, segq_ref, seg64_ref,
                  qa_ref, lat_ref, kr_ref, krs_ref, sga_ref, qb_ref, kb_ref, vb_ref,
                  kbf_ref, vbf_ref, sgb_ref, sma_ref, smb_ref, *, per_row):
    x = x_ref[...]
    if per_row:
        scale, shift = scale_ref[...], shift_ref[...]
    else:
        scale, shift = scale_ref[0], shift_ref[0]
    h = _rms_full(x, gnorm_ref[...]) * (1.0 + scale) + shift
    hb = h.astype(BF16)
    cos_t, sin_lo, sin_hi = cos_ref[...], slo_ref[...], shi_ref[...]

    def proj(c0, c1):
        return _dot(hb, win_ref[:, c0:c1])

    segq, seg64 = segq_ref[...], seg64_ref[...]
    pair = 2 * SLAB

    def finish_qa(p, qa, ms):
        qa = qa * lax.rsqrt(ms + NORM_EPS) * gq_ref[:, p * pair:(p + 1) * pair]
        for s in range(2):
            slab = _rope_slab(qa[:, s * SLAB:(s + 1) * SLAB], cos_t, sin_lo, sin_hi)
            qa_ref[:, p * pair + s * SLAB:p * pair + (s + 1) * SLAB] = slab.astype(BF16)

    def finish_qb(p, qb, ms):
        qb = qb * lax.rsqrt(ms + NORM_EPS) * gqb_ref[:, p * pair:(p + 1) * pair]
        qb_ref[:, p * pair:(p + 1) * pair] = qb.astype(BF16)

    def finish_kb(p, kb, ms):
        kb = kb * lax.rsqrt(ms + NORM_EPS) * gkb_ref[:, p * pair:(p + 1) * pair]
        kbf_ref[:, p * pair:(p + 1) * pair] = kb
        kb_ref[:, p * pair:(p + 1) * pair] = kb.astype(BF16)

    def seg_ms(x, seg):
        return _dot((x * x).astype(BF16), seg)

    low_rank = proj(C_QLAT, C_GA)
    ga = proj(C_GA, C_QB)
    sga_ref[...] = (ga * jax.nn.sigmoid(ga)).astype(BF16)

    r = _rms_full(low_rank[:, C_QLAT:C_KVLAT], gql_ref[...]).astype(BF16)
    qa = [None] * (A_HEADS // 2)
    qa[0] = _dot(r, wuq_ref[:, 0:pair])
    qb0 = proj(C_QB, C_QB + pair)
    qa[1] = _dot(r, wuq_ref[:, pair:2 * pair])
    finish_qa(0, qa[0], seg_ms(qa[0], segq))
    kb0 = proj(C_KB, C_KB + pair)
    qa[2] = _dot(r, wuq_ref[:, 2 * pair:3 * pair])
    finish_qa(1, qa[1], seg_ms(qa[1], segq))
    gb = proj(C_GB, C_MA)
    sgb_ref[...] = (gb * jax.nn.sigmoid(gb)).astype(BF16)
    qa[3] = _dot(r, wuq_ref[:, 3 * pair:4 * pair])
    finish_qa(2, qa[2], seg_ms(qa[2], segq))
    finish_qb(0, qb0, seg_ms(qb0, seg64))
    sma_ref[...] = jax.nn.sigmoid(proj(C_MA, C_MB)).astype(BF16)
    finish_qa(3, qa[3], seg_ms(qa[3], segq))
    finish_kb(0, kb0, seg_ms(kb0, seg64))
    qb1 = proj(C_QB + pair, C_QB + 2 * pair)
    kb1 = proj(C_KB + pair, C_KB + 2 * pair)
    smb_ref[...] = jax.nn.sigmoid(proj(C_MB, C_END)).astype(BF16)
    finish_qb(1, qb1, seg_ms(qb1, seg64))
    finish_kb(1, kb1, seg_ms(kb1, seg64))
    vb = proj(C_VB, C_GB)
    vbf_ref[...] = vb
    vb_ref[...] = vb.astype(BF16)

    lat_ref[...] = _rms_full(low_rank[:, C_KVLAT:C_KR], gkv_ref[...])
    krs = low_rank[:, C_KR:C_GA]
    ms = jnp.sum(krs * krs, axis=-1, keepdims=True) * (1.0 / A_ROPE)
    krs = _rope_slab(krs * lax.rsqrt(ms + NORM_EPS) * gkr_ref[...], cos_t, sin_lo, sin_hi)
    krs_ref[...] = krs.astype(BF16)
    kr_ref[...] = pltpu.roll(krs, SLAB - ROPE_LANE0, axis=1)[:, :A_ROPE]


def _front(x2d, scale, shift, rope_tabs, consts, *, tm, rows_per_batch, per_row, tail_rows):
    rows = x2d.shape[0]
    nt = rows // tm
    row = lambda i: (i, 0)
    fixed = lambda i: (0, 0)
    tpb = rows_per_batch // tm
    if per_row:
        mod_spec = pl.BlockSpec((tm, D_MODEL), row)
        tab_spec = pl.BlockSpec((tm, SLAB), row)
    else:
        mod_spec = pl.BlockSpec((1, 1, D_MODEL), lambda i: (i // tpb, 0, 0))
        tab_spec = pl.BlockSpec((tm, SLAB), lambda i: (i % tpb, 0))

    def full(a):
        return pl.BlockSpec(a.shape, fixed)

    def out(width, dtype):
        return jax.ShapeDtypeStruct((rows, width), dtype), pl.BlockSpec((tm, width), row)

    if tail_rows == rows_per_batch:
        tail = out(B_WIDTH, F32)
    else:
        ntail = tail_rows // tm
        tail = (jax.ShapeDtypeStruct((rows // rows_per_batch * tail_rows, B_WIDTH), F32),
                pl.BlockSpec((tm, B_WIDTH),
                             lambda i: (i // tpb * ntail + jnp.maximum(i % tpb - (tpb - ntail), 0), 0)))

    outs = [out(A_HEADS * SLAB, BF16),
            out(A_KV_RANK, F32),
            out(A_ROPE, F32),
            out(SLAB, BF16),
            out(A_WIDTH, BF16),
            out(B_WIDTH, BF16),
            out(B_WIDTH, BF16),
            out(B_WIDTH, BF16),
            tail,
            tail,
            out(B_WIDTH, BF16),
            out(D_MODEL, BF16),
            out(D_MODEL, BF16)]
    return pl.pallas_call(
        functools.partial(_front_kernel, per_row=per_row),
        out_shape=[o[0] for o in outs],
        grid=(nt,),
        in_specs=[pl.BlockSpec((tm, D_MODEL), row), mod_spec, mod_spec,
                  tab_spec, tab_spec, tab_spec] + [full(a) for a in consts],
        out_specs=[o[1] for o in outs],
        compiler_params=pltpu.CompilerParams(dimension_semantics=("arbitrary",),
                                             vmem_limit_bytes=VMEM_LIMIT),
        name="front",
    )(x2d, scale, shift, *rope_tabs, *consts)


def _expand_kernel(lat_ref, krs_ref, wuk_ref, wuv_ref, gk_ref, segk_ref, ka_ref, va_ref,
                   *, v_transposed):
    latb = lat_ref[...].astype(BF16)
    krs = krs_ref[...].astype(F32)
    segk = segk_ref[...]
    for p in range(A_HEADS // 2):
        c0 = 2 * SLAB * p
        kn = _dot(latb, wuk_ref[:, c0:c0 + 2 * SLAB])
        kn = _seg_rms(kn, segk, gk_ref[:, c0:c0 + 2 * SLAB])
        for s in range(2):
            ka_ref[:, c0 + s * SLAB:c0 + (s + 1) * SLAB] = (
                kn[:, s * SLAB:(s + 1) * SLAB] + krs).astype(BF16)
    if v_transposed:
        vt = _dot_nt(wuv_ref[...], latb).astype(BF16)
        for hp in range(A_HEADS // 2):
            va_ref[0, hp, 0] = vt[hp * SLAB:(hp + 1) * SLAB, :]
    else:
        va_ref[...] = _dot(latb, wuv_ref[...]).astype(BF16)


def _expand(lat2d, krs2d, wuk_p, wuv, gk, segk, *, tm, rows_per_batch, v_transposed):
    rows = lat2d.shape[0]
    row = lambda i: (i, 0)
    fixed = lambda i: (0, 0)
    if v_transposed:
        tpb = rows_per_batch // tm
        v_shape = jax.ShapeDtypeStruct((rows // rows_per_batch, A_HEADS // 2, tpb, SLAB, tm), BF16)
        v_spec = pl.BlockSpec((1, A_HEADS // 2, 1, SLAB, tm), lambda i: (i // tpb, 0, i % tpb, 0, 0))
    else:
        v_shape = jax.ShapeDtypeStruct((rows, A_WIDTH), BF16)
        v_spec = pl.BlockSpec((tm, A_WIDTH), row)
    return pl.pallas_call(
        functools.partial(_expand_kernel, v_transposed=v_transposed),
        out_shape=[jax.ShapeDtypeStruct((rows, A_HEADS * SLAB), BF16), v_shape],
        grid=(rows // tm,),
        in_specs=[pl.BlockSpec((tm, A_KV_RANK), row), pl.BlockSpec((tm, SLAB), row),
                  pl.BlockSpec(wuk_p.shape, fixed), pl.BlockSpec(wuv.shape, fixed),
                  pl.BlockSpec(gk.shape, fixed), pl.BlockSpec(segk.shape, fixed)],
        out_specs=[pl.BlockSpec((tm, A_HEADS * SLAB), row), v_spec],
        compiler_params=pltpu.CompilerParams(dimension_semantics=("arbitrary",),
                                             vmem_limit_bytes=VMEM_LIMIT),
        name="expand",
    )(lat2d, krs2d, wuk_p, wuv, gk, segk)


def _select_heads(o_even, o_odd):
    lane = lax.broadcasted_iota(jnp.int32, o_even.shape, 1)
    return jnp.where(lane < A_VDIM, o_even, o_odd)


def _mla_prompt_kernel(q_ref, k_ref, vt_ref, o_ref, *, tq, nsub):
    qi = pl.program_id(2)
    tk = vt_ref.shape[-1]
    ratio = tq // tk
    n_full = nsub * ratio * qi
    chunk_delta = (lax.broadcasted_iota(jnp.int32, (tk, tq), 0) // CHUNK
                   - lax.broadcasted_iota(jnp.int32, (tk, tq), 1) // CHUNK)
    ones = jnp.ones((16, tk), BF16)
    chains = [(e, h) for h in range(nsub) for e in range(2)]

    def step(j, carry, work, lagged):
        k0 = pl.multiple_of(j * tk, tk)
        scores = {}
        new = list(carry)

        def qk(c):
            e, h = chains[c]
            k = k_ref[0, pl.ds(k0, tk), e * SLAB:(e + 1) * SLAB]
            q = q_ref[0, h * tq:(h + 1) * tq, e * SLAB:(e + 1) * SLAB]
            scores[c] = _dot_nt(k, q)

        def update(c, max_delta):
            e, h = chains[c]
            m, acc, drift = carry[c]
            s = scores.pop(c)
            if max_delta is not None:
                s = jnp.where(chunk_delta <= max_delta, s, NEG_INF)
            vt1 = jnp.concatenate([vt_ref[0, 0, j, e * A_VDIM:(e + 1) * A_VDIM, :], ones], axis=0)
            bmax = jnp.max(s, axis=0, keepdims=True)
            m_new = jnp.maximum(m, bmax)
            if lagged:
                p = jnp.exp2(s - m).astype(BF16)
                new[c] = (m_new, jnp.exp2(m - m_new) * (acc + _dot(vt1, p)),
                          jnp.maximum(drift, bmax - m))
            else:
                p = jnp.exp2(s - m_new).astype(BF16)
                new[c] = (m_new, jnp.exp2(m - m_new) * acc + _dot(vt1, p), drift)

        for c, _ in work:
            qk(c)
        for c, max_delta in work:
            update(c, max_delta)
        return tuple(new)

    everyone = [(c, None) for c in range(len(chains))]
    diagonal = []
    for d in range(nsub * ratio):
        work = []
        for c, (e, h) in enumerate(chains):
            if d * tk >= (h + 1) * tq:
                continue
            before = (d + 1) * tk <= h * tq
            work.append((c, None if before else (h * tq - d * tk) // CHUNK))
        diagonal.append(work)

    def attend(lagged):
        carry = tuple((jnp.full((1, tq), NEG_INF, F32), jnp.zeros((A_VDIM + 16, tq), F32),
                       jnp.full((1, tq), NEG_INF, F32)) for _ in chains)
        first = jnp.minimum(n_full, 1) if lagged else 0
        carry = lax.fori_loop(0, first, lambda j, cr: step(j, cr, everyone, False), carry)
        carry = lax.fori_loop(first, n_full, lambda j, cr: step(j, cr, everyone, lagged), carry)
        for d, work in enumerate(diagonal):
            carry = step(n_full + d, carry, work, lagged and d >= 1)
        for h in range(nsub):
            outs = [carry[2 * h + e][1] for e in range(2)]
            outs = [acc[:A_VDIM] / acc[A_VDIM:A_VDIM + 1] for acc in outs]
            o_ref[0, h * tq:(h + 1) * tq, :] = jnp.concatenate(outs, axis=0).T.astype(o_ref.dtype)
        return functools.reduce(jnp.maximum, [jnp.max(cr[2]) for cr in carry])

    worst_drift = attend(True)

    @pl.when(worst_drift > MAX_EXP2_DRIFT)
    def _():
        attend(False)


def _mla_prompt(qa, ka, vt, *, tq, nsub):
    b, s, _ = qa.shape
    tk = vt.shape[-1]
    assert vt.shape == (b, A_HEADS // 2, s // tk, SLAB, tk) and tq % tk == 0
    tqq = nsub * tq
    return pl.pallas_call(
        functools.partial(_mla_prompt_kernel, tq=tq, nsub=nsub),
        out_shape=jax.ShapeDtypeStruct((b, s, A_WIDTH), BF16),
        grid=(b, A_HEADS // 2, s // tqq),
        in_specs=[pl.BlockSpec((1, tqq, 2 * SLAB), lambda bi, hp, qi: (bi, qi, hp)),
                  pl.BlockSpec((1, s, 2 * SLAB), lambda bi, hp, qi: (bi, 0, hp)),
                  pl.BlockSpec((1, 1, s // tk, SLAB, tk), lambda bi, hp, qi: (bi, hp, 0, 0, 0))],
        out_specs=pl.BlockSpec((1, tqq, SLAB), lambda bi, hp, qi: (bi, qi, hp)),
        compiler_params=pltpu.CompilerParams(
            dimension_semantics=("arbitrary", "arbitrary", "arbitrary"),
            vmem_limit_bytes=VMEM_LIMIT),
        name="mla_prompt",
    )(qa, ka, vt)


def _mla_sample_kernel(q_ref, k_ref, v_ref, o_ref):
    v = v_ref[0]
    outs = []
    for e in range(2):
        s = _dot_nt(q_ref[0, :, e * SLAB:(e + 1) * SLAB], k_ref[0, :, e * SLAB:(e + 1) * SLAB])
        p = jnp.exp2(s - jnp.max(s, axis=-1, keepdims=True))
        l = jnp.sum(p, axis=-1, keepdims=True)
        outs.append(_dot(p.astype(BF16), v) / l)
    o_ref[0] = _select_heads(outs[0], outs[1]).astype(o_ref.dtype)


def _mla_sample(qa, ka, va):
    b, t, _ = qa.shape
    n = ka.shape[1]
    return pl.pallas_call(
        _mla_sample_kernel,
        out_shape=jax.ShapeDtypeStruct((b, t, A_WIDTH), BF16),
        grid=(b, A_HEADS // 2),
        in_specs=[pl.BlockSpec((1, t, 2 * SLAB), lambda bi, hp: (bi, 0, hp)),
                  pl.BlockSpec((1, n, 2 * SLAB), lambda bi, hp: (bi, 0, hp)),
                  pl.BlockSpec((1, n, SLAB), lambda bi, hp: (bi, 0, hp))],
        out_specs=pl.BlockSpec((1, t, SLAB), lambda bi, hp: (bi, 0, hp)),
        compiler_params=pltpu.CompilerParams(dimension_semantics=("arbitrary", "arbitrary"),
                                             vmem_limit_bytes=VMEM_LIMIT),
        name="mla_sample",
    )(qa, ka, va)


BAND_TQ = 256
BAND_KEYS = B_WINDOW + BAND_TQ
TOEPLITZ_W = 1024


def _band_table_kernel(rb_ref, onehot_ref, allowed_ref, tb_ref, tp_ref):
    g = jnp.dot(rb_ref[...], onehot_ref[...], preferred_element_type=F32,
                precision=lax.Precision.HIGHEST) * LOG2E
    allowed = allowed_ref[...] > 0.0
    for h in range(B_HEADS):
        row = jnp.broadcast_to(g[h:h + 1, :], (BAND_TQ, TOEPLITZ_W))
        t = pltpu.roll(row, 0, axis=1, stride=1, stride_axis=0)[:, :BAND_KEYS]
        tb_ref[h] = t
        tp_ref[h] = jnp.where(allowed, t, NEG_INF)


def _band_tables(rel_bias):
    n_rel = rel_bias.shape[1]
    n_pad = -(-n_rel // LANES) * LANES
    x = np.arange(TOEPLITZ_W)
    key_minus_query = np.where(x < BAND_KEYS, x, x - TOEPLITZ_W)
    dist = B_WINDOW - key_minus_query
    idx = np.clip(dist, -B_MAX_REL, B_MAX_REL) + B_MAX_REL
    onehot = np.zeros((n_pad, TOEPLITZ_W), np.float32)
    onehot[idx, x] = 1.0
    q_chunk = np.arange(BAND_TQ) // CHUNK + B_LEFT_CHUNKS
    k_chunk = np.arange(BAND_KEYS) // CHUNK
    allowed = (k_chunk[None, :] <= q_chunk[:, None]) & (k_chunk[None, :] >= q_chunk[:, None] - B_LEFT_CHUNKS)
    rb = jnp.pad(rel_bias, ((0, 0), (0, n_pad - n_rel)))
    shape = jax.ShapeDtypeStruct((B_HEADS, BAND_TQ, BAND_KEYS), F32)
    return pl.pallas_call(
        _band_table_kernel,
        out_shape=[shape, shape],
        compiler_params=pltpu.CompilerParams(vmem_limit_bytes=VMEM_LIMIT),
        name="band_table",
    )(rb, jnp.asarray(onehot), jnp.asarray(allowed.astype(np.float32)))


def _head_lane_masks(width):
    lane = lax.broadcasted_iota(jnp.int32, (1, width), 1)
    return [(lane < B_HDIM), (lane >= B_HDIM)]


def _band_prompt_kernel(q_ref, kp_ref, kc_ref, vp_ref, vc_ref, t_ref, o_ref, *, tq, nsub):
    qi = pl.program_id(2)
    nprev = B_WINDOW // tq
    nkb = nprev + 1

    def window(prev_ref, cur_ref, w):
        if w < nprev:
            return prev_ref[0, w * tq:(w + 1) * tq, :]
        return cur_ref[0, (w - nprev) * tq:(w - nprev + 1) * tq, :]

    masks = _head_lane_masks(SLAB)
    units = [(t, e) for t in range(nsub) for e in range(2)]
    scores = {}
    for t, e in units:
        q = q_ref[0, t * tq:(t + 1) * tq, :]
        qm = jnp.where(masks[e], q, jnp.zeros_like(q))
        ss = []
        for j in range(nkb):
            s = _dot_nt(qm, window(kp_ref, kc_ref, t + j)) + t_ref[e, :, j * tq:(j + 1) * tq]
            if t + j < nprev:
                s = jnp.where(qi > 0, s, NEG_INF)
            ss.append(s)
        scores[t, e] = ss
    outs = {}
    for t, e in units:
        ss = scores.pop((t, e))
        m = functools.reduce(jnp.maximum, [jnp.max(s, axis=-1, keepdims=True) for s in ss])
        ps = [jnp.exp2(s - m) for s in ss]
        l = sum(jnp.sum(p, axis=-1, keepdims=True) for p in ps)
        o = sum(_dot(p.astype(BF16), window(vp_ref, vc_ref, t + j)) for j, p in enumerate(ps))
        outs[t, e] = o / l
    for t in range(nsub):
        o_ref[0, t * tq:(t + 1) * tq, :] = _select_heads(outs[t, 0], outs[t, 1]).astype(o_ref.dtype)


def _band_prompt(qb, kb, vb, table, *, tq, nsub):
    b, s, _ = qb.shape
    tqq = nsub * tq
    per_step = tqq // B_WINDOW
    assert tqq % B_WINDOW == 0 and B_WINDOW % tq == 0
    cur = pl.BlockSpec((1, tqq, SLAB), lambda hp, bi, qi: (bi, qi, hp))
    prev = pl.BlockSpec((1, B_WINDOW, SLAB),
                        lambda hp, bi, qi: (bi, jnp.maximum(qi * per_step - 1, 0), hp))
    return pl.pallas_call(
        functools.partial(_band_prompt_kernel, tq=tq, nsub=nsub),
        out_shape=jax.ShapeDtypeStruct((b, s, B_WIDTH), BF16),
        grid=(B_HEADS // 2, b, s // tqq),
        in_specs=[cur, prev, cur, prev, cur,
                  pl.BlockSpec((2, tq, B_WINDOW + tq), lambda hp, bi, qi: (hp, 0, 0))],
        out_specs=cur,
        compiler_params=pltpu.CompilerParams(
            dimension_semantics=("arbitrary", "arbitrary", "arbitrary"),
            vmem_limit_bytes=VMEM_LIMIT),
        name="band_prompt",
    )(qb, kb, kb, vb, vb, table)


def _band_sample_kernel(q_ref, k_ref, v_ref, t_ref, o_ref):
    q = q_ref[0]
    k = k_ref[0]
    v = v_ref[0]
    masks = _head_lane_masks(SLAB)
    outs = []
    for e in range(2):
        qm = jnp.where(masks[e], q, jnp.zeros_like(q))
        s = _dot_nt(qm, k) + t_ref[e, :, :k.shape[0]]
        p = jnp.exp2(s - jnp.max(s, axis=-1, keepdims=True))
        l = jnp.sum(p, axis=-1, keepdims=True)
        outs.append(_dot(p.astype(BF16), v) / l)
    o_ref[0] = _select_heads(outs[0], outs[1]).astype(o_ref.dtype)


def _band_sample(qb, kb, vb, table):
    b, t, _ = qb.shape
    n = kb.shape[1]
    return pl.pallas_call(
        _band_sample_kernel,
        out_shape=jax.ShapeDtypeStruct((b, t, B_WIDTH), BF16),
        grid=(B_HEADS // 2, b),
        in_specs=[pl.BlockSpec((1, t, SLAB), lambda hp, bi: (bi, 0, hp)),
                  pl.BlockSpec((1, n, SLAB), lambda hp, bi: (bi, 0, hp)),
                  pl.BlockSpec((1, n, SLAB), lambda hp, bi: (bi, 0, hp)),
                  pl.BlockSpec((2, t, BAND_KEYS), lambda hp, bi: (hp, 0, 0))],
        out_specs=pl.BlockSpec((1, t, SLAB), lambda hp, bi: (bi, 0, hp)),
        compiler_params=pltpu.CompilerParams(dimension_semantics=("arbitrary", "arbitrary"),
                                             vmem_limit_bytes=VMEM_LIMIT),
        name="band_sample",
    )(qb, kb, vb, table)


def _back_kernel(x_ref, gate_ref, aa_ref, sga_ref, ab_ref, sgb_ref, sma_ref, smb_ref,
                 woa_ref, wob_ref, wout_ref, o_ref, *, per_row):
    gate = gate_ref[...] if per_row else gate_ref[0]
    ua = _dot(aa_ref[...] * sga_ref[...], woa_ref[...])
    ub = _dot(ab_ref[...] * sgb_ref[...], wob_ref[...])
    merged = sma_ref[...].astype(F32) * ua + smb_ref[...].astype(F32) * ub
    o_ref[...] = x_ref[...] + gate * _dot(merged.astype(BF16), wout_ref[...])


def _back(x2d, gate, aa, sga, ab, sgb, sma, smb, woa, wob, wout, *, tm, rows_per_batch, per_row):
    rows = x2d.shape[0]
    row = lambda i: (i, 0)
    fixed = lambda i: (0, 0)
    if per_row:
        gate_spec = pl.BlockSpec((tm, D_MODEL), row)
    else:
        tpb = rows_per_batch // tm
        gate_spec = pl.BlockSpec((1, 1, D_MODEL), lambda i: (i // tpb, 0, 0))
    return pl.pallas_call(
        functools.partial(_back_kernel, per_row=per_row),
        out_shape=jax.ShapeDtypeStruct((rows, D_MODEL), F32),
        grid=(rows // tm,),
        in_specs=[pl.BlockSpec((tm, D_MODEL), row), gate_spec,
                  pl.BlockSpec((tm, A_WIDTH), row), pl.BlockSpec((tm, A_WIDTH), row),
                  pl.BlockSpec((tm, B_WIDTH), row), pl.BlockSpec((tm, B_WIDTH), row),
                  pl.BlockSpec((tm, D_MODEL), row), pl.BlockSpec((tm, D_MODEL), row),
                  pl.BlockSpec(woa.shape, fixed), pl.BlockSpec(wob.shape, fixed),
                  pl.BlockSpec(wout.shape, fixed)],
        out_specs=pl.BlockSpec((tm, D_MODEL), row),
        compiler_params=pltpu.CompilerParams(dimension_semantics=("arbitrary",),
                                             vmem_limit_bytes=VMEM_LIMIT),
        name="back",
    )(x2d, gate, aa, sga, ab, sgb, sma, smb, woa, wob, wout)


def _seg_matrix(group_of_lane, sizes):
    lane = np.arange(2 * SLAB)
    slab = lane // SLAB
    grp = group_of_lane[lane % SLAB]
    same = (slab[:, None] == slab[None, :]) & (grp[:, None] == grp[None, :])
    return jnp.asarray(np.where(same, 1.0 / sizes[grp][None, :], 0.0), dtype=BF16)


def _rope_tables(pos):
    inv = ROPE_BASE ** (-jnp.arange(0, A_ROPE, 2, dtype=F32) / A_ROPE)
    ang = pos.astype(F32)[:, None] * inv[None, :]
    cos, sin = lax.optimization_barrier((jnp.cos(ang), jnp.sin(ang)))
    n = pos.shape[0]
    ones = jnp.ones((n, A_NOPE), F32)
    zeros = jnp.zeros((n, A_NOPE), F32)
    z16 = jnp.zeros((n, HALF_ROPE), F32)
    pad1 = jnp.ones((n, SLAB - A_QK), F32)
    pad0 = jnp.zeros((n, SLAB - A_QK), F32)
    cos_t = jnp.concatenate([ones, cos, cos, pad1], axis=1)
    sin_lo = jnp.concatenate([zeros, -sin, z16, pad0], axis=1)
    sin_hi = jnp.concatenate([zeros, z16, sin, pad0], axis=1)
    return cos_t, sin_lo, sin_hi


def kernel(x_prompt, x_sample, cache_mla_latent, cache_mla_krope, cache_band_k, cache_band_v,
           c_prompt, c_sample, g_norm, w_ada, b_ada, w_in, g_q_lat, w_uq, g_kv_lat, w_uk, w_uv,
           g_qn_a, g_qr_a, g_kn_a, g_kr_a, g_q_b, g_k_b, rel_bias_b, w_oa, w_ob, w_out):
    bp, s, _ = x_prompt.shape
    bs, t, _ = x_sample.shape
    past = cache_mla_latent.shape[2]
    win_s = cache_band_k.shape[2]
    win_p = min(B_WINDOW, s)
    depth = g_norm.shape[0]
    assert depth == 1

    seg_q = _seg_matrix(np.where(np.arange(SLAB) < A_NOPE, 0, np.where(np.arange(SLAB) < A_QK, 1, 2)),
                        np.array([A_NOPE, A_ROPE, SLAB - A_QK], np.float64))
    seg_64 = _seg_matrix(np.arange(SLAB) // B_HDIM, np.array([B_HDIM, B_HDIM], np.float64))
    seg_k = _seg_matrix(np.where(np.arange(SLAB) < A_NOPE, 0, 1),
                        np.array([A_NOPE, SLAB - A_NOPE], np.float64))

    assert win_s == B_WINDOW and t <= BAND_TQ
    pos_s = past + np.arange(t)

    l = 0
    zpad = jnp.zeros((D_MODEL, SLAB - A_QK), F32)
    z64 = jnp.zeros((D_MODEL, A_NOPE), F32)
    win = w_in[l]
    c_kr0 = A_Q_RANK + A_KV_RANK
    win_p_ = jnp.concatenate([win[:, :c_kr0], z64, win[:, c_kr0:c_kr0 + A_ROPE], zpad,
                              win[:, c_kr0 + A_ROPE:]], axis=1).astype(BF16)
    wuq_p = jnp.pad(w_uq[l], ((0, 0), (0, 0), (0, SLAB - A_QK))).reshape(A_Q_RANK, -1).astype(BF16)
    wuk_p = jnp.pad(w_uk[l], ((0, 0), (0, 0), (0, SLAB - A_NOPE))).reshape(A_KV_RANK, -1).astype(BF16)
    wuv_b = w_uv[l].reshape(A_KV_RANK, -1).astype(BF16)
    wuv_t = wuv_b.T
    qscale = A_SCALE * LOG2E
    gq = jnp.tile(jnp.concatenate([g_qn_a[l], g_qr_a[l], jnp.zeros((SLAB - A_QK,), F32)]) * qscale,
                  A_HEADS)[None]
    gk = jnp.tile(jnp.concatenate([g_kn_a[l], jnp.zeros((SLAB - A_NOPE,), F32)]), A_HEADS)[None]
    gkr = jnp.concatenate([jnp.zeros((A_NOPE,), F32), g_kr_a[l], jnp.zeros((SLAB - A_QK,), F32)])[None]
    gqb = jnp.tile(g_q_b[l] * (B_SCALE * LOG2E), B_HEADS)[None]
    gkb = jnp.tile(g_k_b[l], B_HEADS)[None]
    consts = (g_norm[l][None], win_p_, g_q_lat[l][None], wuq_p, g_kv_lat[l][None], gq, gkr,
              gqb, gkb, seg_q, seg_64)
    woa, wob, wout = w_oa[l].astype(BF16), w_ob[l].astype(BF16), w_out[l].astype(BF16)

    c_rows = jnp.concatenate([c_prompt, jnp.zeros((8 - bp, D_MODEL), F32),
                              jnp.repeat(c_sample, t, axis=0)], axis=0)
    mod = _adaln(c_rows, w_ada[l], b_ada[l])
    shift, scale, gate = mod[:, :D_MODEL], mod[:, D_MODEL:2 * D_MODEL], mod[:, 2 * D_MODEL:]

    tm = 512
    xp2d = x_prompt.reshape(bp * s, D_MODEL)
    (qa, lat, kr, krs, sga, qb, kb, vb, kbf, vbf, sgb, sma, smb) = _front(
        xp2d, scale[:bp, None], shift[:bp, None], _rope_tables(jnp.arange(s)), consts,
        tm=tm, rows_per_batch=s, per_row=False, tail_rows=win_p)
    tk_mla = 512
    ka, vt = _expand(lat, krs, wuk_p, wuv_t, gk, seg_k, tm=tk_mla, rows_per_batch=s, v_transposed=True)
    attn_a = _mla_prompt(qa.reshape(bp, s, -1), ka.reshape(bp, s, -1), vt, tq=512, nsub=4)
    table_plain, table_band = _band_tables(rel_bias_b[l])
    attn_b = _band_prompt(qb.reshape(bp, s, -1), kb.reshape(bp, s, -1), vb.reshape(bp, s, -1),
                          table_band, tq=BAND_TQ, nsub=4)
    y_prompt = _back(xp2d, gate[:bp, None], attn_a.reshape(bp * s, -1), sga,
                     attn_b.reshape(bp * s, -1), sgb, sma, smb, woa, wob, wout,
                     tm=1024, rows_per_batch=s, per_row=False).reshape(bp, s, D_MODEL)

    xs2d = x_sample.reshape(bs * t, D_MODEL)
    rows_s = bs * t
    pos_tab = _rope_tables(jnp.asarray(pos_s))
    pos_tab = tuple(jnp.tile(a, (bs, 1)) for a in pos_tab)
    (qa2, lat2, kr2, krs2, sga2, qb2, kb2, vb2, kbf2, vbf2, sgb2, sma2, smb2) = _front(
        xs2d, scale[8:], shift[8:], pos_tab, consts, tm=rows_s, rows_per_batch=t, per_row=True,
        tail_rows=t)
    lat_all = jnp.concatenate([cache_mla_latent[l], lat2.reshape(bs, t, -1)], axis=1)
    krs_cache = jnp.pad(cache_mla_krope[l], ((0, 0), (0, 0), (A_NOPE, SLAB - A_QK))).astype(BF16)
    krs_all = jnp.concatenate([krs_cache, krs2.reshape(bs, t, -1)], axis=1)
    n_all = past + t
    ka2, va2 = _expand(lat_all.reshape(bs * n_all, -1), krs_all.reshape(bs * n_all, -1),
                       wuk_p, wuv_b, gk, seg_k, tm=n_all // 2, rows_per_batch=n_all, v_transposed=False)
    attn_a2 = _mla_sample(qa2.reshape(bs, t, -1), ka2.reshape(bs, n_all, -1), va2.reshape(bs, n_all, -1))
    kb_all = jnp.concatenate([cache_band_k[l].reshape(bs, win_s, -1).astype(BF16),
                              kb2.reshape(bs, t, -1)], axis=1)
    vb_all = jnp.concatenate([cache_band_v[l].reshape(bs, win_s, -1).astype(BF16),
                              vb2.reshape(bs, t, -1)], axis=1)
    attn_b2 = _band_sample(qb2.reshape(bs, t, -1), kb_all, vb_all, table_plain)
    y_sample = _back(xs2d, gate[8:], attn_a2.reshape(rows_s, -1), sga2, attn_b2.reshape(rows_s, -1),
                     sgb2, sma2, smb2, woa, wob, wout,
                     tm=rows_s, rows_per_batch=t, per_row=True).reshape(bs, t, D_MODEL)

    return (y_prompt, y_sample,
            lat.reshape(1, bp, s, A_KV_RANK), kr.reshape(1, bp, s, A_ROPE),
            kbf.reshape(1, bp, win_p, B_HEADS, B_HDIM), vbf.reshape(1, bp, win_p, B_HEADS, B_HDIM),
            lat2.reshape(1, bs, t, A_KV_RANK), kr2.reshape(1, bs, t, A_ROPE),
            kbf2.reshape(1, bs, t, B_HEADS, B_HDIM), vbf2.reshape(1, bs, t, B_HEADS, B_HDIM))
```

```python
import functools
import math

import jax
import jax.numpy as jnp
import numpy as np
from jax import lax
from jax.experimental import pallas as pl
from jax.experimental.pallas import tpu as pltpu

F32 = jnp.float32
BF16 = jnp.bfloat16

D_MODEL = 1024
CHUNK = 64
A_HEADS = 8
A_NOPE = 64
A_ROPE = 32
A_VDIM = 64
A_QK = A_NOPE + A_ROPE
A_Q_RANK = 384
A_KV_RANK = 256
A_WIDTH = A_HEADS * A_VDIM
A_SCALE = A_QK ** -0.5
B_HEADS = 8
B_HDIM = 64
B_WIDTH = B_HEADS * B_HDIM
B_LEFT_CHUNKS = 8
B_WINDOW = B_LEFT_CHUNKS * CHUNK
B_MAX_REL = 256
B_SCALE = B_HDIM ** -0.5
ROPE_BASE = 10000.0
NORM_EPS = 1e-6
NEG_INF = -1e30
LOG2E = math.log2(math.e)
MAX_EXP2_DRIFT = 60.0

LANES = 128
SLAB = LANES
HALF_ROPE = A_ROPE // 2
ROPE_LANE0 = A_NOPE
VMEM_LIMIT = 56 * 1024 * 1024

C_QLAT = 0
C_KVLAT = C_QLAT + A_Q_RANK
C_KR = C_KVLAT + A_KV_RANK
C_GA = C_KR + SLAB
C_QB = C_GA + A_WIDTH
C_KB = C_QB + B_WIDTH
C_VB = C_KB + B_WIDTH
C_GB = C_VB + B_WIDTH
C_MA = C_GB + B_WIDTH
C_MB = C_MA + D_MODEL
C_END = C_MB + D_MODEL

NT = (((1,), (1,)), ((), ()))


def _dot(a, b):
    return jnp.dot(a, b, preferred_element_type=F32)


def _dot_nt(a, b):
    return lax.dot_general(a, b, NT, preferred_element_type=F32)


def _rms_full(x, g):
    ms = jnp.mean(x * x, axis=-1, keepdims=True)
    return x * lax.rsqrt(ms + NORM_EPS) * g


def _seg_rms(x, seg, g):
    ms = _dot((x * x).astype(BF16), seg)
    return x * lax.rsqrt(ms + NORM_EPS) * g


def _rope_slab(x, cos_t, sin_lo, sin_hi):
    return (x * cos_t
            + pltpu.roll(x, SLAB - HALF_ROPE, axis=1) * sin_lo
            + pltpu.roll(x, HALF_ROPE, axis=1) * sin_hi)


def _adaln_kernel(c_ref, w_ref, b_ref, o_ref):
    c = c_ref[...]
    sc = c * jax.nn.sigmoid(c)
    o_ref[...] = jnp.dot(sc, w_ref[...], preferred_element_type=F32,
                         precision=lax.Precision.HIGHEST) + b_ref[...]


def _adaln(c_rows, w_ada, b_ada):
    n = c_rows.shape[0]
    tn = 1024
    return pl.pallas_call(
        _adaln_kernel,
        out_shape=jax.ShapeDtypeStruct((n, 3 * D_MODEL), F32),
        grid=(3 * D_MODEL // tn,),
        in_specs=[pl.BlockSpec((n, D_MODEL), lambda j: (0, 0)),
                  pl.BlockSpec((D_MODEL, tn), lambda j: (0, j)),
                  pl.BlockSpec((1, tn), lambda j: (0, j))],
        out_specs=pl.BlockSpec((n, tn), lambda j: (0, j)),
        compiler_params=pltpu.CompilerParams(dimension_semantics=("arbitrary",),
                                             vmem_limit_bytes=VMEM_LIMIT),
        name="adaln",
    )(c_rows, w_ada, b_ada.reshape(1, -1))


def _front_kernel(x_ref, scale_ref, shift_ref, cos_ref, slo_ref, shi_ref,
                  gnorm_ref, win_ref, gql_ref, wuq_ref, gkv_ref, gq_ref, gkr_ref,
                  gqb_ref, gkb_ref, segq_ref, seg64_ref,
                  qa_ref, lat_ref, kr_ref, krs_ref, sga_ref, qb_ref, kb_ref, vb_ref,
                  kbf_ref, vbf_ref, sgb_ref, sma_ref, smb_ref, *, per_row):
    x = x_ref[...]
    if per_row:
        scale, shift = scale_ref[...], shift_ref[...]
    else:
        scale, shift = scale_ref[0], shift_ref[0]
    h = _rms_full(x, gnorm_ref[...]) * (1.0 + scale) + shift
    hb = h.astype(BF16)
    cos_t, sin_lo, sin_hi = cos_ref[...], slo_ref[...], shi_ref[...]

    def proj(c0, c1):
        return _dot(hb, win_ref[:, c0:c1])

    segq, seg64 = segq_ref[...], seg64_ref[...]
    pair = 2 * SLAB

    def finish_qa(p, qa, ms):
        qa = qa * lax.rsqrt(ms + NORM_EPS) * gq_ref[:, p * pair:(p + 1) * pair]
        for s in range(2):
            slab = _rope_slab(qa[:, s * SLAB:(s + 1) * SLAB], cos_t, sin_lo, sin_hi)
            qa_ref[:, p * pair + s * SLAB:p * pair + (s + 1) * SLAB] = slab.astype(BF16)

    def finish_qb(p, qb, ms):
        qb = qb * lax.rsqrt(ms + NORM_EPS) * gqb_ref[:, p * pair:(p + 1) * pair]
        qb_ref[:, p * pair:(p + 1) * pair] = qb.astype(BF16)

    def finish_kb(p, kb, ms):
        kb = kb * lax.rsqrt(ms + NORM_EPS) * gkb_ref[:, p * pair:(p + 1) * pair]
        kbf_ref[:, p * pair:(p + 1) * pair] = kb
        kb_ref[:, p * pair:(p + 1) * pair] = kb.astype(BF16)

    def seg_ms(x, seg):
        return _dot((x * x).astype(BF16), seg)

    low_rank = proj(C_QLAT, C_GA)
    ga = proj(C_GA, C_QB)
    sga_ref[...] = (ga * jax.nn.sigmoid(ga)).astype(BF16)

    r = _rms_full(low_rank[:, C_QLAT:C_KVLAT], gql_ref[...]).astype(BF16)
    qa = [None] * (A_HEADS // 2)
    qa[0] = _dot(r, wuq_ref[:, 0:pair])
    qb0 = proj(C_QB, C_QB + pair)
    qa[1] = _dot(r, wuq_ref[:, pair:2 * pair])
    finish_qa(0, qa[0], seg_ms(qa[0], segq))
    kb0 = proj(C_KB, C_KB + pair)
    qa[2] = _dot(r, wuq_ref[:, 2 * pair:3 * pair])
    finish_qa(1, qa[1], seg_ms(qa[1], segq))
    gb = proj(C_GB, C_MA)
    sgb_ref[...] = (gb * jax.nn.sigmoid(gb)).astype(BF16)
    qa[3] = _dot(r, wuq_ref[:, 3 * pair:4 * pair])
    finish_qa(2, qa[2], seg_ms(qa[2], segq))
    finish_qb(0, qb0, seg_ms(qb0, seg64))
    sma_ref[...] = jax.nn.sigmoid(proj(C_MA, C_MB)).astype(BF16)
    finish_qa(3, qa[3], seg_ms(qa[3], segq))
    finish_kb(0, kb0, seg_ms(kb0, seg64))
    qb1 = proj(C_QB + pair, C_QB + 2 * pair)
    kb1 = proj(C_KB + pair, C_KB + 2 * pair)
    smb_ref[...] = jax.nn.sigmoid(proj(C_MB, C_END)).astype(BF16)
    finish_qb(1, qb1, seg_ms(qb1, seg64))
    finish_kb(1, kb1, seg_ms(kb1, seg64))
    vb = proj(C_VB, C_GB)
    vbf_ref[...] = vb
    vb_ref[...] = vb.astype(BF16)

    lat_ref[...] = _rms_full(low_rank[:, C_KVLAT:C_KR], gkv_ref[...])
    krs = low_rank[:, C_KR:C_GA]
    ms = jnp.sum(krs * krs, axis=-1, keepdims=True) * (1.0 / A_ROPE)
    krs = _rope_slab(krs * lax.rsqrt(ms + NORM_EPS) * gkr_ref[...], cos_t, sin_lo, sin_hi)
    krs_ref[...] = krs.astype(BF16)
    kr_ref[...] = pltpu.roll(krs, SLAB - ROPE_LANE0, axis=1)[:, :A_ROPE]


def _front(x2d, scale, shift, rope_tabs, consts, *, tm, rows_per_batch, per_row, tail_rows):
    rows = x2d.shape[0]
    nt = rows // tm
    row = lambda i: (i, 0)
    fixed = lambda i: (0, 0)
    tpb = rows_per_batch // tm
    if per_row:
        mod_spec = pl.BlockSpec((tm, D_MODEL), row)
        tab_spec = pl.BlockSpec((tm, SLAB), row)
    else:
        mod_spec = pl.BlockSpec((1, 1, D_MODEL), lambda i: (i // tpb, 0, 0))
        tab_spec = pl.BlockSpec((tm, SLAB), lambda i: (i % tpb, 0))

    def full(a):
        return pl.BlockSpec(a.shape, fixed)

    def out(width, dtype):
        return jax.ShapeDtypeStruct((rows, width), dtype), pl.BlockSpec((tm, width), row)

    if tail_rows == rows_per_batch:
        tail = out(B_WIDTH, F32)
    else:
        ntail = tail_rows // tm
        tail = (jax.ShapeDtypeStruct((rows // rows_per_batch * tail_rows, B_WIDTH), F32),
                pl.BlockSpec((tm, B_WIDTH),
                             lambda i: (i // tpb * ntail + jnp.maximum(i % tpb - (tpb - ntail), 0), 0)))

    outs = [out(A_HEADS * SLAB, BF16),
            out(A_KV_RANK, F32),
            out(A_ROPE, F32),
            out(SLAB, BF16),
            out(A_WIDTH, BF16),
            out(B_WIDTH, BF16),
            out(B_WIDTH, BF16),
            out(B_WIDTH, BF16),
            tail,
            tail,
            out(B_WIDTH, BF16),
            out(D_MODEL, BF16),
            out(D_MODEL, BF16)]
    return pl.pallas_call(
        functools.partial(_front_kernel, per_row=per_row),
        out_shape=[o[0] for o in outs],
        grid=(nt,),
        in_specs=[pl.BlockSpec((tm, D_MODEL), row), mod_spec, mod_spec,
                  tab_spec, tab_spec, tab_spec] + [full(a) for a in consts],
        out_specs=[o[1] for o in outs],
        compiler_params=pltpu.CompilerParams(dimension_semantics=("arbitrary",),
                                             vmem_limit_bytes=VMEM_LIMIT),
        name="front",
    )(x2d, scale, shift, *rope_tabs, *consts)


def _expand_kernel(lat_ref, krs_ref, wuk_ref, wuv_ref, gk_ref, segk_ref, ka_ref, va_ref,
                   *, v_transposed):
    latb = lat_ref[...].astype(BF16)
    krs = krs_ref[...].astype(F32)
    segk = segk_ref[...]
    for p in range(A_HEADS // 2):
        c0 = 2 * SLAB * p
        kn = _dot(latb, wuk_ref[:, c0:c0 + 2 * SLAB])
        kn = _seg_rms(kn, segk, gk_ref[:, c0:c0 + 2 * SLAB])
        for s in range(2):
            ka_ref[:, c0 + s * SLAB:c0 + (s + 1) * SLAB] = (
                kn[:, s * SLAB:(s + 1) * SLAB] + krs).astype(BF16)
    if v_transposed:
        vt = _dot_nt(wuv_ref[...], latb).astype(BF16)
        for hp in range(A_HEADS // 2):
            va_ref[0, hp, 0] = vt[hp * SLAB:(hp + 1) * SLAB, :]
    else:
        va_ref[...] = _dot(latb, wuv_ref[...]).astype(BF16)


def _expand(lat2d, krs2d, wuk_p, wuv, gk, segk, *, tm, rows_per_batch, v_transposed):
    rows = lat2d.shape[0]
    row = lambda i: (i, 0)
    fixed = lambda i: (0, 0)
    if v_transposed:
        tpb = rows_per_batch // tm
        v_shape = jax.ShapeDtypeStruct((rows // rows_per_batch, A_HEADS // 2, tpb, SLAB, tm), BF16)
        v_spec = pl.BlockSpec((1, A_HEADS // 2, 1, SLAB, tm), lambda i: (i // tpb, 0, i % tpb, 0, 0))
    else:
        v_shape = jax.ShapeDtypeStruct((rows, A_WIDTH), BF16)
        v_spec = pl.BlockSpec((tm, A_WIDTH), row)
    return pl.pallas_call(
        functools.partial(_expand_kernel, v_transposed=v_transposed),
        out_shape=[jax.ShapeDtypeStruct((rows, A_HEADS * SLAB), BF16), v_shape],
        grid=(rows // tm,),
        in_specs=[pl.BlockSpec((tm, A_KV_RANK), row), pl.BlockSpec((tm, SLAB), row),
                  pl.BlockSpec(wuk_p.shape, fixed), pl.BlockSpec(wuv.shape, fixed),
                  pl.BlockSpec(gk.shape, fixed), pl.BlockSpec(segk.shape, fixed)],
        out_specs=[pl.BlockSpec((tm, A_HEADS * SLAB), row), v_spec],
        compiler_params=pltpu.CompilerParams(dimension_semantics=("arbitrary",),
                                             vmem_limit_bytes=VMEM_LIMIT),
        name="expand",
    )(lat2d, krs2d, wuk_p, wuv, gk, segk)


def _select_heads(o_even, o_odd):
    lane = lax.broadcasted_iota(jnp.int32, o_even.shape, 1)
    return jnp.where(lane < A_VDIM, o_even, o_odd)


def _mla_prompt_kernel(q_ref, k_ref, vt_ref, o_ref, *, tq, nsub):
    qi = pl.program_id(2)
    tk = vt_ref.shape[-1]
    ratio = tq // tk
    n_full = nsub * ratio * qi
    chunk_delta = (lax.broadcasted_iota(jnp.int32, (tk, tq), 0) // CHUNK
                   - lax.broadcasted_iota(jnp.int32, (tk, tq), 1) // CHUNK)
    ones = jnp.ones((16, tk), BF16)
    chains = [(e, h) for h in range(nsub) for e in range(2)]

    def step(j, carry, work, lagged):
        k0 = pl.multiple_of(j * tk, tk)
        scores = {}
        new = list(carry)

        def qk(c):
            e, h = chains[c]
            k = k_ref[0, pl.ds(k0, tk), e * SLAB:(e + 1) * SLAB]
            q = q_ref[0, h * tq:(h + 1) * tq, e * SLAB:(e + 1) * SLAB]
            scores[c] = _dot_nt(k, q)

        def update(c, max_delta):
            e, h = chains[c]
            m, acc, drift = carry[c]
            s = scores.pop(c)
            if max_delta is not None:
                s = jnp.where(chunk_delta <= max_delta, s, NEG_INF)
            vt1 = jnp.concatenate([vt_ref[0, 0, j, e * A_VDIM:(e + 1) * A_VDIM, :], ones], axis=0)
            bmax = jnp.max(s, axis=0, keepdims=True)
            m_new = jnp.maximum(m, bmax)
            if lagged:
                p = jnp.exp2(s - m).astype(BF16)
                new[c] = (m_new, jnp.exp2(m - m_new) * (acc + _dot(vt1, p)),
                          jnp.maximum(drift, bmax - m))
            else:
                p = jnp.exp2(s - m_new).astype(BF16)
                new[c] = (m_new, jnp.exp2(m - m_new) * acc + _dot(vt1, p), drift)

        for c, _ in work:
            qk(c)
        for c, max_delta in work:
            update(c, max_delta)
        return tuple(new)

    everyone = [(c, None) for c in range(len(chains))]
    diagonal = []
    for d in range(nsub * ratio):
        work = []
        for c, (e, h) in enumerate(chains):
            if d * tk >= (h + 1) * tq:
                continue
            before = (d + 1) * tk <= h * tq
            work.append((c, None if before else (h * tq - d * tk) // CHUNK))
        diagonal.append(work)

    def attend(lagged):
        carry = tuple((jnp.full((1, tq), NEG_INF, F32), jnp.zeros((A_VDIM + 16, tq), F32),
                       jnp.full((1, tq), NEG_INF, F32)) for _ in chains)
        first = jnp.minimum(n_full, 1) if lagged else 0
        carry = lax.fori_loop(0, first, lambda j, cr: step(j, cr, everyone, False), carry)
        carry = lax.fori_loop(first, n_full, lambda j, cr: step(j, cr, everyone, lagged), carry)
        for d, work in enumerate(diagonal):
            carry = step(n_full + d, carry, work, lagged and d >= 1)
        for h in range(nsub):
            outs = [carry[2 * h + e][1] for e in range(2)]
            outs = [acc[:A_VDIM] / acc[A_VDIM:A_VDIM + 1] for acc in outs]
            o_ref[0, h * tq:(h + 1) * tq, :] = jnp.concatenate(outs, axis=0).T.astype(o_ref.dtype)
        return functools.reduce(jnp.maximum, [jnp.max(cr[2]) for cr in carry])

    worst_drift = attend(True)

    @pl.when(worst_drift > MAX_EXP2_DRIFT)
    def _():
        attend(False)


def _mla_prompt(qa, ka, vt, *, tq, nsub):
    b, s, _ = qa.shape
    tk = vt.shape[-1]
    assert vt.shape == (b, A_HEADS // 2, s // tk, SLAB, tk) and tq % tk == 0
    tqq = nsub * tq
    return pl.pallas_call(
        functools.partial(_mla_prompt_kernel, tq=tq, nsub=nsub),
        out_shape=jax.ShapeDtypeStruct((b, s, A_WIDTH), BF16),
        grid=(b, A_HEADS // 2, s // tqq),
        in_specs=[pl.BlockSpec((1, tqq, 2 * SLAB), lambda bi, hp, qi: (bi, qi, hp)),
                  pl.BlockSpec((1, s, 2 * SLAB), lambda bi, hp, qi: (bi, 0, hp)),
                  pl.BlockSpec((1, 1, s // tk, SLAB, tk), lambda bi, hp, qi: (bi, hp, 0, 0, 0))],
        out_specs=pl.BlockSpec((1, tqq, SLAB), lambda bi, hp, qi: (bi, qi, hp)),
        compiler_params=pltpu.CompilerParams(
            dimension_semantics=("arbitrary", "arbitrary", "arbitrary"),
            vmem_limit_bytes=VMEM_LIMIT),
        name="mla_prompt",
    )(qa, ka, vt)


def _softmax_over_parts(scores, values):
    m = functools.reduce(jnp.maximum, [jnp.max(s, axis=-1, keepdims=True) for s in scores])
    ps = [jnp.exp2(s - m) for s in scores]
    l = sum(jnp.sum(p, axis=-1, keepdims=True) for p in ps)
    return sum(_dot(p.astype(BF16), v) for p, v in zip(ps, values)) / l


def _mla_sample_kernel(q_ref, kc_ref, vc_ref, kn_ref, vn_ref, o_ref):
    values = [vc_ref[0], vn_ref[0]]
    outs = []
    for e in range(2):
        q = q_ref[0, :, e * SLAB:(e + 1) * SLAB]
        scores = [_dot_nt(q, k_ref[0, :, e * SLAB:(e + 1) * SLAB]) for k_ref in (kc_ref, kn_ref)]
        outs.append(_softmax_over_parts(scores, values))
    o_ref[0] = _select_heads(outs[0], outs[1]).astype(o_ref.dtype)


def _mla_sample(qa, ka_cache, va_cache, ka_new, va_new):
    b, t, _ = qa.shape
    n = ka_cache.shape[1]
    pair = lambda rows: pl.BlockSpec((1, rows, 2 * SLAB), lambda bi, hp: (bi, 0, hp))
    slab = lambda rows: pl.BlockSpec((1, rows, SLAB), lambda bi, hp: (bi, 0, hp))
    return pl.pallas_call(
        _mla_sample_kernel,
        out_shape=jax.ShapeDtypeStruct((b, t, A_WIDTH), BF16),
        grid=(b, A_HEADS // 2),
        in_specs=[pair(t), pair(n), slab(n), pair(t), slab(t)],
        out_specs=slab(t),
        compiler_params=pltpu.CompilerParams(dimension_semantics=("arbitrary", "arbitrary"),
                                             vmem_limit_bytes=VMEM_LIMIT),
        name="mla_sample",
    )(qa, ka_cache, va_cache, ka_new, va_new)


BAND_TQ = 256
BAND_KEYS = B_WINDOW + BAND_TQ
TOEPLITZ_W = 1024


def _band_table_kernel(rb_ref, onehot_ref, allowed_ref, tb_ref, tp_ref):
    g = jnp.dot(rb_ref[...], onehot_ref[...], preferred_element_type=F32,
                precision=lax.Precision.HIGHEST) * LOG2E
    allowed = allowed_ref[...] > 0.0
    for h in range(B_HEADS):
        row = jnp.broadcast_to(g[h:h + 1, :], (BAND_TQ, TOEPLITZ_W))
        t = pltpu.roll(row, 0, axis=1, stride=1, stride_axis=0)[:, :BAND_KEYS]
        tb_ref[h] = t
        tp_ref[h] = jnp.where(allowed, t, NEG_INF)


def _band_tables(rel_bias):
    n_rel = rel_bias.shape[1]
    n_pad = -(-n_rel // LANES) * LANES
    x = np.arange(TOEPLITZ_W)
    key_minus_query = np.where(x < BAND_KEYS, x, x - TOEPLITZ_W)
    dist = B_WINDOW - key_minus_query
    idx = np.clip(dist, -B_MAX_REL, B_MAX_REL) + B_MAX_REL
    onehot = np.zeros((n_pad, TOEPLITZ_W), np.float32)
    onehot[idx, x] = 1.0
    q_chunk = np.arange(BAND_TQ) // CHUNK + B_LEFT_CHUNKS
    k_chunk = np.arange(BAND_KEYS) // CHUNK
    allowed = (k_chunk[None, :] <= q_chunk[:, None]) & (k_chunk[None, :] >= q_chunk[:, None] - B_LEFT_CHUNKS)
    rb = jnp.pad(rel_bias, ((0, 0), (0, n_pad - n_rel)))
    shape = jax.ShapeDtypeStruct((B_HEADS, BAND_TQ, BAND_KEYS), F32)
    return pl.pallas_call(
        _band_table_kernel,
        out_shape=[shape, shape],
        compiler_params=pltpu.CompilerParams(vmem_limit_bytes=VMEM_LIMIT),
        name="band_table",
    )(rb, jnp.asarray(onehot), jnp.asarray(allowed.astype(np.float32)))


def _head_lane_masks(width):
    lane = lax.broadcasted_iota(jnp.int32, (1, width), 1)
    return [(lane < B_HDIM), (lane >= B_HDIM)]


def _band_prompt_kernel(q_ref, kp_ref, kc_ref, vp_ref, vc_ref, t_ref, o_ref, *, tq, nsub):
    qi = pl.program_id(2)
    nprev = B_WINDOW // tq
    nkb = nprev + 1

    def window(prev_ref, cur_ref, w):
        if w < nprev:
            return prev_ref[0, w * tq:(w + 1) * tq, :]
        return cur_ref[0, (w - nprev) * tq:(w - nprev + 1) * tq, :]

    masks = _head_lane_masks(SLAB)
    units = [(t, e) for t in range(nsub) for e in range(2)]
    scores = {}
    for t, e in units:
        q = q_ref[0, t * tq:(t + 1) * tq, :]
        qm = jnp.where(masks[e], q, jnp.zeros_like(q))
        ss = []
        for j in range(nkb):
            s = _dot_nt(qm, window(kp_ref, kc_ref, t + j)) + t_ref[e, :, j * tq:(j + 1) * tq]
            if t + j < nprev:
                s = jnp.where(qi > 0, s, NEG_INF)
            ss.append(s)
        scores[t, e] = ss
    outs = {}
    for t, e in units:
        ss = scores.pop((t, e))
        m = functools.reduce(jnp.maximum, [jnp.max(s, axis=-1, keepdims=True) for s in ss])
        ps = [jnp.exp2(s - m) for s in ss]
        l = sum(jnp.sum(p, axis=-1, keepdims=True) for p in ps)
        o = sum(_dot(p.astype(BF16), window(vp_ref, vc_ref, t + j)) for j, p in enumerate(ps))
        outs[t, e] = o / l
    for t in range(nsub):
        o_ref[0, t * tq:(t + 1) * tq, :] = _select_heads(outs[t, 0], outs[t, 1]).astype(o_ref.dtype)


def _band_prompt(qb, kb, vb, table, *, tq, nsub):
    b, s, _ = qb.shape
    tqq = nsub * tq
    per_step = tqq // B_WINDOW
    assert tqq % B_WINDOW == 0 and B_WINDOW % tq == 0
    cur = pl.BlockSpec((1, tqq, SLAB), lambda hp, bi, qi: (bi, qi, hp))
    prev = pl.BlockSpec((1, B_WINDOW, SLAB),
                        lambda hp, bi, qi: (bi, jnp.maximum(qi * per_step - 1, 0), hp))
    return pl.pallas_call(
        functools.partial(_band_prompt_kernel, tq=tq, nsub=nsub),
        out_shape=jax.ShapeDtypeStruct((b, s, B_WIDTH), BF16),
        grid=(B_HEADS // 2, b, s // tqq),
        in_specs=[cur, prev, cur, prev, cur,
                  pl.BlockSpec((2, tq, B_WINDOW + tq), lambda hp, bi, qi: (hp, 0, 0))],
        out_specs=cur,
        compiler_params=pltpu.CompilerParams(
            dimension_semantics=("arbitrary", "arbitrary", "arbitrary"),
            vmem_limit_bytes=VMEM_LIMIT),
        name="band_prompt",
    )(qb, kb, kb, vb, vb, table)


def _band_sample_kernel(q_ref, kc_ref, vc_ref, kn_ref, vn_ref, t_ref, o_ref):
    q = q_ref[0]
    keys = [kc_ref[0].astype(BF16), kn_ref[0]]
    values = [vc_ref[0].astype(BF16), vn_ref[0]]
    masks = _head_lane_masks(SLAB)
    outs = []
    for e in range(2):
        qm = jnp.where(masks[e], q, jnp.zeros_like(q))
        scores, col = [], 0
        for k in keys:
            scores.append(_dot_nt(qm, k) + t_ref[e, :, col:col + k.shape[0]])
            col += k.shape[0]
        outs.append(_softmax_over_parts(scores, values))
    o_ref[0] = _select_heads(outs[0], outs[1]).astype(o_ref.dtype)


def _band_sample(qb, kb_cache, vb_cache, kb_new, vb_new, table):
    b, t, _ = qb.shape
    n = kb_cache.shape[1]
    slab = lambda rows: pl.BlockSpec((1, rows, SLAB), lambda hp, bi: (bi, 0, hp))
    return pl.pallas_call(
        _band_sample_kernel,
        out_shape=jax.ShapeDtypeStruct((b, t, B_WIDTH), BF16),
        grid=(B_HEADS // 2, b),
        in_specs=[slab(t), slab(n), slab(n), slab(t), slab(t),
                  pl.BlockSpec((2, t, BAND_KEYS), lambda hp, bi: (hp, 0, 0))],
        out_specs=slab(t),
        compiler_params=pltpu.CompilerParams(dimension_semantics=("arbitrary", "arbitrary"),
                                             vmem_limit_bytes=VMEM_LIMIT),
        name="band_sample",
    )(qb, kb_cache, vb_cache, kb_new, vb_new, table)


def _back_kernel(x_ref, gate_ref, aa_ref, sga_ref, ab_ref, sgb_ref, sma_ref, smb_ref,
                 woa_ref, wob_ref, wout_ref, o_ref, *, per_row):
    gate = gate_ref[...] if per_row else gate_ref[0]
    ua = _dot(aa_ref[...] * sga_ref[...], woa_ref[...])
    ub = _dot(ab_ref[...] * sgb_ref[...], wob_ref[...])
    merged = sma_ref[...].astype(F32) * ua + smb_ref[...].astype(F32) * ub
    o_ref[...] = x_ref[...] + gate * _dot(merged.astype(BF16), wout_ref[...])


def _back(x2d, gate, aa, sga, ab, sgb, sma, smb, woa, wob, wout, *, tm, rows_per_batch, per_row):
    rows = x2d.shape[0]
    row = lambda i: (i, 0)
    fixed = lambda i: (0, 0)
    if per_row:
        gate_spec = pl.BlockSpec((tm, D_MODEL), row)
    else:
        tpb = rows_per_batch // tm
        gate_spec = pl.BlockSpec((1, 1, D_MODEL), lambda i: (i // tpb, 0, 0))
    return pl.pallas_call(
        functools.partial(_back_kernel, per_row=per_row),
        out_shape=jax.ShapeDtypeStruct((rows, D_MODEL), F32),
        grid=(rows // tm,),
        in_specs=[pl.BlockSpec((tm, D_MODEL), row), gate_spec,
                  pl.BlockSpec((tm, A_WIDTH), row), pl.BlockSpec((tm, A_WIDTH), row),
                  pl.BlockSpec((tm, B_WIDTH), row), pl.BlockSpec((tm, B_WIDTH), row),
                  pl.BlockSpec((tm, D_MODEL), row), pl.BlockSpec((tm, D_MODEL), row),
                  pl.BlockSpec(woa.shape, fixed), pl.BlockSpec(wob.shape, fixed),
                  pl.BlockSpec(wout.shape, fixed)],
        out_specs=pl.BlockSpec((tm, D_MODEL), row),
        compiler_params=pltpu.CompilerParams(dimension_semantics=("arbitrary",),
                                             vmem_limit_bytes=VMEM_LIMIT),
        name="back",
    )(x2d, gate, aa, sga, ab, sgb, sma, smb, woa, wob, wout)


def _seg_matrix(group_of_lane, sizes):
    lane = np.arange(2 * SLAB)
    slab = lane // SLAB
    grp = group_of_lane[lane % SLAB]
    same = (slab[:, None] == slab[None, :]) & (grp[:, None] == grp[None, :])
    return jnp.asarray(np.where(same, 1.0 / sizes[grp][None, :], 0.0), dtype=BF16)


def _rope_tables(pos):
    inv = ROPE_BASE ** (-jnp.arange(0, A_ROPE, 2, dtype=F32) / A_ROPE)
    n = pos.shape[0]
    ang = (pos.astype(F32)[:, None] * inv[None, :]).reshape(-1, LANES)
    cos, sin = lax.optimization_barrier((jnp.cos(ang), jnp.sin(ang)))
    cos, sin = cos.reshape(n, HALF_ROPE), sin.reshape(n, HALF_ROPE)
    ones = jnp.ones((n, A_NOPE), F32)
    zeros = jnp.zeros((n, A_NOPE), F32)
    z16 = jnp.zeros((n, HALF_ROPE), F32)
    pad1 = jnp.ones((n, SLAB - A_QK), F32)
    pad0 = jnp.zeros((n, SLAB - A_QK), F32)
    cos_t = jnp.concatenate([ones, cos, cos, pad1], axis=1)
    sin_lo = jnp.concatenate([zeros, -sin, z16, pad0], axis=1)
    sin_hi = jnp.concatenate([zeros, z16, sin, pad0], axis=1)
    return cos_t, sin_lo, sin_hi


def kernel(x_prompt, x_sample, cache_mla_latent, cache_mla_krope, cache_band_k, cache_band_v,
           c_prompt, c_sample, g_norm, w_ada, b_ada, w_in, g_q_lat, w_uq, g_kv_lat, w_uk, w_uv,
           g_qn_a, g_qr_a, g_kn_a, g_kr_a, g_q_b, g_k_b, rel_bias_b, w_oa, w_ob, w_out):
    bp, s, _ = x_prompt.shape
    bs, t, _ = x_sample.shape
    past = cache_mla_latent.shape[2]
    win_s = cache_band_k.shape[2]
    win_p = min(B_WINDOW, s)
    depth = g_norm.shape[0]
    assert depth == 1

    seg_q = _seg_matrix(np.where(np.arange(SLAB) < A_NOPE, 0, np.where(np.arange(SLAB) < A_QK, 1, 2)),
                        np.array([A_NOPE, A_ROPE, SLAB - A_QK], np.float64))
    seg_64 = _seg_matrix(np.arange(SLAB) // B_HDIM, np.array([B_HDIM, B_HDIM], np.float64))
    seg_k = _seg_matrix(np.where(np.arange(SLAB) < A_NOPE, 0, 1),
                        np.array([A_NOPE, SLAB - A_NOPE], np.float64))

    assert win_s == B_WINDOW and t <= BAND_TQ
    pos_s = past + np.arange(t)

    (g_norm, w_ada, b_ada, w_in, g_q_lat, w_uq, g_kv_lat, w_uk, w_uv, g_qn_a, g_qr_a, g_kn_a,
     g_kr_a, g_q_b, g_k_b, rel_bias_b, w_oa, w_ob, w_out) = [
        a.reshape(a.shape[1:]) for a in
        (g_norm, w_ada, b_ada, w_in, g_q_lat, w_uq, g_kv_lat, w_uk, w_uv, g_qn_a, g_qr_a, g_kn_a,
         g_kr_a, g_q_b, g_k_b, rel_bias_b, w_oa, w_ob, w_out)]
    zpad = jnp.zeros((D_MODEL, SLAB - A_QK), F32)
    z64 = jnp.zeros((D_MODEL, A_NOPE), F32)
    win = w_in
    c_kr0 = A_Q_RANK + A_KV_RANK
    win_p_ = jnp.concatenate([win[:, :c_kr0], z64, win[:, c_kr0:c_kr0 + A_ROPE], zpad,
                              win[:, c_kr0 + A_ROPE:]], axis=1).astype(BF16)
    wuq_p = jnp.pad(w_uq, ((0, 0), (0, 0), (0, SLAB - A_QK))).reshape(A_Q_RANK, -1).astype(BF16)
    wuk_p = jnp.pad(w_uk, ((0, 0), (0, 0), (0, SLAB - A_NOPE))).reshape(A_KV_RANK, -1).astype(BF16)
    wuv_b = w_uv.reshape(A_KV_RANK, -1).astype(BF16)
    wuv_t = wuv_b.T
    qscale = A_SCALE * LOG2E
    gq = jnp.tile(jnp.concatenate([g_qn_a, g_qr_a, jnp.zeros((SLAB - A_QK,), F32)]) * qscale,
                  A_HEADS)[None]
    gk = jnp.tile(jnp.concatenate([g_kn_a, jnp.zeros((SLAB - A_NOPE,), F32)]), A_HEADS)[None]
    gkr = jnp.concatenate([jnp.zeros((A_NOPE,), F32), g_kr_a, jnp.zeros((SLAB - A_QK,), F32)])[None]
    gqb = jnp.tile(g_q_b * (B_SCALE * LOG2E), B_HEADS)[None]
    gkb = jnp.tile(g_k_b, B_HEADS)[None]
    consts = (g_norm[None], win_p_, g_q_lat[None], wuq_p, g_kv_lat[None], gq, gkr,
              gqb, gkb, seg_q, seg_64)
    woa, wob, wout = w_oa.astype(BF16), w_ob.astype(BF16), w_out.astype(BF16)

    c_rows = jnp.concatenate([c_prompt, jnp.zeros((8 - bp, D_MODEL), F32),
                              jnp.repeat(c_sample, t, axis=0)], axis=0)
    mod = _adaln(c_rows, w_ada, b_ada)
    shift, scale, gate = mod[:, :D_MODEL], mod[:, D_MODEL:2 * D_MODEL], mod[:, 2 * D_MODEL:]

    tm = 512
    xp2d = x_prompt.reshape(bp * s, D_MODEL)
    (qa, lat, kr, krs, sga, qb, kb, vb, kbf, vbf, sgb, sma, smb) = _front(
        xp2d, scale[:bp, None], shift[:bp, None], _rope_tables(jnp.arange(s)), consts,
        tm=tm, rows_per_batch=s, per_row=False, tail_rows=win_p)
    tk_mla = 512
    ka, vt = _expand(lat, krs, wuk_p, wuv_t, gk, seg_k, tm=tk_mla, rows_per_batch=s, v_transposed=True)
    attn_a = _mla_prompt(qa.reshape(bp, s, -1), ka.reshape(bp, s, -1), vt, tq=512, nsub=4)
    table_plain, table_band = _band_tables(rel_bias_b)
    attn_b = _band_prompt(qb.reshape(bp, s, -1), kb.reshape(bp, s, -1), vb.reshape(bp, s, -1),
                          table_band, tq=BAND_TQ, nsub=4)
    y_prompt = _back(xp2d, gate[:bp, None], attn_a.reshape(bp * s, -1), sga,
                     attn_b.reshape(bp * s, -1), sgb, sma, smb, woa, wob, wout,
                     tm=1024, rows_per_batch=s, per_row=False).reshape(bp, s, D_MODEL)

    xs2d = x_sample.reshape(bs * t, D_MODEL)
    rows_s = bs * t
    pos_tab = _rope_tables(jnp.asarray(pos_s))
    pos_tab = tuple(jnp.tile(a, (bs, 1)) for a in pos_tab)
    (qa2, lat2, kr2, krs2, sga2, qb2, kb2, vb2, kbf2, vbf2, sgb2, sma2, smb2) = _front(
        xs2d, scale[8:], shift[8:], pos_tab, consts, tm=rows_s, rows_per_batch=t, per_row=True,
        tail_rows=t)
    krs_cache = jnp.pad(cache_mla_krope.reshape(bs * past, A_ROPE),
                        ((0, 0), (A_NOPE, SLAB - A_QK))).astype(BF16)
    ka_c, va_c = _expand(cache_mla_latent.reshape(bs * past, A_KV_RANK), krs_cache,
                         wuk_p, wuv_b, gk, seg_k, tm=past // 2, rows_per_batch=past, v_transposed=False)
    ka_n, va_n = _expand(lat2, krs2, wuk_p, wuv_b, gk, seg_k, tm=rows_s, rows_per_batch=t,
                         v_transposed=False)
    attn_a2 = _mla_sample(qa2.reshape(bs, t, -1), ka_c.reshape(bs, past, -1), va_c.reshape(bs, past, -1),
                          ka_n.reshape(bs, t, -1), va_n.reshape(bs, t, -1))
    attn_b2 = _band_sample(qb2.reshape(bs, t, -1), cache_band_k.reshape(bs, win_s, B_WIDTH),
                           cache_band_v.reshape(bs, win_s, B_WIDTH), kb2.reshape(bs, t, -1),
                           vb2.reshape(bs, t, -1), table_plain)
    y_sample = _back(xs2d, gate[8:], attn_a2.reshape(rows_s, -1), sga2, attn_b2.reshape(rows_s, -1),
                     sgb2, sma2, smb2, woa, wob, wout,
                     tm=rows_s, rows_per_batch=t, per_row=True).reshape(bs, t, D_MODEL)

    return (y_prompt, y_sample,
            lat.reshape(1, bp, s, A_KV_RANK), kr.reshape(1, bp, s, A_ROPE),
            kbf.reshape(1, bp, win_p, B_HEADS, B_HDIM), vbf.reshape(1, bp, win_p, B_HEADS, B_HDIM),
            lat2.reshape(1, bs, t, A_KV_RANK), kr2.reshape(1, bs, t, A_ROPE),
            kbf2.reshape(1, bs, t, B_HEADS, B_HDIM), vbf2.reshape(1, bs, t, B_HEADS, B_HDIM))
```

```python
import functools
import math

import jax
import jax.numpy as jnp
import numpy as np
from jax import lax
from jax.experimental import pallas as pl
from jax.experimental.pallas import tpu as pltpu

F32 = jnp.float32
BF16 = jnp.bfloat16

D_MODEL = 1024
CHUNK = 64
A_HEADS = 8
A_NOPE = 64
A_ROPE = 32
A_VDIM = 64
A_QK = A_NOPE + A_ROPE
A_Q_RANK = 384
A_KV_RANK = 256
A_WIDTH = A_HEADS * A_VDIM
A_SCALE = A_QK ** -0.5
B_HEADS = 8
B_HDIM = 64
B_WIDTH = B_HEADS * B_HDIM
B_LEFT_CHUNKS = 8
B_WINDOW = B_LEFT_CHUNKS * CHUNK
B_MAX_REL = 256
B_SCALE = B_HDIM ** -0.5
ROPE_BASE = 10000.0
NORM_EPS = 1e-6
NEG_INF = -1e30
LOG2E = math.log2(math.e)
MAX_EXP2_DRIFT = 60.0

LANES = 128
SLAB = LANES
HALF_ROPE = A_ROPE // 2
ROPE_LANE0 = A_NOPE
VMEM_LIMIT = 56 * 1024 * 1024

C_QLAT = 0
C_KVLAT = C_QLAT + A_Q_RANK
C_KR = C_KVLAT + A_KV_RANK
C_GA = C_KR + SLAB
C_QB = C_GA + A_WIDTH
C_KB = C_QB + B_WIDTH
C_VB = C_KB + B_WIDTH
C_GB = C_VB + B_WIDTH
C_MA = C_GB + B_WIDTH
C_MB = C_MA + D_MODEL
C_END = C_MB + D_MODEL

NT = (((1,), (1,)), ((), ()))


def _dot(a, b):
    return jnp.dot(a, b, preferred_element_type=F32)


def _dot_nt(a, b):
    return lax.dot_general(a, b, NT, preferred_element_type=F32)


def _rms_full(x, g):
    ms = jnp.mean(x * x, axis=-1, keepdims=True)
    return x * lax.rsqrt(ms + NORM_EPS) * g


def _seg_rms(x, seg, g):
    ms = _dot((x * x).astype(BF16), seg)
    return x * lax.rsqrt(ms + NORM_EPS) * g


def _rope_slab(x, cos_t, sin_lo, sin_hi):
    return (x * cos_t
            + pltpu.roll(x, SLAB - HALF_ROPE, axis=1) * sin_lo
            + pltpu.roll(x, HALF_ROPE, axis=1) * sin_hi)


def _adaln_kernel(c_ref, w_ref, b_ref, o_ref):
    c = c_ref[...]
    sc = c * jax.nn.sigmoid(c)
    o_ref[...] = jnp.dot(sc, w_ref[...], preferred_element_type=F32,
                         precision=lax.Precision.HIGHEST) + b_ref[...]


def _adaln(c_rows, w_ada, b_ada):
    n = c_rows.shape[0]
    tn = 1024
    return pl.pallas_call(
        _adaln_kernel,
        out_shape=jax.ShapeDtypeStruct((n, 3 * D_MODEL), F32),
        grid=(3 * D_MODEL // tn,),
        in_specs=[pl.BlockSpec((n, D_MODEL), lambda j: (0, 0)),
                  pl.BlockSpec((D_MODEL, tn), lambda j: (0, j)),
                  pl.BlockSpec((1, tn), lambda j: (0, j))],
        out_specs=pl.BlockSpec((n, tn), lambda j: (0, j)),
        compiler_params=pltpu.CompilerParams(dimension_semantics=("arbitrary",),
                                             vmem_limit_bytes=VMEM_LIMIT),
        name="adaln",
    )(c_rows, w_ada, b_ada.reshape(1, -1))


def _front_kernel(x_ref, scale_ref, shift_ref, cos_ref, slo_ref, shi_ref,
                  gnorm_ref, wlow_ref, win_ref, gql_ref, wuq_ref, gkv_ref, gq_ref, gkr_ref,
                  gqb_ref, gkb_ref, segq_ref, seg64_ref,
                  qa_ref, lat_ref, kr_ref, krs_ref, sga_ref, qb_ref, kb_ref, vb_ref,
                  kbf_ref, vbf_ref, sgb_ref, sma_ref, smb_ref, *, per_row):
    x = x_ref[...]
    if per_row:
        scale, shift = scale_ref[...], shift_ref[...]
    else:
        scale, shift = scale_ref[0], shift_ref[0]
    h = _rms_full(x, gnorm_ref[...]) * (1.0 + scale) + shift
    hb = h.astype(BF16)
    cos_t, sin_lo, sin_hi = cos_ref[...], slo_ref[...], shi_ref[...]

    def proj(c0, c1):
        if c1 <= C_GA:
            return _dot(hb, wlow_ref[:, c0:c1])
        return _dot(hb, win_ref[:, c0 - C_GA:c1 - C_GA])

    segq, seg64 = segq_ref[...], seg64_ref[...]
    pair = 2 * SLAB

    def finish_qa(p, qa, ms):
        qa = qa * lax.rsqrt(ms + NORM_EPS) * gq_ref[:, p * pair:(p + 1) * pair]
        for s in range(2):
            slab = _rope_slab(qa[:, s * SLAB:(s + 1) * SLAB], cos_t, sin_lo, sin_hi)
            qa_ref[:, p * pair + s * SLAB:p * pair + (s + 1) * SLAB] = slab.astype(BF16)

    def finish_qb(p, qb, ms):
        qb = qb * lax.rsqrt(ms + NORM_EPS) * gqb_ref[:, p * pair:(p + 1) * pair]
        qb_ref[:, p * pair:(p + 1) * pair] = qb.astype(BF16)

    def finish_kb(p, kb, ms):
        kb = kb * lax.rsqrt(ms + NORM_EPS) * gkb_ref[:, p * pair:(p + 1) * pair]
        kbf_ref[:, p * pair:(p + 1) * pair] = kb
        kb_ref[:, p * pair:(p + 1) * pair] = kb.astype(BF16)

    def seg_ms(x, seg):
        return _dot((x * x).astype(BF16), seg)

    low_rank = proj(C_QLAT, C_GA)
    ga = proj(C_GA, C_QB)
    sga_ref[...] = (ga * jax.nn.sigmoid(ga)).astype(BF16)

    r = _rms_full(low_rank[:, C_QLAT:C_KVLAT], gql_ref[...]).astype(BF16)
    qa = [None] * (A_HEADS // 2)
    qa[0] = _dot(r, wuq_ref[:, 0:pair])
    qb0 = proj(C_QB, C_QB + pair)
    qa[1] = _dot(r, wuq_ref[:, pair:2 * pair])
    finish_qa(0, qa[0], seg_ms(qa[0], segq))
    kb0 = proj(C_KB, C_KB + pair)
    qa[2] = _dot(r, wuq_ref[:, 2 * pair:3 * pair])
    finish_qa(1, qa[1], seg_ms(qa[1], segq))
    gb = proj(C_GB, C_MA)
    sgb_ref[...] = (gb * jax.nn.sigmoid(gb)).astype(BF16)
    qa[3] = _dot(r, wuq_ref[:, 3 * pair:4 * pair])
    finish_qa(2, qa[2], seg_ms(qa[2], segq))
    finish_qb(0, qb0, seg_ms(qb0, seg64))
    sma_ref[...] = jax.nn.sigmoid(proj(C_MA, C_MB)).astype(BF16)
    finish_qa(3, qa[3], seg_ms(qa[3], segq))
    finish_kb(0, kb0, seg_ms(kb0, seg64))
    qb1 = proj(C_QB + pair, C_QB + 2 * pair)
    kb1 = proj(C_KB + pair, C_KB + 2 * pair)
    smb_ref[...] = jax.nn.sigmoid(proj(C_MB, C_END)).astype(BF16)
    finish_qb(1, qb1, seg_ms(qb1, seg64))
    finish_kb(1, kb1, seg_ms(kb1, seg64))
    vb = proj(C_VB, C_GB)
    vbf_ref[...] = vb
    vb_ref[...] = vb.astype(BF16)

    lat_ref[...] = _rms_full(low_rank[:, C_KVLAT:C_KR], gkv_ref[...])
    krs = low_rank[:, C_KR:C_GA]
    ms = jnp.sum(krs * krs, axis=-1, keepdims=True) * (1.0 / A_ROPE)
    krs = _rope_slab(krs * lax.rsqrt(ms + NORM_EPS) * gkr_ref[...], cos_t, sin_lo, sin_hi)
    krs_ref[...] = krs.astype(BF16)
    kr_ref[...] = pltpu.roll(krs, SLAB - ROPE_LANE0, axis=1)[:, :A_ROPE]


def _front(x2d, scale, shift, rope_tabs, consts, *, tm, rows_per_batch, per_row, tail_rows):
    rows = x2d.shape[0]
    nt = rows // tm
    row = lambda i: (i, 0)
    fixed = lambda i: (0, 0)
    tpb = rows_per_batch // tm
    if per_row:
        mod_spec = pl.BlockSpec((tm, D_MODEL), row)
        tab_spec = pl.BlockSpec((tm, SLAB), row)
    else:
        mod_spec = pl.BlockSpec((1, 1, D_MODEL), lambda i: (i // tpb, 0, 0))
        tab_spec = pl.BlockSpec((tm, SLAB), lambda i: (i % tpb, 0))

    def full(a):
        return pl.BlockSpec(a.shape, fixed)

    def out(width, dtype):
        return jax.ShapeDtypeStruct((rows, width), dtype), pl.BlockSpec((tm, width), row)

    if tail_rows == rows_per_batch:
        tail = out(B_WIDTH, F32)
    else:
        ntail = tail_rows // tm
        tail = (jax.ShapeDtypeStruct((rows // rows_per_batch * tail_rows, B_WIDTH), F32),
                pl.BlockSpec((tm, B_WIDTH),
                             lambda i: (i // tpb * ntail + jnp.maximum(i % tpb - (tpb - ntail), 0), 0)))

    outs = [out(A_HEADS * SLAB, BF16),
            out(A_KV_RANK, F32),
            out(A_ROPE, F32),
            out(SLAB, BF16),
            out(A_WIDTH, BF16),
            out(B_WIDTH, BF16),
            out(B_WIDTH, BF16),
            out(B_WIDTH, BF16),
            tail,
            tail,
            out(B_WIDTH, BF16),
            out(D_MODEL, BF16),
            out(D_MODEL, BF16)]
    return pl.pallas_call(
        functools.partial(_front_kernel, per_row=per_row),
        out_shape=[o[0] for o in outs],
        grid=(nt,),
        in_specs=[pl.BlockSpec((tm, D_MODEL), row), mod_spec, mod_spec,
                  tab_spec, tab_spec, tab_spec] + [full(a) for a in consts],
        out_specs=[o[1] for o in outs],
        compiler_params=pltpu.CompilerParams(dimension_semantics=("arbitrary",),
                                             vmem_limit_bytes=VMEM_LIMIT),
        name="front",
    )(x2d, scale, shift, *rope_tabs, *consts)


def _expand_kernel(lat_ref, krs_ref, wuk_ref, wuv_ref, gk_ref, segk_ref, ka_ref, va_ref,
                   *, v_transposed):
    latb = lat_ref[...].astype(BF16)
    krs = krs_ref[...].astype(F32)
    segk = segk_ref[...]
    for p in range(A_HEADS // 2):
        c0 = 2 * SLAB * p
        kn = _dot(latb, wuk_ref[:, c0:c0 + 2 * SLAB])
        kn = _seg_rms(kn, segk, gk_ref[:, c0:c0 + 2 * SLAB])
        for s in range(2):
            ka_ref[:, c0 + s * SLAB:c0 + (s + 1) * SLAB] = (
                kn[:, s * SLAB:(s + 1) * SLAB] + krs).astype(BF16)
    if v_transposed:
        vt = _dot_nt(wuv_ref[...], latb).astype(BF16)
        for hp in range(A_HEADS // 2):
            va_ref[0, hp, 0] = vt[hp * SLAB:(hp + 1) * SLAB, :]
    else:
        va_ref[...] = _dot(latb, wuv_ref[...]).astype(BF16)


def _expand(lat2d, krs2d, wuk_p, wuv, gk, segk, *, tm, rows_per_batch, v_transposed):
    rows = lat2d.shape[0]
    row = lambda i: (i, 0)
    fixed = lambda i: (0, 0)
    if v_transposed:
        tpb = rows_per_batch // tm
        v_shape = jax.ShapeDtypeStruct((rows // rows_per_batch, A_HEADS // 2, tpb, SLAB, tm), BF16)
        v_spec = pl.BlockSpec((1, A_HEADS // 2, 1, SLAB, tm), lambda i: (i // tpb, 0, i % tpb, 0, 0))
    else:
        v_shape = jax.ShapeDtypeStruct((rows, A_WIDTH), BF16)
        v_spec = pl.BlockSpec((tm, A_WIDTH), row)
    return pl.pallas_call(
        functools.partial(_expand_kernel, v_transposed=v_transposed),
        out_shape=[jax.ShapeDtypeStruct((rows, A_HEADS * SLAB), BF16), v_shape],
        grid=(rows // tm,),
        in_specs=[pl.BlockSpec((tm, A_KV_RANK), row), pl.BlockSpec((tm, SLAB), row),
                  pl.BlockSpec(wuk_p.shape, fixed), pl.BlockSpec(wuv.shape, fixed),
                  pl.BlockSpec(gk.shape, fixed), pl.BlockSpec(segk.shape, fixed)],
        out_specs=[pl.BlockSpec((tm, A_HEADS * SLAB), row), v_spec],
        compiler_params=pltpu.CompilerParams(dimension_semantics=("arbitrary",),
                                             vmem_limit_bytes=VMEM_LIMIT),
        name="expand",
    )(lat2d, krs2d, wuk_p, wuv, gk, segk)


def _select_heads(o_even, o_odd):
    lane = lax.broadcasted_iota(jnp.int32, o_even.shape, 1)
    return jnp.where(lane < A_VDIM, o_even, o_odd)


def _mla_prompt_kernel(q_ref, k_ref, vt_ref, o_ref, *, tq, nsub):
    qi = pl.program_id(2)
    tk = vt_ref.shape[-1]
    ratio = tq // tk
    n_full = nsub * ratio * qi
    chunk_delta = (lax.broadcasted_iota(jnp.int32, (tk, tq), 0) // CHUNK
                   - lax.broadcasted_iota(jnp.int32, (tk, tq), 1) // CHUNK)
    ones = jnp.ones((16, tk), BF16)
    chains = [(e, h) for h in range(nsub) for e in range(2)]

    def step(j, carry, work, ref):
        k0 = pl.multiple_of(j * tk, tk)
        scores = {}
        new = list(carry)

        def qk(c):
            e, h = chains[c]
            k = k_ref[0, pl.ds(k0, tk), e * SLAB:(e + 1) * SLAB]
            q = q_ref[0, h * tq:(h + 1) * tq, e * SLAB:(e + 1) * SLAB]
            scores[c] = _dot_nt(k, q)

        def update(c, max_delta):
            e, h = chains[c]
            m, acc, drift = carry[c]
            s = scores.pop(c)
            if max_delta is not None:
                s = jnp.where(chunk_delta <= max_delta, s, NEG_INF)
            vt1 = jnp.concatenate([vt_ref[0, 0, j, e * A_VDIM:(e + 1) * A_VDIM, :], ones], axis=0)
            bmax = jnp.max(s, axis=0, keepdims=True)
            m_new = jnp.maximum(m, bmax)
            if ref == "exact":
                p = jnp.exp2(s - m_new).astype(BF16)
                new[c] = (m_new, jnp.exp2(m - m_new) * acc + _dot(vt1, p), drift)
            else:
                m_ref = m
                if ref == "probe":
                    m_ref = jnp.maximum(m, jnp.max(s[:CHUNK], axis=0, keepdims=True))
                    acc = jnp.exp2(m - m_ref) * acc
                p = jnp.exp2(s - m_ref).astype(BF16)
                new[c] = (m_new, jnp.exp2(m_ref - m_new) * (acc + _dot(vt1, p)),
                          jnp.maximum(drift, bmax - m_ref))

        for c, _ in work:
            qk(c)
        for c, max_delta in work:
            update(c, max_delta)
        return tuple(new)

    everyone = [(c, None) for c in range(len(chains))]
    diagonal = []
    for d in range(nsub * ratio):
        work = []
        for c, (e, h) in enumerate(chains):
            if d * tk >= (h + 1) * tq:
                continue
            before = (d + 1) * tk <= h * tq
            work.append((c, None if before else (h * tq - d * tk) // CHUNK))
        diagonal.append(work)

    def attend(streaming):
        carry = tuple((jnp.full((1, tq), NEG_INF, F32), jnp.zeros((A_VDIM + 16, tq), F32),
                       jnp.full((1, tq), NEG_INF, F32)) for _ in chains)
        if streaming:
            first = jnp.minimum(n_full, 1)
            carry = lax.fori_loop(0, first, lambda j, cr: step(j, cr, everyone, "probe"), carry)
            carry = lax.fori_loop(first, n_full, lambda j, cr: step(j, cr, everyone, "lagged"), carry)
        else:
            carry = lax.fori_loop(0, n_full, lambda j, cr: step(j, cr, everyone, "exact"), carry)
        for d, work in enumerate(diagonal):
            carry = step(n_full + d, carry, work,
                         "exact" if not streaming else ("probe" if d == 0 else "lagged"))
        for h in range(nsub):
            outs = [carry[2 * h + e][1] for e in range(2)]
            outs = [acc[:A_VDIM] / acc[A_VDIM:A_VDIM + 1] for acc in outs]
            o_ref[0, h * tq:(h + 1) * tq, :] = jnp.concatenate(outs, axis=0).T.astype(o_ref.dtype)
        return functools.reduce(jnp.maximum, [jnp.max(cr[2]) for cr in carry])

    worst_drift = attend(True)

    @pl.when(worst_drift > MAX_EXP2_DRIFT)
    def _():
        attend(False)


def _mla_prompt(qa, ka, vt, *, tq, nsub):
    b, s, _ = qa.shape
    tk = vt.shape[-1]
    assert vt.shape == (b, A_HEADS // 2, s // tk, SLAB, tk) and tq % tk == 0
    tqq = nsub * tq
    return pl.pallas_call(
        functools.partial(_mla_prompt_kernel, tq=tq, nsub=nsub),
        out_shape=jax.ShapeDtypeStruct((b, s, A_WIDTH), BF16),
        grid=(b, A_HEADS // 2, s // tqq),
        in_specs=[pl.BlockSpec((1, tqq, 2 * SLAB), lambda bi, hp, qi: (bi, qi, hp)),
                  pl.BlockSpec((1, s, 2 * SLAB), lambda bi, hp, qi: (bi, 0, hp)),
                  pl.BlockSpec((1, 1, s // tk, SLAB, tk), lambda bi, hp, qi: (bi, hp, 0, 0, 0))],
        out_specs=pl.BlockSpec((1, tqq, SLAB), lambda bi, hp, qi: (bi, qi, hp)),
        compiler_params=pltpu.CompilerParams(
            dimension_semantics=("arbitrary", "arbitrary", "arbitrary"),
            vmem_limit_bytes=VMEM_LIMIT),
        name="mla_prompt",
    )(qa, ka, vt)


def _softmax_over_parts(scores, values):
    m = functools.reduce(jnp.maximum, [jnp.max(s, axis=-1, keepdims=True) for s in scores])
    ps = [jnp.exp2(s - m) for s in scores]
    l = sum(jnp.sum(p, axis=-1, keepdims=True) for p in ps)
    return sum(_dot(p.astype(BF16), v) for p, v in zip(ps, values)) / l


def _mla_sample_kernel(q_ref, kc_ref, vc_ref, kn_ref, vn_ref, o_ref):
    values = [vc_ref[0], vn_ref[0]]
    outs = []
    for e in range(2):
        q = q_ref[0, :, e * SLAB:(e + 1) * SLAB]
        scores = [_dot_nt(q, k_ref[0, :, e * SLAB:(e + 1) * SLAB]) for k_ref in (kc_ref, kn_ref)]
        outs.append(_softmax_over_parts(scores, values))
    o_ref[0] = _select_heads(outs[0], outs[1]).astype(o_ref.dtype)


def _mla_sample(qa, ka_cache, va_cache, ka_new, va_new):
    b, t, _ = qa.shape
    n = ka_cache.shape[1]
    pair = lambda rows: pl.BlockSpec((1, rows, 2 * SLAB), lambda bi, hp: (bi, 0, hp))
    slab = lambda rows: pl.BlockSpec((1, rows, SLAB), lambda bi, hp: (bi, 0, hp))
    return pl.pallas_call(
        _mla_sample_kernel,
        out_shape=jax.ShapeDtypeStruct((b, t, A_WIDTH), BF16),
        grid=(b, A_HEADS // 2),
        in_specs=[pair(t), pair(n), slab(n), pair(t), slab(t)],
        out_specs=slab(t),
        compiler_params=pltpu.CompilerParams(dimension_semantics=("arbitrary", "arbitrary"),
                                             vmem_limit_bytes=VMEM_LIMIT),
        name="mla_sample",
    )(qa, ka_cache, va_cache, ka_new, va_new)


BAND_TQ = 256
BAND_KEYS = B_WINDOW + BAND_TQ
TOEPLITZ_W = 1024


def _band_table_kernel(rb_ref, onehot_ref, allowed_ref, tb_ref, tp_ref):
    g = jnp.dot(rb_ref[...], onehot_ref[...], preferred_element_type=F32,
                precision=lax.Precision.HIGHEST) * LOG2E
    allowed = allowed_ref[...] > 0.0
    for h in range(B_HEADS):
        row = jnp.broadcast_to(g[h:h + 1, :], (BAND_TQ, TOEPLITZ_W))
        t = pltpu.roll(row, 0, axis=1, stride=1, stride_axis=0)[:, :BAND_KEYS]
        tb_ref[h] = t
        tp_ref[h] = jnp.where(allowed, t, NEG_INF)


def _band_tables(rel_bias):
    n_rel = rel_bias.shape[1]
    n_pad = -(-n_rel // LANES) * LANES
    x = np.arange(TOEPLITZ_W)
    key_minus_query = np.where(x < BAND_KEYS, x, x - TOEPLITZ_W)
    dist = B_WINDOW - key_minus_query
    idx = np.clip(dist, -B_MAX_REL, B_MAX_REL) + B_MAX_REL
    onehot = np.zeros((n_pad, TOEPLITZ_W), np.float32)
    onehot[idx, x] = 1.0
    q_chunk = np.arange(BAND_TQ) // CHUNK + B_LEFT_CHUNKS
    k_chunk = np.arange(BAND_KEYS) // CHUNK
    allowed = (k_chunk[None, :] <= q_chunk[:, None]) & (k_chunk[None, :] >= q_chunk[:, None] - B_LEFT_CHUNKS)
    rb = jnp.pad(rel_bias, ((0, 0), (0, n_pad - n_rel)))
    shape = jax.ShapeDtypeStruct((B_HEADS, BAND_TQ, BAND_KEYS), F32)
    return pl.pallas_call(
        _band_table_kernel,
        out_shape=[shape, shape],
        compiler_params=pltpu.CompilerParams(vmem_limit_bytes=VMEM_LIMIT),
        name="band_table",
    )(rb, jnp.asarray(onehot), jnp.asarray(allowed.astype(np.float32)))


def _head_lane_masks(width):
    lane = lax.broadcasted_iota(jnp.int32, (1, width), 1)
    return [(lane < B_HDIM), (lane >= B_HDIM)]


def _band_prompt_kernel(q_ref, kp_ref, kc_ref, vp_ref, vc_ref, t_ref, o_ref, *, tq, nsub):
    qi = pl.program_id(2)
    nprev = B_WINDOW // tq
    nkb = nprev + 1

    def window(prev_ref, cur_ref, w):
        if w < nprev:
            return prev_ref[0, w * tq:(w + 1) * tq, :]
        return cur_ref[0, (w - nprev) * tq:(w - nprev + 1) * tq, :]

    masks = _head_lane_masks(SLAB)
    units = [(t, e) for t in range(nsub) for e in range(2)]
    scores = {}
    for t, e in units:
        q = q_ref[0, t * tq:(t + 1) * tq, :]
        qm = jnp.where(masks[e], q, jnp.zeros_like(q))
        ss = []
        for j in range(nkb):
            s = _dot_nt(qm, window(kp_ref, kc_ref, t + j)) + t_ref[e, :, j * tq:(j + 1) * tq]
            if t + j < nprev:
                s = jnp.where(qi > 0, s, NEG_INF)
            ss.append(s)
        scores[t, e] = ss
    outs = {}
    for t, e in units:
        ss = scores.pop((t, e))
        m = functools.reduce(jnp.maximum, [jnp.max(s, axis=-1, keepdims=True) for s in ss])
        ps = [jnp.exp2(s - m) for s in ss]
        l = sum(jnp.sum(p, axis=-1, keepdims=True) for p in ps)
        o = sum(_dot(p.astype(BF16), window(vp_ref, vc_ref, t + j)) for j, p in enumerate(ps))
        outs[t, e] = o / l
    for t in range(nsub):
        o_ref[0, t * tq:(t + 1) * tq, :] = _select_heads(outs[t, 0], outs[t, 1]).astype(o_ref.dtype)


def _band_prompt(qb, kb, vb, table, *, tq, nsub):
    b, s, _ = qb.shape
    tqq = nsub * tq
    per_step = tqq // B_WINDOW
    assert tqq % B_WINDOW == 0 and B_WINDOW % tq == 0
    cur = pl.BlockSpec((1, tqq, SLAB), lambda hp, bi, qi: (bi, qi, hp))
    prev = pl.BlockSpec((1, B_WINDOW, SLAB),
                        lambda hp, bi, qi: (bi, jnp.maximum(qi * per_step - 1, 0), hp))
    return pl.pallas_call(
        functools.partial(_band_prompt_kernel, tq=tq, nsub=nsub),
        out_shape=jax.ShapeDtypeStruct((b, s, B_WIDTH), BF16),
        grid=(B_HEADS // 2, b, s // tqq),
        in_specs=[cur, prev, cur, prev, cur,
                  pl.BlockSpec((2, tq, B_WINDOW + tq), lambda hp, bi, qi: (hp, 0, 0))],
        out_specs=cur,
        compiler_params=pltpu.CompilerParams(
            dimension_semantics=("arbitrary", "arbitrary", "arbitrary"),
            vmem_limit_bytes=VMEM_LIMIT),
        name="band_prompt",
    )(qb, kb, kb, vb, vb, table)


def _band_sample_kernel(q_ref, kc_ref, vc_ref, kn_ref, vn_ref, t_ref, o_ref):
    q = q_ref[0]
    keys = [kc_ref[0].astype(BF16), kn_ref[0]]
    values = [vc_ref[0].astype(BF16), vn_ref[0]]
    masks = _head_lane_masks(SLAB)
    outs = []
    for e in range(2):
        qm = jnp.where(masks[e], q, jnp.zeros_like(q))
        scores, col = [], 0
        for k in keys:
            scores.append(_dot_nt(qm, k) + t_ref[e, :, col:col + k.shape[0]])
            col += k.shape[0]
        outs.append(_softmax_over_parts(scores, values))
    o_ref[0] = _select_heads(outs[0], outs[1]).astype(o_ref.dtype)


def _band_sample(qb, kb_cache, vb_cache, kb_new, vb_new, table):
    b, t, _ = qb.shape
    n = kb_cache.shape[1]
    slab = lambda rows: pl.BlockSpec((1, rows, SLAB), lambda hp, bi: (bi, 0, hp))
    return pl.pallas_call(
        _band_sample_kernel,
        out_shape=jax.ShapeDtypeStruct((b, t, B_WIDTH), BF16),
        grid=(B_HEADS // 2, b),
        in_specs=[slab(t), slab(n), slab(n), slab(t), slab(t),
                  pl.BlockSpec((2, t, BAND_KEYS), lambda hp, bi: (hp, 0, 0))],
        out_specs=slab(t),
        compiler_params=pltpu.CompilerParams(dimension_semantics=("arbitrary", "arbitrary"),
                                             vmem_limit_bytes=VMEM_LIMIT),
        name="band_sample",
    )(qb, kb_cache, vb_cache, kb_new, vb_new, table)


def _back_kernel(x_ref, gate_ref, aa_ref, sga_ref, ab_ref, sgb_ref, sma_ref, smb_ref,
                 woa_ref, wob_ref, wout_ref, o_ref, *, per_row):
    gate = gate_ref[...] if per_row else gate_ref[0]
    ua = _dot(aa_ref[...] * sga_ref[...], woa_ref[...])
    ub = _dot(ab_ref[...] * sgb_ref[...], wob_ref[...])
    merged = sma_ref[...].astype(F32) * ua + smb_ref[...].astype(F32) * ub
    o_ref[...] = x_ref[...] + gate * _dot(merged.astype(BF16), wout_ref[...])


def _back(x2d, gate, aa, sga, ab, sgb, sma, smb, woa, wob, wout, *, tm, rows_per_batch, per_row):
    rows = x2d.shape[0]
    row = lambda i: (i, 0)
    fixed = lambda i: (0, 0)
    if per_row:
        gate_spec = pl.BlockSpec((tm, D_MODEL), row)
    else:
        tpb = rows_per_batch // tm
        gate_spec = pl.BlockSpec((1, 1, D_MODEL), lambda i: (i // tpb, 0, 0))
    return pl.pallas_call(
        functools.partial(_back_kernel, per_row=per_row),
        out_shape=jax.ShapeDtypeStruct((rows, D_MODEL), F32),
        grid=(rows // tm,),
        in_specs=[pl.BlockSpec((tm, D_MODEL), row), gate_spec,
                  pl.BlockSpec((tm, A_WIDTH), row), pl.BlockSpec((tm, A_WIDTH), row),
                  pl.BlockSpec((tm, B_WIDTH), row), pl.BlockSpec((tm, B_WIDTH), row),
                  pl.BlockSpec((tm, D_MODEL), row), pl.BlockSpec((tm, D_MODEL), row),
                  pl.BlockSpec(woa.shape, fixed), pl.BlockSpec(wob.shape, fixed),
                  pl.BlockSpec(wout.shape, fixed)],
        out_specs=pl.BlockSpec((tm, D_MODEL), row),
        compiler_params=pltpu.CompilerParams(dimension_semantics=("arbitrary",),
                                             vmem_limit_bytes=VMEM_LIMIT),
        name="back",
    )(x2d, gate, aa, sga, ab, sgb, sma, smb, woa, wob, wout)


def _seg_matrix(group_of_lane, sizes):
    lane = np.arange(2 * SLAB)
    slab = lane // SLAB
    grp = group_of_lane[lane % SLAB]
    same = (slab[:, None] == slab[None, :]) & (grp[:, None] == grp[None, :])
    return jnp.asarray(np.where(same, 1.0 / sizes[grp][None, :], 0.0), dtype=BF16)


def _rope_tables(pos):
    inv = ROPE_BASE ** (-np.arange(0, A_ROPE, 2, dtype=np.float64) / A_ROPE)
    ang = np.asarray(pos, np.float64)[:, None] * inv[None, :]
    cos, sin = np.cos(ang), np.sin(ang)
    n = ang.shape[0]
    ones = np.ones((n, A_NOPE))
    zeros = np.zeros((n, A_NOPE))
    z16 = np.zeros((n, HALF_ROPE))
    pad1 = np.ones((n, SLAB - A_QK))
    pad0 = np.zeros((n, SLAB - A_QK))
    cos_t = np.concatenate([ones, cos, cos, pad1], axis=1)
    sin_lo = np.concatenate([zeros, -sin, z16, pad0], axis=1)
    sin_hi = np.concatenate([zeros, z16, sin, pad0], axis=1)
    return tuple(jnp.asarray(a, dtype=F32) for a in (cos_t, sin_lo, sin_hi))


def kernel(x_prompt, x_sample, cache_mla_latent, cache_mla_krope, cache_band_k, cache_band_v,
           c_prompt, c_sample, g_norm, w_ada, b_ada, w_in, g_q_lat, w_uq, g_kv_lat, w_uk, w_uv,
           g_qn_a, g_qr_a, g_kn_a, g_kr_a, g_q_b, g_k_b, rel_bias_b, w_oa, w_ob, w_out):
    bp, s, _ = x_prompt.shape
    bs, t, _ = x_sample.shape
    past = cache_mla_latent.shape[2]
    win_s = cache_band_k.shape[2]
    win_p = min(B_WINDOW, s)
    depth = g_norm.shape[0]
    assert depth == 1

    seg_q = _seg_matrix(np.where(np.arange(SLAB) < A_NOPE, 0, np.where(np.arange(SLAB) < A_QK, 1, 2)),
                        np.array([A_NOPE, A_ROPE, SLAB - A_QK], np.float64))
    seg_64 = _seg_matrix(np.arange(SLAB) // B_HDIM, np.array([B_HDIM, B_HDIM], np.float64))
    seg_k = _seg_matrix(np.where(np.arange(SLAB) < A_NOPE, 0, 1),
                        np.array([A_NOPE, SLAB - A_NOPE], np.float64))

    assert win_s == B_WINDOW and t <= BAND_TQ
    pos_s = past + np.arange(t)

    (g_norm, w_ada, b_ada, w_in, g_q_lat, w_uq, g_kv_lat, w_uk, w_uv, g_qn_a, g_qr_a, g_kn_a,
     g_kr_a, g_q_b, g_k_b, rel_bias_b, w_oa, w_ob, w_out) = [
        a.reshape(a.shape[1:]) for a in
        (g_norm, w_ada, b_ada, w_in, g_q_lat, w_uq, g_kv_lat, w_uk, w_uv, g_qn_a, g_qr_a, g_kn_a,
         g_kr_a, g_q_b, g_k_b, rel_bias_b, w_oa, w_ob, w_out)]
    zpad = jnp.zeros((D_MODEL, SLAB - A_QK), F32)
    z64 = jnp.zeros((D_MODEL, A_NOPE), F32)
    win = w_in.astype(BF16)
    c_kr0 = A_Q_RANK + A_KV_RANK
    w_low = jnp.concatenate([win[:, :c_kr0], z64.astype(BF16), win[:, c_kr0:c_kr0 + A_ROPE],
                             zpad.astype(BF16)], axis=1)
    w_rest = win[:, c_kr0 + A_ROPE:]
    wuq_p = jnp.pad(w_uq, ((0, 0), (0, 0), (0, SLAB - A_QK))).reshape(A_Q_RANK, -1).astype(BF16)
    wuk_p = jnp.pad(w_uk, ((0, 0), (0, 0), (0, SLAB - A_NOPE))).reshape(A_KV_RANK, -1).astype(BF16)
    wuv_b = w_uv.reshape(A_KV_RANK, -1).astype(BF16)
    wuv_t = wuv_b.T
    qscale = A_SCALE * LOG2E
    gq = jnp.tile(jnp.concatenate([g_qn_a, g_qr_a, jnp.zeros((SLAB - A_QK,), F32)]) * qscale,
                  A_HEADS)[None]
    gk = jnp.tile(jnp.concatenate([g_kn_a, jnp.zeros((SLAB - A_NOPE,), F32)]), A_HEADS)[None]
    gkr = jnp.concatenate([jnp.zeros((A_NOPE,), F32), g_kr_a, jnp.zeros((SLAB - A_QK,), F32)])[None]
    gqb = jnp.tile(g_q_b * (B_SCALE * LOG2E), B_HEADS)[None]
    gkb = jnp.tile(g_k_b, B_HEADS)[None]
    consts = (g_norm[None], w_low, w_rest, g_q_lat[None], wuq_p, g_kv_lat[None], gq, gkr,
              gqb, gkb, seg_q, seg_64)
    woa, wob, wout = w_oa.astype(BF16), w_ob.astype(BF16), w_out.astype(BF16)

    c_rows = jnp.concatenate([c_prompt, jnp.zeros((8 - bp, D_MODEL), F32),
                              jnp.repeat(c_sample, t, axis=0)], axis=0)
    mod = _adaln(c_rows, w_ada, b_ada)
    shift, scale, gate = mod[:, :D_MODEL], mod[:, D_MODEL:2 * D_MODEL], mod[:, 2 * D_MODEL:]

    tm = 512
    xp2d = x_prompt.reshape(bp * s, D_MODEL)
    (qa, lat, kr, krs, sga, qb, kb, vb, kbf, vbf, sgb, sma, smb) = _front(
        xp2d, scale[:bp, None], shift[:bp, None], _rope_tables(np.arange(s)), consts,
        tm=tm, rows_per_batch=s, per_row=False, tail_rows=win_p)
    tk_mla = 512
    ka, vt = _expand(lat, krs, wuk_p, wuv_t, gk, seg_k, tm=tk_mla, rows_per_batch=s, v_transposed=True)
    attn_a = _mla_prompt(qa.reshape(bp, s, -1), ka.reshape(bp, s, -1), vt, tq=512, nsub=4)
    table_plain, table_band = _band_tables(rel_bias_b)
    attn_b = _band_prompt(qb.reshape(bp, s, -1), kb.reshape(bp, s, -1), vb.reshape(bp, s, -1),
                          table_band, tq=BAND_TQ, nsub=4)
    y_prompt = _back(xp2d, gate[:bp, None], attn_a.reshape(bp * s, -1), sga,
                     attn_b.reshape(bp * s, -1), sgb, sma, smb, woa, wob, wout,
                     tm=1024, rows_per_batch=s, per_row=False).reshape(bp, s, D_MODEL)

    xs2d = x_sample.reshape(bs * t, D_MODEL)
    rows_s = bs * t
    pos_tab = _rope_tables(np.tile(pos_s, bs))
    (qa2, lat2, kr2, krs2, sga2, qb2, kb2, vb2, kbf2, vbf2, sgb2, sma2, smb2) = _front(
        xs2d, scale[8:], shift[8:], pos_tab, consts, tm=rows_s, rows_per_batch=t, per_row=True,
        tail_rows=t)
    krs_cache = jnp.pad(cache_mla_krope.reshape(bs * past, A_ROPE),
                        ((0, 0), (A_NOPE, SLAB - A_QK))).astype(BF16)
    ka_c, va_c = _expand(cache_mla_latent.reshape(bs * past, A_KV_RANK), krs_cache,
                         wuk_p, wuv_b, gk, seg_k, tm=past // 2, rows_per_batch=past, v_transposed=False)
    ka_n, va_n = _expand(lat2, krs2, wuk_p, wuv_b, gk, seg_k, tm=rows_s, rows_per_batch=t,
                         v_transposed=False)
    attn_a2 = _mla_sample(qa2.reshape(bs, t, -1), ka_c.reshape(bs, past, -1), va_c.reshape(bs, past, -1),
                          ka_n.reshape(bs, t, -1), va_n.reshape(bs, t, -1))
    attn_b2 = _band_sample(qb2.reshape(bs, t, -1), cache_band_k.reshape(bs, win_s, B_WIDTH),
                           cache_band_v.reshape(bs, win_s, B_WIDTH), kb2.reshape(bs, t, -1),
                           vb2.reshape(bs, t, -1), table_plain)
    y_sample = _back(xs2d, gate[8:], attn_a2.reshape(rows_s, -1), sga2, attn_b2.reshape(rows_s, -1),
                     sgb2, sma2, smb2, woa, wob, wout,
                     tm=rows_s, rows_per_batch=t, per_row=True).reshape(bs, t, D_MODEL)

    return (y_prompt, y_sample,
            lat.reshape(1, bp, s, A_KV_RANK), kr.reshape(1, bp, s, A_ROPE),
            kbf.reshape(1, bp, win_p, B_HEADS, B_HDIM), vbf.reshape(1, bp, win_p, B_HEADS, B_HDIM),
            lat2.reshape(1, bs, t, A_KV_RANK), kr2.reshape(1, bs, t, A_ROPE),
            kbf2.reshape(1, bs, t, B_HEADS, B_HDIM), vbf2.reshape(1, bs, t, B_HEADS, B_HDIM))
```

```python
import functools
import math

import jax
import jax.numpy as jnp
import numpy as np
from jax import lax
from jax.experimental import pallas as pl
from jax.experimental.pallas import tpu as pltpu

F32 = jnp.float32
BF16 = jnp.bfloat16

D_MODEL = 1024
CHUNK = 64
A_HEADS = 8
A_NOPE = 64
A_ROPE = 32
A_VDIM = 64
A_QK = A_NOPE + A_ROPE
A_Q_RANK = 384
A_KV_RANK = 256
A_WIDTH = A_HEADS * A_VDIM
A_SCALE = A_QK ** -0.5
B_HEADS = 8
B_HDIM = 64
B_WIDTH = B_HEADS * B_HDIM
B_LEFT_CHUNKS = 8
B_WINDOW = B_LEFT_CHUNKS * CHUNK
B_MAX_REL = 256
B_SCALE = B_HDIM ** -0.5
ROPE_BASE = 10000.0
NORM_EPS = 1e-6
NEG_INF = -1e30
LOG2E = math.log2(math.e)
MAX_EXP2_DRIFT = 60.0

LANES = 128
SLAB = LANES
HALF_ROPE = A_ROPE // 2
ROPE_LANE0 = A_NOPE
VMEM_LIMIT = 56 * 1024 * 1024

C_QLAT = 0
C_KVLAT = C_QLAT + A_Q_RANK
C_KR = C_KVLAT + A_KV_RANK
C_GA = C_KR + SLAB
C_QB = C_GA + A_WIDTH
C_KB = C_QB + B_WIDTH
C_VB = C_KB + B_WIDTH
C_GB = C_VB + B_WIDTH
C_MA = C_GB + B_WIDTH
C_MB = C_MA + D_MODEL
C_END = C_MB + D_MODEL

NT = (((1,), (1,)), ((), ()))


def _dot(a, b):
    return jnp.dot(a, b, preferred_element_type=F32)


def _dot_nt(a, b):
    return lax.dot_general(a, b, NT, preferred_element_type=F32)


def _rms_full(x, g):
    ms = jnp.mean(x * x, axis=-1, keepdims=True)
    return x * lax.rsqrt(ms + NORM_EPS) * g


def _rope_slab(x, cos_t, sin_lo, sin_hi):
    return (x * cos_t
            + pltpu.roll(x, SLAB - HALF_ROPE, axis=1) * sin_lo
            + pltpu.roll(x, HALF_ROPE, axis=1) * sin_hi)


def _adaln_kernel(c_ref, w_ref, b_ref, o_ref):
    c = c_ref[...]
    sc = c * jax.nn.sigmoid(c)
    o_ref[...] = jnp.dot(sc, w_ref[...], preferred_element_type=F32,
                         precision=lax.Precision.HIGHEST) + b_ref[...]


def _adaln(c_rows, w_ada, b_ada):
    n = c_rows.shape[0]
    tn = 1024
    return pl.pallas_call(
        _adaln_kernel,
        out_shape=jax.ShapeDtypeStruct((n, 3 * D_MODEL), F32),
        grid=(3 * D_MODEL // tn,),
        in_specs=[pl.BlockSpec((n, D_MODEL), lambda j: (0, 0)),
                  pl.BlockSpec((D_MODEL, tn), lambda j: (0, j)),
                  pl.BlockSpec((1, tn), lambda j: (0, j))],
        out_specs=pl.BlockSpec((n, tn), lambda j: (0, j)),
        compiler_params=pltpu.CompilerParams(dimension_semantics=("arbitrary",),
                                             vmem_limit_bytes=VMEM_LIMIT),
        name="adaln",
    )(c_rows, w_ada, b_ada.reshape(1, -1))


def _finish_keys(p, kn, ms, *, ka_ref, krs, gk_ref):
    pair = 2 * SLAB
    kn = kn * lax.rsqrt(ms + NORM_EPS) * gk_ref[:, p * pair:(p + 1) * pair]
    for s in range(2):
        ka_ref[:, p * pair + s * SLAB:p * pair + (s + 1) * SLAB] = (
            kn[:, s * SLAB:(s + 1) * SLAB] + krs).astype(BF16)


def _store_values(latb, wuv_ref, va_ref, v_transposed):
    if v_transposed:
        vt = _dot_nt(wuv_ref[...], latb).astype(BF16)
        for hp in range(A_HEADS // 2):
            va_ref[0, hp, 0] = vt[hp * SLAB:(hp + 1) * SLAB, :]
    else:
        va_ref[...] = _dot(latb, wuv_ref[...]).astype(BF16)


def _front_kernel(x_ref, scale_ref, shift_ref, cos_ref, slo_ref, shi_ref,
                  gnorm_ref, wlow_ref, win_ref, gql_ref, wuq_ref, gkv_ref, gq_ref, gkr_ref,
                  gqb_ref, gkb_ref, segq_ref, seg64_ref, wuk_ref, wuv_ref, gk_ref, segk_ref,
                  qa_ref, lat_ref, kr_ref, sga_ref, qb_ref, kb_ref, vb_ref,
                  kbf_ref, vbf_ref, sgb_ref, sma_ref, smb_ref, ka_ref, va_ref,
                  *, per_row, v_transposed):
    x = x_ref[...]
    if per_row:
        scale, shift = scale_ref[...], shift_ref[...]
    else:
        scale, shift = scale_ref[0], shift_ref[0]
    h = _rms_full(x, gnorm_ref[...]) * (1.0 + scale) + shift
    hb = h.astype(BF16)
    cos_t, sin_lo, sin_hi = cos_ref[...], slo_ref[...], shi_ref[...]

    def proj(c0, c1):
        if c1 <= C_GA:
            return _dot(hb, wlow_ref[:, c0:c1])
        return _dot(hb, win_ref[:, c0 - C_GA:c1 - C_GA])

    segq, seg64 = segq_ref[...], seg64_ref[...]
    pair = 2 * SLAB

    def finish_qa(p, qa, ms):
        qa = qa * lax.rsqrt(ms + NORM_EPS) * gq_ref[:, p * pair:(p + 1) * pair]
        for s in range(2):
            slab = _rope_slab(qa[:, s * SLAB:(s + 1) * SLAB], cos_t, sin_lo, sin_hi)
            qa_ref[:, p * pair + s * SLAB:p * pair + (s + 1) * SLAB] = slab.astype(BF16)

    def finish_qb(p, qb, ms):
        qb = qb * lax.rsqrt(ms + NORM_EPS) * gqb_ref[:, p * pair:(p + 1) * pair]
        qb_ref[:, p * pair:(p + 1) * pair] = qb.astype(BF16)

    def finish_kb(p, kb, ms):
        kb = kb * lax.rsqrt(ms + NORM_EPS) * gkb_ref[:, p * pair:(p + 1) * pair]
        kbf_ref[:, p * pair:(p + 1) * pair] = kb
        kb_ref[:, p * pair:(p + 1) * pair] = kb.astype(BF16)

    def seg_ms(x, seg):
        return _dot((x * x).astype(BF16), seg)

    low_rank = proj(C_QLAT, C_GA)
    ga = proj(C_GA, C_QB)
    sga_ref[...] = (ga * jax.nn.sigmoid(ga)).astype(BF16)

    lat = _rms_full(low_rank[:, C_KVLAT:C_KR], gkv_ref[...])
    lat_ref[...] = lat
    latb = lat.astype(BF16)
    krs = low_rank[:, C_KR:C_GA]
    ms = jnp.sum(krs * krs, axis=-1, keepdims=True) * (1.0 / A_ROPE)
    krs = _rope_slab(krs * lax.rsqrt(ms + NORM_EPS) * gkr_ref[...], cos_t, sin_lo, sin_hi)
    kr_ref[...] = pltpu.roll(krs, SLAB - ROPE_LANE0, axis=1)[:, :A_ROPE]

    _expand_keys = functools.partial(_finish_keys, ka_ref=ka_ref, krs=krs, gk_ref=gk_ref)
    segk = segk_ref[...]

    def kn_dot(p):
        return _dot(latb, wuk_ref[:, p * pair:(p + 1) * pair])

    r = _rms_full(low_rank[:, C_QLAT:C_KVLAT], gql_ref[...]).astype(BF16)
    qa = [None] * (A_HEADS // 2)
    kn = [None] * (A_HEADS // 2)
    qa[0] = _dot(r, wuq_ref[:, 0:pair])
    qb0 = proj(C_QB, C_QB + pair)
    qa[1] = _dot(r, wuq_ref[:, pair:2 * pair])
    finish_qa(0, qa[0], seg_ms(qa[0], segq))
    kb0 = proj(C_KB, C_KB + pair)
    qa[2] = _dot(r, wuq_ref[:, 2 * pair:3 * pair])
    finish_qa(1, qa[1], seg_ms(qa[1], segq))
    kn[0] = kn_dot(0)
    gb = proj(C_GB, C_MA)
    sgb_ref[...] = (gb * jax.nn.sigmoid(gb)).astype(BF16)
    qa[3] = _dot(r, wuq_ref[:, 3 * pair:4 * pair])
    finish_qa(2, qa[2], seg_ms(qa[2], segq))
    finish_qb(0, qb0, seg_ms(qb0, seg64))
    kn[1] = kn_dot(1)
    sma_ref[...] = jax.nn.sigmoid(proj(C_MA, C_MB)).astype(BF16)
    finish_qa(3, qa[3], seg_ms(qa[3], segq))
    finish_kb(0, kb0, seg_ms(kb0, seg64))
    _expand_keys(0, kn[0], seg_ms(kn[0], segk))
    qb1 = proj(C_QB + pair, C_QB + 2 * pair)
    kb1 = proj(C_KB + pair, C_KB + 2 * pair)
    kn[2] = kn_dot(2)
    smb_ref[...] = jax.nn.sigmoid(proj(C_MB, C_END)).astype(BF16)
    finish_qb(1, qb1, seg_ms(qb1, seg64))
    finish_kb(1, kb1, seg_ms(kb1, seg64))
    _expand_keys(1, kn[1], seg_ms(kn[1], segk))
    kn[3] = kn_dot(3)
    vb = proj(C_VB, C_GB)
    vbf_ref[...] = vb
    vb_ref[...] = vb.astype(BF16)
    _expand_keys(2, kn[2], seg_ms(kn[2], segk))
    _store_values(latb, wuv_ref, va_ref, v_transposed)
    _expand_keys(3, kn[3], seg_ms(kn[3], segk))


def _front(x2d, scale, shift, rope_tabs, consts, *, tm, rows_per_batch, per_row, tail_rows,
           v_transposed):
    rows = x2d.shape[0]
    nt = rows // tm
    row = lambda i: (i, 0)
    fixed = lambda i: (0, 0)
    tpb = rows_per_batch // tm
    if per_row:
        mod_spec = pl.BlockSpec((tm, D_MODEL), row)
        tab_spec = pl.BlockSpec((tm, SLAB), row)
    else:
        mod_spec = pl.BlockSpec((1, 1, D_MODEL), lambda i: (i // tpb, 0, 0))
        tab_spec = pl.BlockSpec((tm, SLAB), lambda i: (i % tpb, 0))

    def full(a):
        return pl.BlockSpec(a.shape, fixed)

    def out(width, dtype):
        return jax.ShapeDtypeStruct((rows, width), dtype), pl.BlockSpec((tm, width), row)

    if tail_rows == rows_per_batch:
        tail = out(B_WIDTH, F32)
    else:
        ntail = tail_rows // tm
        tail = (jax.ShapeDtypeStruct((rows // rows_per_batch * tail_rows, B_WIDTH), F32),
                pl.BlockSpec((tm, B_WIDTH),
                             lambda i: (i // tpb * ntail + jnp.maximum(i % tpb - (tpb - ntail), 0), 0)))

    if v_transposed:
        values = (jax.ShapeDtypeStruct((rows // rows_per_batch, A_HEADS // 2, tpb, SLAB, tm), BF16),
                  pl.BlockSpec((1, A_HEADS // 2, 1, SLAB, tm), lambda i: (i // tpb, 0, i % tpb, 0, 0)))
    else:
        values = out(A_WIDTH, BF16)

    outs = [out(A_HEADS * SLAB, BF16),
            out(A_KV_RANK, F32),
            out(A_ROPE, F32),
            out(A_WIDTH, BF16),
            out(B_WIDTH, BF16),
            out(B_WIDTH, BF16),
            out(B_WIDTH, BF16),
            tail,
            tail,
            out(B_WIDTH, BF16),
            out(D_MODEL, BF16),
            out(D_MODEL, BF16),
            out(A_HEADS * SLAB, BF16),
            values]
    return pl.pallas_call(
        functools.partial(_front_kernel, per_row=per_row, v_transposed=v_transposed),
        out_shape=[o[0] for o in outs],
        grid=(nt,),
        in_specs=[pl.BlockSpec((tm, D_MODEL), row), mod_spec, mod_spec,
                  tab_spec, tab_spec, tab_spec] + [full(a) for a in consts],
        out_specs=[o[1] for o in outs],
        compiler_params=pltpu.CompilerParams(dimension_semantics=("arbitrary",),
                                             vmem_limit_bytes=VMEM_LIMIT),
        name="front",
    )(x2d, scale, shift, *rope_tabs, *consts)


def _expand_kernel(lat_ref, krs_ref, wuk_ref, wuv_ref, gk_ref, segk_ref, ka_ref, va_ref):
    latb = lat_ref[...].astype(BF16)
    krs = krs_ref[...].astype(F32)
    segk = segk_ref[...]
    pair = 2 * SLAB
    for p in range(A_HEADS // 2):
        kn = _dot(latb, wuk_ref[:, p * pair:(p + 1) * pair])
        _finish_keys(p, kn, _dot((kn * kn).astype(BF16), segk), ka_ref=ka_ref, krs=krs, gk_ref=gk_ref)
    _store_values(latb, wuv_ref, va_ref, False)


def _expand(lat2d, krs2d, wuk_p, wuv, gk, segk, *, tm):
    rows = lat2d.shape[0]
    row = lambda i: (i, 0)
    fixed = lambda i: (0, 0)
    return pl.pallas_call(
        _expand_kernel,
        out_shape=[jax.ShapeDtypeStruct((rows, A_HEADS * SLAB), BF16),
                   jax.ShapeDtypeStruct((rows, A_WIDTH), BF16)],
        grid=(rows // tm,),
        in_specs=[pl.BlockSpec((tm, A_KV_RANK), row), pl.BlockSpec((tm, SLAB), row),
                  pl.BlockSpec(wuk_p.shape, fixed), pl.BlockSpec(wuv.shape, fixed),
                  pl.BlockSpec(gk.shape, fixed), pl.BlockSpec(segk.shape, fixed)],
        out_specs=[pl.BlockSpec((tm, A_HEADS * SLAB), row), pl.BlockSpec((tm, A_WIDTH), row)],
        compiler_params=pltpu.CompilerParams(dimension_semantics=("arbitrary",),
                                             vmem_limit_bytes=VMEM_LIMIT),
        name="expand",
    )(lat2d, krs2d, wuk_p, wuv, gk, segk)


def _select_heads(o_even, o_odd):
    lane = lax.broadcasted_iota(jnp.int32, o_even.shape, 1)
    return jnp.where(lane < A_VDIM, o_even, o_odd)


def _mla_prompt_kernel(q_ref, k_ref, vt_ref, o_ref, *, tq, nsub):
    qi = pl.program_id(2)
    tk = vt_ref.shape[-1]
    ratio = tq // tk
    n_full = nsub * ratio * qi
    chunk_delta = (lax.broadcasted_iota(jnp.int32, (tk, tq), 0) // CHUNK
                   - lax.broadcasted_iota(jnp.int32, (tk, tq), 1) // CHUNK)
    ones = jnp.ones((16, tk), BF16)
    chains = [(e, h) for h in range(nsub) for e in range(2)]

    def step(j, carry, work, ref):
        k0 = pl.multiple_of(j * tk, tk)
        scores = {}
        new = list(carry)

        def qk(c):
            e, h = chains[c]
            k = k_ref[0, pl.ds(k0, tk), e * SLAB:(e + 1) * SLAB]
            q = q_ref[0, h * tq:(h + 1) * tq, e * SLAB:(e + 1) * SLAB]
            scores[c] = _dot_nt(k, q)

        def update(c, max_delta):
            e, h = chains[c]
            m, acc, drift = carry[c]
            s = scores.pop(c)
            if max_delta is not None:
                s = jnp.where(chunk_delta <= max_delta, s, NEG_INF)
            vt1 = jnp.concatenate([vt_ref[0, 0, j, e * A_VDIM:(e + 1) * A_VDIM, :], ones], axis=0)
            bmax = jnp.max(s, axis=0, keepdims=True)
            m_new = jnp.maximum(m, bmax)
            if ref == "exact":
                p = jnp.exp2(s - m_new).astype(BF16)
                new[c] = (m_new, jnp.exp2(m - m_new) * acc + _dot(vt1, p), drift)
            else:
                m_ref = m
                if ref == "probe":
                    m_ref = jnp.maximum(m, jnp.max(s[:CHUNK], axis=0, keepdims=True))
                    acc = jnp.exp2(m - m_ref) * acc
                p = jnp.exp2(s - m_ref).astype(BF16)
                new[c] = (m_new, jnp.exp2(m_ref - m_new) * (acc + _dot(vt1, p)),
                          jnp.maximum(drift, bmax - m_ref))

        for c, _ in work:
            qk(c)
        for c, max_delta in work:
            update(c, max_delta)
        return tuple(new)

    everyone = [(c, None) for c in range(len(chains))]
    diagonal = []
    for d in range(nsub * ratio):
        work = []
        for c, (e, h) in enumerate(chains):
            if d * tk >= (h + 1) * tq:
                continue
            before = (d + 1) * tk <= h * tq
            work.append((c, None if before else (h * tq - d * tk) // CHUNK))
        diagonal.append(work)

    def attend(streaming):
        carry = tuple((jnp.full((1, tq), NEG_INF, F32), jnp.zeros((A_VDIM + 16, tq), F32),
                       jnp.full((1, tq), NEG_INF, F32)) for _ in chains)
        if streaming:
            first = jnp.minimum(n_full, 1)
            carry = lax.fori_loop(0, first, lambda j, cr: step(j, cr, everyone, "probe"), carry)
            carry = lax.fori_loop(first, n_full, lambda j, cr: step(j, cr, everyone, "lagged"), carry)
        else:
            carry = lax.fori_loop(0, n_full, lambda j, cr: step(j, cr, everyone, "exact"), carry)
        for d, work in enumerate(diagonal):
            carry = step(n_full + d, carry, work,
                         "exact" if not streaming else ("probe" if d == 0 else "lagged"))
        for h in range(nsub):
            outs = [carry[2 * h + e][1] for e in range(2)]
            outs = [acc[:A_VDIM] / acc[A_VDIM:A_VDIM + 1] for acc in outs]
            o_ref[0, h * tq:(h + 1) * tq, :] = jnp.concatenate(outs, axis=0).T.astype(o_ref.dtype)
        return functools.reduce(jnp.maximum, [jnp.max(cr[2]) for cr in carry])

    worst_drift = attend(True)

    @pl.when(worst_drift > MAX_EXP2_DRIFT)
    def _():
        attend(False)


def _mla_prompt(qa, ka, vt, *, tq, nsub):
    b, s, _ = qa.shape
    tk = vt.shape[-1]
    assert vt.shape == (b, A_HEADS // 2, s // tk, SLAB, tk) and tq % tk == 0
    tqq = nsub * tq
    return pl.pallas_call(
        functools.partial(_mla_prompt_kernel, tq=tq, nsub=nsub),
        out_shape=jax.ShapeDtypeStruct((b, s, A_WIDTH), BF16),
        grid=(b, A_HEADS // 2, s // tqq),
        in_specs=[pl.BlockSpec((1, tqq, 2 * SLAB), lambda bi, hp, qi: (bi, qi, hp)),
                  pl.BlockSpec((1, s, 2 * SLAB), lambda bi, hp, qi: (bi, 0, hp)),
                  pl.BlockSpec((1, 1, s // tk, SLAB, tk), lambda bi, hp, qi: (bi, hp, 0, 0, 0))],
        out_specs=pl.BlockSpec((1, tqq, SLAB), lambda bi, hp, qi: (bi, qi, hp)),
        compiler_params=pltpu.CompilerParams(
            dimension_semantics=("arbitrary", "arbitrary", "arbitrary"),
            vmem_limit_bytes=VMEM_LIMIT),
        name="mla_prompt",
    )(qa, ka, vt)


def _softmax_over_parts(scores, values):
    m = functools.reduce(jnp.maximum, [jnp.max(s, axis=-1, keepdims=True) for s in scores])
    ps = [jnp.exp2(s - m) for s in scores]
    l = sum(jnp.sum(p, axis=-1, keepdims=True) for p in ps)
    return sum(_dot(p.astype(BF16), v) for p, v in zip(ps, values)) / l


def _mla_sample_kernel(q_ref, kc_ref, vc_ref, kn_ref, vn_ref, o_ref):
    values = [vc_ref[0], vn_ref[0]]
    outs = []
    for e in range(2):
        q = q_ref[0, :, e * SLAB:(e + 1) * SLAB]
        scores = [_dot_nt(q, k_ref[0, :, e * SLAB:(e + 1) * SLAB]) for k_ref in (kc_ref, kn_ref)]
        outs.append(_softmax_over_parts(scores, values))
    o_ref[0] = _select_heads(outs[0], outs[1]).astype(o_ref.dtype)


def _mla_sample(qa, ka_cache, va_cache, ka_new, va_new):
    b, t, _ = qa.shape
    n = ka_cache.shape[1]
    pair = lambda rows: pl.BlockSpec((1, rows, 2 * SLAB), lambda bi, hp: (bi, 0, hp))
    slab = lambda rows: pl.BlockSpec((1, rows, SLAB), lambda bi, hp: (bi, 0, hp))
    return pl.pallas_call(
        _mla_sample_kernel,
        out_shape=jax.ShapeDtypeStruct((b, t, A_WIDTH), BF16),
        grid=(b, A_HEADS // 2),
        in_specs=[pair(t), pair(n), slab(n), pair(t), slab(t)],
        out_specs=slab(t),
        compiler_params=pltpu.CompilerParams(dimension_semantics=("arbitrary", "arbitrary"),
                                             vmem_limit_bytes=VMEM_LIMIT),
        name="mla_sample",
    )(qa, ka_cache, va_cache, ka_new, va_new)


BAND_TQ = 256
BAND_KEYS = B_WINDOW + BAND_TQ
TOEPLITZ_W = 1024


def _band_table_kernel(rb_ref, onehot_ref, allowed_ref, tb_ref, tp_ref):
    g = jnp.dot(rb_ref[...], onehot_ref[...], preferred_element_type=F32,
                precision=lax.Precision.HIGHEST) * LOG2E
    allowed = allowed_ref[...] > 0.0
    for h in range(B_HEADS):
        row = jnp.broadcast_to(g[h:h + 1, :], (BAND_TQ, TOEPLITZ_W))
        t = pltpu.roll(row, 0, axis=1, stride=1, stride_axis=0)[:, :BAND_KEYS]
        tb_ref[h] = t
        tp_ref[h] = jnp.where(allowed, t, NEG_INF)


def _band_tables(rel_bias):
    n_rel = rel_bias.shape[1]
    n_pad = -(-n_rel // LANES) * LANES
    x = np.arange(TOEPLITZ_W)
    key_minus_query = np.where(x < BAND_KEYS, x, x - TOEPLITZ_W)
    dist = B_WINDOW - key_minus_query
    idx = np.clip(dist, -B_MAX_REL, B_MAX_REL) + B_MAX_REL
    onehot = np.zeros((n_pad, TOEPLITZ_W), np.float32)
    onehot[idx, x] = 1.0
    q_chunk = np.arange(BAND_TQ) // CHUNK + B_LEFT_CHUNKS
    k_chunk = np.arange(BAND_KEYS) // CHUNK
    allowed = (k_chunk[None, :] <= q_chunk[:, None]) & (k_chunk[None, :] >= q_chunk[:, None] - B_LEFT_CHUNKS)
    rb = jnp.pad(rel_bias, ((0, 0), (0, n_pad - n_rel)))
    shape = jax.ShapeDtypeStruct((B_HEADS, BAND_TQ, BAND_KEYS), F32)
    return pl.pallas_call(
        _band_table_kernel,
        out_shape=[shape, shape],
        compiler_params=pltpu.CompilerParams(vmem_limit_bytes=VMEM_LIMIT),
        name="band_table",
    )(rb, jnp.asarray(onehot), jnp.asarray(allowed.astype(np.float32)))


def _head_lane_masks(width):
    lane = lax.broadcasted_iota(jnp.int32, (1, width), 1)
    return [(lane < B_HDIM), (lane >= B_HDIM)]


def _band_prompt_kernel(q_ref, kp_ref, kc_ref, vp_ref, vc_ref, t_ref, o_ref, *, tq, nsub):
    qi = pl.program_id(2)
    nprev = B_WINDOW // tq
    nkb = nprev + 1

    def window(prev_ref, cur_ref, w):
        if w < nprev:
            return prev_ref[0, w * tq:(w + 1) * tq, :]
        return cur_ref[0, (w - nprev) * tq:(w - nprev + 1) * tq, :]

    masks = _head_lane_masks(SLAB)
    units = [(t, e) for t in range(nsub) for e in range(2)]
    scores = {}
    for t, e in units:
        q = q_ref[0, t * tq:(t + 1) * tq, :]
        qm = jnp.where(masks[e], q, jnp.zeros_like(q))
        ss = []
        for j in range(nkb):
            s = _dot_nt(qm, window(kp_ref, kc_ref, t + j)) + t_ref[e, :, j * tq:(j + 1) * tq]
            if t + j < nprev:
                s = jnp.where(qi > 0, s, NEG_INF)
            ss.append(s)
        scores[t, e] = ss
    outs = {}
    for t, e in units:
        ss = scores.pop((t, e))
        m = functools.reduce(jnp.maximum, [jnp.max(s, axis=-1, keepdims=True) for s in ss])
        ps = [jnp.exp2(s - m) for s in ss]
        l = sum(jnp.sum(p, axis=-1, keepdims=True) for p in ps)
        o = sum(_dot(p.astype(BF16), window(vp_ref, vc_ref, t + j)) for j, p in enumerate(ps))
        outs[t, e] = o / l
    for t in range(nsub):
        o_ref[0, t * tq:(t + 1) * tq, :] = _select_heads(outs[t, 0], outs[t, 1]).astype(o_ref.dtype)


def _band_prompt(qb, kb, vb, table, *, tq, nsub):
    b, s, _ = qb.shape
    tqq = nsub * tq
    per_step = tqq // B_WINDOW
    assert tqq % B_WINDOW == 0 and B_WINDOW % tq == 0
    cur = pl.BlockSpec((1, tqq, SLAB), lambda hp, bi, qi: (bi, qi, hp))
    prev = pl.BlockSpec((1, B_WINDOW, SLAB),
                        lambda hp, bi, qi: (bi, jnp.maximum(qi * per_step - 1, 0), hp))
    return pl.pallas_call(
        functools.partial(_band_prompt_kernel, tq=tq, nsub=nsub),
        out_shape=jax.ShapeDtypeStruct((b, s, B_WIDTH), BF16),
        grid=(B_HEADS // 2, b, s // tqq),
        in_specs=[cur, prev, cur, prev, cur,
                  pl.BlockSpec((2, tq, B_WINDOW + tq), lambda hp, bi, qi: (hp, 0, 0))],
        out_specs=cur,
        compiler_params=pltpu.CompilerParams(
            dimension_semantics=("arbitrary", "arbitrary", "arbitrary"),
            vmem_limit_bytes=VMEM_LIMIT),
        name="band_prompt",
    )(qb, kb, kb, vb, vb, table)


def _band_sample_kernel(q_ref, kc_ref, vc_ref, kn_ref, vn_ref, t_ref, o_ref):
    q = q_ref[0]
    keys = [kc_ref[0].astype(BF16), kn_ref[0]]
    values = [vc_ref[0].astype(BF16), vn_ref[0]]
    masks = _head_lane_masks(SLAB)
    outs = []
    for e in range(2):
        qm = jnp.where(masks[e], q, jnp.zeros_like(q))
        scores, col = [], 0
        for k in keys:
            scores.append(_dot_nt(qm, k) + t_ref[e, :, col:col + k.shape[0]])
            col += k.shape[0]
        outs.append(_softmax_over_parts(scores, values))
    o_ref[0] = _select_heads(outs[0], outs[1]).astype(o_ref.dtype)


def _band_sample(qb, kb_cache, vb_cache, kb_new, vb_new, table):
    b, t, _ = qb.shape
    n = kb_cache.shape[1]
    slab = lambda rows: pl.BlockSpec((1, rows, SLAB), lambda hp, bi: (bi, 0, hp))
    return pl.pallas_call(
        _band_sample_kernel,
        out_shape=jax.ShapeDtypeStruct((b, t, B_WIDTH), BF16),
        grid=(B_HEADS // 2, b),
        in_specs=[slab(t), slab(n), slab(n), slab(t), slab(t),
                  pl.BlockSpec((2, t, BAND_KEYS), lambda hp, bi: (hp, 0, 0))],
        out_specs=slab(t),
        compiler_params=pltpu.CompilerParams(dimension_semantics=("arbitrary", "arbitrary"),
                                             vmem_limit_bytes=VMEM_LIMIT),
        name="band_sample",
    )(qb, kb_cache, vb_cache, kb_new, vb_new, table)


def _back_kernel(x_ref, gate_ref, aa_ref, sga_ref, ab_ref, sgb_ref, sma_ref, smb_ref,
                 woa_ref, wob_ref, wout_ref, o_ref, *, per_row):
    gate = gate_ref[...] if per_row else gate_ref[0]
    ua = _dot(aa_ref[...] * sga_ref[...], woa_ref[...])
    ub = _dot(ab_ref[...] * sgb_ref[...], wob_ref[...])
    merged = sma_ref[...].astype(F32) * ua + smb_ref[...].astype(F32) * ub
    o_ref[...] = x_ref[...] + gate * _dot(merged.astype(BF16), wout_ref[...])


def _back(x2d, gate, aa, sga, ab, sgb, sma, smb, woa, wob, wout, *, tm, rows_per_batch, per_row):
    rows = x2d.shape[0]
    row = lambda i: (i, 0)
    fixed = lambda i: (0, 0)
    if per_row:
        gate_spec = pl.BlockSpec((tm, D_MODEL), row)
    else:
        tpb = rows_per_batch // tm
        gate_spec = pl.BlockSpec((1, 1, D_MODEL), lambda i: (i // tpb, 0, 0))
    return pl.pallas_call(
        functools.partial(_back_kernel, per_row=per_row),
        out_shape=jax.ShapeDtypeStruct((rows, D_MODEL), F32),
        grid=(rows // tm,),
        in_specs=[pl.BlockSpec((tm, D_MODEL), row), gate_spec,
                  pl.BlockSpec((tm, A_WIDTH), row), pl.BlockSpec((tm, A_WIDTH), row),
                  pl.BlockSpec((tm, B_WIDTH), row), pl.BlockSpec((tm, B_WIDTH), row),
                  pl.BlockSpec((tm, D_MODEL), row), pl.BlockSpec((tm, D_MODEL), row),
                  pl.BlockSpec(woa.shape, fixed), pl.BlockSpec(wob.shape, fixed),
                  pl.BlockSpec(wout.shape, fixed)],
        out_specs=pl.BlockSpec((tm, D_MODEL), row),
        compiler_params=pltpu.CompilerParams(dimension_semantics=("arbitrary",),
                                             vmem_limit_bytes=VMEM_LIMIT),
        name="back",
    )(x2d, gate, aa, sga, ab, sgb, sma, smb, woa, wob, wout)


def _seg_matrix(group_of_lane, sizes):
    lane = np.arange(2 * SLAB)
    slab = lane // SLAB
    grp = group_of_lane[lane % SLAB]
    same = (slab[:, None] == slab[None, :]) & (grp[:, None] == grp[None, :])
    return jnp.asarray(np.where(same, 1.0 / sizes[grp][None, :], 0.0), dtype=BF16)


def _rope_tables(pos):
    inv = ROPE_BASE ** (-np.arange(0, A_ROPE, 2, dtype=np.float64) / A_ROPE)
    ang = np.asarray(pos, np.float64)[:, None] * inv[None, :]
    cos, sin = np.cos(ang), np.sin(ang)
    n = ang.shape[0]
    ones = np.ones((n, A_NOPE))
    zeros = np.zeros((n, A_NOPE))
    z16 = np.zeros((n, HALF_ROPE))
    pad1 = np.ones((n, SLAB - A_QK))
    pad0 = np.zeros((n, SLAB - A_QK))
    cos_t = np.concatenate([ones, cos, cos, pad1], axis=1)
    sin_lo = np.concatenate([zeros, -sin, z16, pad0], axis=1)
    sin_hi = np.concatenate([zeros, z16, sin, pad0], axis=1)
    return tuple(jnp.asarray(a, dtype=F32) for a in (cos_t, sin_lo, sin_hi))


def kernel(x_prompt, x_sample, cache_mla_latent, cache_mla_krope, cache_band_k, cache_band_v,
           c_prompt, c_sample, g_norm, w_ada, b_ada, w_in, g_q_lat, w_uq, g_kv_lat, w_uk, w_uv,
           g_qn_a, g_qr_a, g_kn_a, g_kr_a, g_q_b, g_k_b, rel_bias_b, w_oa, w_ob, w_out):
    bp, s, _ = x_prompt.shape
    bs, t, _ = x_sample.shape
    past = cache_mla_latent.shape[2]
    win_s = cache_band_k.shape[2]
    win_p = min(B_WINDOW, s)
    depth = g_norm.shape[0]
    assert depth == 1

    seg_q = _seg_matrix(np.where(np.arange(SLAB) < A_NOPE, 0, np.where(np.arange(SLAB) < A_QK, 1, 2)),
                        np.array([A_NOPE, A_ROPE, SLAB - A_QK], np.float64))
    seg_64 = _seg_matrix(np.arange(SLAB) // B_HDIM, np.array([B_HDIM, B_HDIM], np.float64))
    seg_k = _seg_matrix(np.where(np.arange(SLAB) < A_NOPE, 0, 1),
                        np.array([A_NOPE, SLAB - A_NOPE], np.float64))

    assert win_s == B_WINDOW and t <= BAND_TQ
    pos_s = past + np.arange(t)

    (g_norm, w_ada, b_ada, w_in, g_q_lat, w_uq, g_kv_lat, w_uk, w_uv, g_qn_a, g_qr_a, g_kn_a,
     g_kr_a, g_q_b, g_k_b, rel_bias_b, w_oa, w_ob, w_out) = [
        a.reshape(a.shape[1:]) for a in
        (g_norm, w_ada, b_ada, w_in, g_q_lat, w_uq, g_kv_lat, w_uk, w_uv, g_qn_a, g_qr_a, g_kn_a,
         g_kr_a, g_q_b, g_k_b, rel_bias_b, w_oa, w_ob, w_out)]
    zpad = jnp.zeros((D_MODEL, SLAB - A_QK), F32)
    z64 = jnp.zeros((D_MODEL, A_NOPE), F32)
    c_kr0 = A_Q_RANK + A_KV_RANK
    w_low = jnp.concatenate([w_in[:, :c_kr0], z64, w_in[:, c_kr0:c_kr0 + A_ROPE], zpad],
                            axis=1).astype(BF16)
    w_rest = w_in[:, c_kr0 + A_ROPE:].astype(BF16)
    wuq_p = jnp.pad(w_uq, ((0, 0), (0, 0), (0, SLAB - A_QK))).reshape(A_Q_RANK, -1).astype(BF16)
    wuk_p = jnp.pad(w_uk, ((0, 0), (0, 0), (0, SLAB - A_NOPE))).reshape(A_KV_RANK, -1).astype(BF16)
    wuv_b = w_uv.reshape(A_KV_RANK, -1).astype(BF16)
    wuv_t = wuv_b.T
    qscale = A_SCALE * LOG2E
    gq = jnp.tile(jnp.concatenate([g_qn_a, g_qr_a, jnp.zeros((SLAB - A_QK,), F32)]) * qscale,
                  A_HEADS)[None]
    gk = jnp.tile(jnp.concatenate([g_kn_a, jnp.zeros((SLAB - A_NOPE,), F32)]), A_HEADS)[None]
    gkr = jnp.concatenate([jnp.zeros((A_NOPE,), F32), g_kr_a, jnp.zeros((SLAB - A_QK,), F32)])[None]
    gqb = jnp.tile(g_q_b * (B_SCALE * LOG2E), B_HEADS)[None]
    gkb = jnp.tile(g_k_b, B_HEADS)[None]
    consts = (g_norm[None], w_low, w_rest, g_q_lat[None], wuq_p, g_kv_lat[None], gq, gkr,
              gqb, gkb, seg_q, seg_64)
    woa, wob, wout = w_oa.astype(BF16), w_ob.astype(BF16), w_out.astype(BF16)

    c_rows = jnp.concatenate([c_prompt, jnp.zeros((8 - bp, D_MODEL), F32),
                              jnp.repeat(c_sample, t, axis=0)], axis=0)
    mod = _adaln(c_rows, w_ada, b_ada)
    shift, scale, gate = mod[:, :D_MODEL], mod[:, D_MODEL:2 * D_MODEL], mod[:, 2 * D_MODEL:]

    tk_mla = 512
    xp2d = x_prompt.reshape(bp * s, D_MODEL)
    (qa, lat, kr, sga, qb, kb, vb, kbf, vbf, sgb, sma, smb, ka, vt) = _front(
        xp2d, scale[:bp, None], shift[:bp, None], _rope_tables(np.arange(s)),
        consts + (wuk_p, wuv_t, gk, seg_k),
        tm=tk_mla, rows_per_batch=s, per_row=False, tail_rows=win_p, v_transposed=True)
    attn_a = _mla_prompt(qa.reshape(bp, s, -1), ka.reshape(bp, s, -1), vt, tq=512, nsub=4)
    table_plain, table_band = _band_tables(rel_bias_b)
    attn_b = _band_prompt(qb.reshape(bp, s, -1), kb.reshape(bp, s, -1), vb.reshape(bp, s, -1),
                          table_band, tq=BAND_TQ, nsub=4)
    y_prompt = _back(xp2d, gate[:bp, None], attn_a.reshape(bp * s, -1), sga,
                     attn_b.reshape(bp * s, -1), sgb, sma, smb, woa, wob, wout,
                     tm=1024, rows_per_batch=s, per_row=False).reshape(bp, s, D_MODEL)

    xs2d = x_sample.reshape(bs * t, D_MODEL)
    rows_s = bs * t
    pos_tab = _rope_tables(np.tile(pos_s, bs))
    (qa2, lat2, kr2, sga2, qb2, kb2, vb2, kbf2, vbf2, sgb2, sma2, smb2, ka_n, va_n) = _front(
        xs2d, scale[8:], shift[8:], pos_tab, consts + (wuk_p, wuv_b, gk, seg_k),
        tm=rows_s, rows_per_batch=t, per_row=True, tail_rows=t, v_transposed=False)
    krs_cache = jnp.pad(cache_mla_krope.reshape(bs * past, A_ROPE),
                        ((0, 0), (A_NOPE, SLAB - A_QK))).astype(BF16)
    ka_c, va_c = _expand(cache_mla_latent.reshape(bs * past, A_KV_RANK), krs_cache,
                         wuk_p, wuv_b, gk, seg_k, tm=past // 2)
    attn_a2 = _mla_sample(qa2.reshape(bs, t, -1), ka_c.reshape(bs, past, -1), va_c.reshape(bs, past, -1),
                          ka_n.reshape(bs, t, -1), va_n.reshape(bs, t, -1))
    attn_b2 = _band_sample(qb2.reshape(bs, t, -1), cache_band_k.reshape(bs, win_s, B_WIDTH),
                           cache_band_v.reshape(bs, win_s, B_WIDTH), kb2.reshape(bs, t, -1),
                           vb2.reshape(bs, t, -1), table_plain)
    y_sample = _back(xs2d, gate[8:], attn_a2.reshape(rows_s, -1), sga2, attn_b2.reshape(rows_s, -1),
                     sgb2, sma2, smb2, woa, wob, wout,
                     tm=rows_s, rows_per_batch=t, per_row=True).reshape(bs, t, D_MODEL)

    return (y_prompt, y_sample,
            lat.reshape(1, bp, s, A_KV_RANK), kr.reshape(1, bp, s, A_ROPE),
            kbf.reshape(1, bp, win_p, B_HEADS, B_HDIM), vbf.reshape(1, bp, win_p, B_HEADS, B_HDIM),
            lat2.reshape(1, bs, t, A_KV_RANK), kr2.reshape(1, bs, t, A_ROPE),
            kbf2.reshape(1, bs, t, B_HEADS, B_HDIM), vbf2.reshape(1, bs, t, B_HEADS, B_HDIM))
```

```python
import functools
import math

import jax
import jax.numpy as jnp
import numpy as np
from jax import lax
from jax.experimental import pallas as pl
from jax.experimental.pallas import tpu as pltpu

F32 = jnp.float32
BF16 = jnp.bfloat16

D_MODEL = 1024
CHUNK = 64
A_HEADS = 8
A_NOPE = 64
A_ROPE = 32
A_VDIM = 64
A_QK = A_NOPE + A_ROPE
A_Q_RANK = 384
A_KV_RANK = 256
A_WIDTH = A_HEADS * A_VDIM
A_SCALE = A_QK ** -0.5
B_HEADS = 8
B_HDIM = 64
B_WIDTH = B_HEADS * B_HDIM
B_LEFT_CHUNKS = 8
B_WINDOW = B_LEFT_CHUNKS * CHUNK
B_MAX_REL = 256
B_SCALE = B_HDIM ** -0.5
ROPE_BASE = 10000.0
NORM_EPS = 1e-6
NEG_INF = -1e30
LOG2E = math.log2(math.e)
MAX_EXP2_DRIFT = 60.0

LANES = 128
SLAB = LANES
HALF_ROPE = A_ROPE // 2
ROPE_LANE0 = A_NOPE
VMEM_LIMIT = 56 * 1024 * 1024

FRONT_TM = 512
MLA_TQ = 512
MLA_NSUB = 4
BAND_NSUB = 4
BACK_TM = 1024

C_QLAT = 0
C_KVLAT = C_QLAT + A_Q_RANK
C_KR = C_KVLAT + A_KV_RANK
C_GA = C_KR + SLAB
C_QB = C_GA + A_WIDTH
C_KB = C_QB + B_WIDTH
C_VB = C_KB + B_WIDTH
C_GB = C_VB + B_WIDTH
C_MA = C_GB + B_WIDTH
C_MB = C_MA + D_MODEL
C_END = C_MB + D_MODEL

NT = (((1,), (1,)), ((), ()))


def _dot(a, b):
    return jnp.dot(a, b, preferred_element_type=F32)


def _dot_nt(a, b):
    return lax.dot_general(a, b, NT, preferred_element_type=F32)


def _rms_full(x, g):
    ms = jnp.mean(x * x, axis=-1, keepdims=True)
    return x * lax.rsqrt(ms + NORM_EPS) * g


def _rope_slab(x, cos_t, sin_lo, sin_hi):
    return (x * cos_t
            + pltpu.roll(x, SLAB - HALF_ROPE, axis=1) * sin_lo
            + pltpu.roll(x, HALF_ROPE, axis=1) * sin_hi)


def _adaln_kernel(c_ref, w_ref, b_ref, o_ref):
    c = c_ref[...]
    sc = c * jax.nn.sigmoid(c)
    o_ref[...] = jnp.dot(sc, w_ref[...], preferred_element_type=F32,
                         precision=lax.Precision.HIGHEST) + b_ref[...]


def _adaln(c_rows, w_ada, b_ada):
    n = c_rows.shape[0]
    tn = 1024
    return pl.pallas_call(
        _adaln_kernel,
        out_shape=jax.ShapeDtypeStruct((n, 3 * D_MODEL), F32),
        grid=(3 * D_MODEL // tn,),
        in_specs=[pl.BlockSpec((n, D_MODEL), lambda j: (0, 0)),
                  pl.BlockSpec((D_MODEL, tn), lambda j: (0, j)),
                  pl.BlockSpec((1, tn), lambda j: (0, j))],
        out_specs=pl.BlockSpec((n, tn), lambda j: (0, j)),
        compiler_params=pltpu.CompilerParams(dimension_semantics=("arbitrary",),
                                             vmem_limit_bytes=VMEM_LIMIT),
        name="adaln",
    )(c_rows, w_ada, b_ada.reshape(1, -1))


def _finish_keys(p, kn, ms, *, ka_ref, krs, gk_ref):
    pair = 2 * SLAB
    kn = kn * lax.rsqrt(ms + NORM_EPS) * gk_ref[:, p * pair:(p + 1) * pair]
    for s in range(2):
        ka_ref[:, p * pair + s * SLAB:p * pair + (s + 1) * SLAB] = (
            kn[:, s * SLAB:(s + 1) * SLAB] + krs).astype(BF16)


def _store_values(latb, wuv_ref, va_ref, v_transposed):
    if v_transposed:
        vt = _dot_nt(wuv_ref[...], latb).astype(BF16)
        for hp in range(A_HEADS // 2):
            va_ref[0, hp, 0] = vt[hp * SLAB:(hp + 1) * SLAB, :]
    else:
        va_ref[...] = _dot(latb, wuv_ref[...]).astype(BF16)


def _front_kernel(x_ref, scale_ref, shift_ref, cos_ref, slo_ref, shi_ref,
                  gnorm_ref, wlow_ref, win_ref, gql_ref, wuq_ref, gkv_ref, gq_ref, gkr_ref,
                  gqb_ref, gkb_ref, segq_ref, seg64_ref, wuk_ref, wuv_ref, gk_ref, segk_ref,
                  qa_ref, lat_ref, kr_ref, sga_ref, qb_ref, kb_ref, vb_ref,
                  kbf_ref, vbf_ref, sgb_ref, sma_ref, smb_ref, ka_ref, va_ref,
                  *, per_row, v_transposed):
    x = x_ref[...]
    if per_row:
        scale, shift = scale_ref[...], shift_ref[...]
    else:
        scale, shift = scale_ref[0], shift_ref[0]
    h = _rms_full(x, gnorm_ref[...]) * (1.0 + scale) + shift
    hb = h.astype(BF16)
    cos_t, sin_lo, sin_hi = cos_ref[...], slo_ref[...], shi_ref[...]

    def proj(c0, c1):
        if c1 <= C_GA:
            return _dot(hb, wlow_ref[:, c0:c1])
        return _dot(hb, win_ref[:, c0 - C_GA:c1 - C_GA])

    segq, seg64 = segq_ref[...], seg64_ref[...]
    pair = 2 * SLAB

    def finish_qa(p, qa, ms):
        qa = qa * lax.rsqrt(ms + NORM_EPS) * gq_ref[:, p * pair:(p + 1) * pair]
        for s in range(2):
            slab = _rope_slab(qa[:, s * SLAB:(s + 1) * SLAB], cos_t, sin_lo, sin_hi)
            qa_ref[:, p * pair + s * SLAB:p * pair + (s + 1) * SLAB] = slab.astype(BF16)

    def finish_qb(p, qb, ms):
        qb = qb * lax.rsqrt(ms + NORM_EPS) * gqb_ref[:, p * pair:(p + 1) * pair]
        qb_ref[:, p * pair:(p + 1) * pair] = qb.astype(BF16)

    def finish_kb(p, kb, ms):
        kb = kb * lax.rsqrt(ms + NORM_EPS) * gkb_ref[:, p * pair:(p + 1) * pair]
        kbf_ref[:, p * pair:(p + 1) * pair] = kb
        kb_ref[:, p * pair:(p + 1) * pair] = kb.astype(BF16)

    def seg_ms(x, seg):
        return _dot((x * x).astype(BF16), seg)

    low_rank = proj(C_QLAT, C_GA)
    ga = proj(C_GA, C_QB)
    sga_ref[...] = (ga * jax.nn.sigmoid(ga)).astype(BF16)

    lat = _rms_full(low_rank[:, C_KVLAT:C_KR], gkv_ref[...])
    lat_ref[...] = lat
    latb = lat.astype(BF16)
    krs = low_rank[:, C_KR:C_GA]
    ms = jnp.sum(krs * krs, axis=-1, keepdims=True) * (1.0 / A_ROPE)
    krs = _rope_slab(krs * lax.rsqrt(ms + NORM_EPS) * gkr_ref[...], cos_t, sin_lo, sin_hi)
    kr_ref[...] = pltpu.roll(krs, SLAB - ROPE_LANE0, axis=1)[:, :A_ROPE]

    _expand_keys = functools.partial(_finish_keys, ka_ref=ka_ref, krs=krs, gk_ref=gk_ref)
    segk = segk_ref[...]

    def kn_dot(p):
        return _dot(latb, wuk_ref[:, p * pair:(p + 1) * pair])

    r = _rms_full(low_rank[:, C_QLAT:C_KVLAT], gql_ref[...]).astype(BF16)
    qa = [None] * (A_HEADS // 2)
    kn = [None] * (A_HEADS // 2)
    qa[0] = _dot(r, wuq_ref[:, 0:pair])
    qb0 = proj(C_QB, C_QB + pair)
    qa[1] = _dot(r, wuq_ref[:, pair:2 * pair])
    finish_qa(0, qa[0], seg_ms(qa[0], segq))
    kb0 = proj(C_KB, C_KB + pair)
    qa[2] = _dot(r, wuq_ref[:, 2 * pair:3 * pair])
    finish_qa(1, qa[1], seg_ms(qa[1], segq))
    kn[0] = kn_dot(0)
    gb = proj(C_GB, C_MA)
    sgb_ref[...] = (gb * jax.nn.sigmoid(gb)).astype(BF16)
    qa[3] = _dot(r, wuq_ref[:, 3 * pair:4 * pair])
    finish_qa(2, qa[2], seg_ms(qa[2], segq))
    finish_qb(0, qb0, seg_ms(qb0, seg64))
    kn[1] = kn_dot(1)
    sma_ref[...] = jax.nn.sigmoid(proj(C_MA, C_MB)).astype(BF16)
    finish_qa(3, qa[3], seg_ms(qa[3], segq))
    finish_kb(0, kb0, seg_ms(kb0, seg64))
    _expand_keys(0, kn[0], seg_ms(kn[0], segk))
    qb1 = proj(C_QB + pair, C_QB + 2 * pair)
    kb1 = proj(C_KB + pair, C_KB + 2 * pair)
    kn[2] = kn_dot(2)
    smb_ref[...] = jax.nn.sigmoid(proj(C_MB, C_END)).astype(BF16)
    finish_qb(1, qb1, seg_ms(qb1, seg64))
    finish_kb(1, kb1, seg_ms(kb1, seg64))
    _expand_keys(1, kn[1], seg_ms(kn[1], segk))
    kn[3] = kn_dot(3)
    vb = proj(C_VB, C_GB)
    vbf_ref[...] = vb
    vb_ref[...] = vb.astype(BF16)
    _expand_keys(2, kn[2], seg_ms(kn[2], segk))
    _store_values(latb, wuv_ref, va_ref, v_transposed)
    _expand_keys(3, kn[3], seg_ms(kn[3], segk))


def _front(x2d, scale, shift, rope_tabs, consts, *, tm, rows_per_batch, per_row, tail_rows,
           v_transposed):
    rows = x2d.shape[0]
    nt = rows // tm
    row = lambda i: (i, 0)
    fixed = lambda i: (0, 0)
    tpb = rows_per_batch // tm
    if per_row:
        mod_spec = pl.BlockSpec((tm, D_MODEL), row)
        tab_spec = pl.BlockSpec((tm, SLAB), row)
    else:
        mod_spec = pl.BlockSpec((1, 1, D_MODEL), lambda i: (i // tpb, 0, 0))
        tab_spec = pl.BlockSpec((tm, SLAB), lambda i: (i % tpb, 0))

    def full(a):
        return pl.BlockSpec(a.shape, fixed)

    def out(width, dtype):
        return jax.ShapeDtypeStruct((rows, width), dtype), pl.BlockSpec((tm, width), row)

    if tail_rows == rows_per_batch:
        tail = out(B_WIDTH, F32)
    else:
        ntail = tail_rows // tm
        tail = (jax.ShapeDtypeStruct((rows // rows_per_batch * tail_rows, B_WIDTH), F32),
                pl.BlockSpec((tm, B_WIDTH),
                             lambda i: (i // tpb * ntail + jnp.maximum(i % tpb - (tpb - ntail), 0), 0)))

    if v_transposed:
        values = (jax.ShapeDtypeStruct((rows // rows_per_batch, A_HEADS // 2, tpb, SLAB, tm), BF16),
                  pl.BlockSpec((1, A_HEADS // 2, 1, SLAB, tm), lambda i: (i // tpb, 0, i % tpb, 0, 0)))
    else:
        values = out(A_WIDTH, BF16)

    outs = [out(A_HEADS * SLAB, BF16),
            out(A_KV_RANK, F32),
            out(A_ROPE, F32),
            out(A_WIDTH, BF16),
            out(B_WIDTH, BF16),
            out(B_WIDTH, BF16),
            out(B_WIDTH, BF16),
            tail,
            tail,
            out(B_WIDTH, BF16),
            out(D_MODEL, BF16),
            out(D_MODEL, BF16),
            out(A_HEADS * SLAB, BF16),
            values]
    return pl.pallas_call(
        functools.partial(_front_kernel, per_row=per_row, v_transposed=v_transposed),
        out_shape=[o[0] for o in outs],
        grid=(nt,),
        in_specs=[pl.BlockSpec((tm, D_MODEL), row), mod_spec, mod_spec,
                  tab_spec, tab_spec, tab_spec] + [full(a) for a in consts],
        out_specs=[o[1] for o in outs],
        compiler_params=pltpu.CompilerParams(dimension_semantics=("arbitrary",),
                                             vmem_limit_bytes=VMEM_LIMIT),
        name="front",
    )(x2d, scale, shift, *rope_tabs, *consts)


def _expand_kernel(lat_ref, krs_ref, wuk_ref, wuv_ref, gk_ref, segk_ref, ka_ref, va_ref):
    latb = lat_ref[...].astype(BF16)
    krs = krs_ref[...].astype(F32)
    segk = segk_ref[...]
    pair = 2 * SLAB
    for p in range(A_HEADS // 2):
        kn = _dot(latb, wuk_ref[:, p * pair:(p + 1) * pair])
        _finish_keys(p, kn, _dot((kn * kn).astype(BF16), segk), ka_ref=ka_ref, krs=krs, gk_ref=gk_ref)
    _store_values(latb, wuv_ref, va_ref, False)


def _expand(lat2d, krs2d, wuk_p, wuv, gk, segk, *, tm):
    rows = lat2d.shape[0]
    row = lambda i: (i, 0)
    fixed = lambda i: (0, 0)
    return pl.pallas_call(
        _expand_kernel,
        out_shape=[jax.ShapeDtypeStruct((rows, A_HEADS * SLAB), BF16),
                   jax.ShapeDtypeStruct((rows, A_WIDTH), BF16)],
        grid=(rows // tm,),
        in_specs=[pl.BlockSpec((tm, A_KV_RANK), row), pl.BlockSpec((tm, SLAB), row),
                  pl.BlockSpec(wuk_p.shape, fixed), pl.BlockSpec(wuv.shape, fixed),
                  pl.BlockSpec(gk.shape, fixed), pl.BlockSpec(segk.shape, fixed)],
        out_specs=[pl.BlockSpec((tm, A_HEADS * SLAB), row), pl.BlockSpec((tm, A_WIDTH), row)],
        compiler_params=pltpu.CompilerParams(dimension_semantics=("arbitrary",),
                                             vmem_limit_bytes=VMEM_LIMIT),
        name="expand",
    )(lat2d, krs2d, wuk_p, wuv, gk, segk)


def _select_heads(o_even, o_odd):
    lane = lax.broadcasted_iota(jnp.int32, o_even.shape, 1)
    return jnp.where(lane < A_VDIM, o_even, o_odd)


def _mla_prompt_kernel(q_ref, k_ref, vt_ref, o_ref, *, tq, nsub):
    qi = pl.program_id(2)
    tk = vt_ref.shape[-1]
    ratio = tq // tk
    n_full = nsub * ratio * qi
    chunk_delta = (lax.broadcasted_iota(jnp.int32, (tk, tq), 0) // CHUNK
                   - lax.broadcasted_iota(jnp.int32, (tk, tq), 1) // CHUNK)
    ones = jnp.ones((16, tk), BF16)
    chains = [(e, h) for h in range(nsub) for e in range(2)]

    def step(j, carry, work, ref):
        k0 = pl.multiple_of(j * tk, tk)
        scores = {}
        new = list(carry)

        def qk(c):
            e, h = chains[c]
            k = k_ref[0, pl.ds(k0, tk), e * SLAB:(e + 1) * SLAB]
            q = q_ref[0, h * tq:(h + 1) * tq, e * SLAB:(e + 1) * SLAB]
            scores[c] = _dot_nt(k, q)

        def update(c, max_delta):
            e, h = chains[c]
            m, acc, drift = carry[c]
            s = scores.pop(c)
            if max_delta is not None:
                s = jnp.where(chunk_delta <= max_delta, s, NEG_INF)
            vt1 = jnp.concatenate([vt_ref[0, 0, j, e * A_VDIM:(e + 1) * A_VDIM, :], ones], axis=0)
            bmax = jnp.max(s, axis=0, keepdims=True)
            m_new = jnp.maximum(m, bmax)
            if ref == "exact":
                p = jnp.exp2(s - m_new).astype(BF16)
                new[c] = (m_new, jnp.exp2(m - m_new) * acc + _dot(vt1, p), drift)
            else:
                m_ref = m
                if ref == "probe":
                    m_ref = jnp.maximum(m, jnp.max(s[:CHUNK], axis=0, keepdims=True))
                    acc = jnp.exp2(m - m_ref) * acc
                p = jnp.exp2(s - m_ref).astype(BF16)
                new[c] = (m_new, jnp.exp2(m_ref - m_new) * (acc + _dot(vt1, p)),
                          jnp.maximum(drift, bmax - m_ref))

        for c, _ in work:
            qk(c)
        for c, max_delta in work:
            update(c, max_delta)
        return tuple(new)

    everyone = [(c, None) for c in range(len(chains))]
    diagonal = []
    for d in range(nsub * ratio):
        work = []
        for c, (e, h) in enumerate(chains):
            if d * tk >= (h + 1) * tq:
                continue
            before = (d + 1) * tk <= h * tq
            work.append((c, None if before else (h * tq - d * tk) // CHUNK))
        diagonal.append(work)

    def attend(streaming):
        carry = tuple((jnp.full((1, tq), NEG_INF, F32), jnp.zeros((A_VDIM + 16, tq), F32),
                       jnp.full((1, tq), NEG_INF, F32)) for _ in chains)
        if streaming:
            first = jnp.minimum(n_full, 1)
            carry = lax.fori_loop(0, first, lambda j, cr: step(j, cr, everyone, "probe"), carry)
            carry = lax.fori_loop(first, n_full, lambda j, cr: step(j, cr, everyone, "lagged"), carry)
        else:
            carry = lax.fori_loop(0, n_full, lambda j, cr: step(j, cr, everyone, "exact"), carry)
        for d, work in enumerate(diagonal):
            carry = step(n_full + d, carry, work,
                         "exact" if not streaming else ("probe" if d == 0 else "lagged"))
        for h in range(nsub):
            outs = [carry[2 * h + e][1] for e in range(2)]
            outs = [acc[:A_VDIM] / acc[A_VDIM:A_VDIM + 1] for acc in outs]
            o_ref[0, h * tq:(h + 1) * tq, :] = jnp.concatenate(outs, axis=0).T.astype(o_ref.dtype)
        return functools.reduce(jnp.maximum, [jnp.max(cr[2]) for cr in carry])

    worst_drift = attend(True)

    @pl.when(worst_drift > MAX_EXP2_DRIFT)
    def _():
        attend(False)


def _mla_prompt(qa, ka, vt, *, tq, nsub):
    b, s, _ = qa.shape
    tk = vt.shape[-1]
    assert vt.shape == (b, A_HEADS // 2, s // tk, SLAB, tk) and tq % tk == 0
    tqq = nsub * tq
    return pl.pallas_call(
        functools.partial(_mla_prompt_kernel, tq=tq, nsub=nsub),
        out_shape=jax.ShapeDtypeStruct((b, s, A_WIDTH), BF16),
        grid=(b, A_HEADS // 2, s // tqq),
        in_specs=[pl.BlockSpec((1, tqq, 2 * SLAB), lambda bi, hp, qi: (bi, qi, hp)),
                  pl.BlockSpec((1, s, 2 * SLAB), lambda bi, hp, qi: (bi, 0, hp)),
                  pl.BlockSpec((1, 1, s // tk, SLAB, tk), lambda bi, hp, qi: (bi, hp, 0, 0, 0))],
        out_specs=pl.BlockSpec((1, tqq, SLAB), lambda bi, hp, qi: (bi, qi, hp)),
        compiler_params=pltpu.CompilerParams(
            dimension_semantics=("arbitrary", "arbitrary", "arbitrary"),
            vmem_limit_bytes=VMEM_LIMIT),
        name="mla_prompt",
    )(qa, ka, vt)


def _softmax_over_parts(scores, values):
    m = functools.reduce(jnp.maximum, [jnp.max(s, axis=-1, keepdims=True) for s in scores])
    ps = [jnp.exp2(s - m) for s in scores]
    l = sum(jnp.sum(p, axis=-1, keepdims=True) for p in ps)
    return sum(_dot(p.astype(BF16), v) for p, v in zip(ps, values)) / l


def _mla_sample_kernel(q_ref, kc_ref, vc_ref, kn_ref, vn_ref, o_ref):
    for hp in range(A_HEADS // 2):
        values = [v_ref[0, :, hp * SLAB:(hp + 1) * SLAB] for v_ref in (vc_ref, vn_ref)]
        outs = []
        for e in range(2):
            lanes = slice((2 * hp + e) * SLAB, (2 * hp + e + 1) * SLAB)
            q = q_ref[0, :, lanes]
            scores = [_dot_nt(q, k_ref[0, :, lanes]) for k_ref in (kc_ref, kn_ref)]
            outs.append(_softmax_over_parts(scores, values))
        o_ref[0, :, hp * SLAB:(hp + 1) * SLAB] = _select_heads(outs[0], outs[1]).astype(o_ref.dtype)


def _mla_sample(qa, ka_cache, va_cache, ka_new, va_new):
    b, t, _ = qa.shape
    n = ka_cache.shape[1]
    spec = lambda rows, width: pl.BlockSpec((1, rows, width), lambda bi: (bi, 0, 0))
    return pl.pallas_call(
        _mla_sample_kernel,
        out_shape=jax.ShapeDtypeStruct((b, t, A_WIDTH), BF16),
        grid=(b,),
        in_specs=[spec(t, A_HEADS * SLAB), spec(n, A_HEADS * SLAB), spec(n, A_WIDTH),
                  spec(t, A_HEADS * SLAB), spec(t, A_WIDTH)],
        out_specs=spec(t, A_WIDTH),
        compiler_params=pltpu.CompilerParams(dimension_semantics=("arbitrary",),
                                             vmem_limit_bytes=VMEM_LIMIT),
        name="mla_sample",
    )(qa, ka_cache, va_cache, ka_new, va_new)


BAND_TQ = 256
BAND_KEYS = B_WINDOW + BAND_TQ
TOEPLITZ_W = 1024


def _band_table_kernel(rb_ref, onehot_ref, allowed_ref, tb_ref, tp_ref):
    g = jnp.dot(rb_ref[...], onehot_ref[...], preferred_element_type=F32,
                precision=lax.Precision.HIGHEST) * LOG2E
    allowed = allowed_ref[...] > 0.0
    for h in range(B_HEADS):
        row = jnp.broadcast_to(g[h:h + 1, :], (BAND_TQ, TOEPLITZ_W))
        t = pltpu.roll(row, 0, axis=1, stride=1, stride_axis=0)[:, :BAND_KEYS]
        tb_ref[h] = t
        tp_ref[h] = jnp.where(allowed, t, NEG_INF)


def _band_tables(rel_bias):
    n_rel = rel_bias.shape[1]
    n_pad = -(-n_rel // LANES) * LANES
    x = np.arange(TOEPLITZ_W)
    key_minus_query = np.where(x < BAND_KEYS, x, x - TOEPLITZ_W)
    dist = B_WINDOW - key_minus_query
    idx = np.clip(dist, -B_MAX_REL, B_MAX_REL) + B_MAX_REL
    onehot = np.zeros((n_pad, TOEPLITZ_W), np.float32)
    onehot[idx, x] = 1.0
    q_chunk = np.arange(BAND_TQ) // CHUNK + B_LEFT_CHUNKS
    k_chunk = np.arange(BAND_KEYS) // CHUNK
    allowed = (k_chunk[None, :] <= q_chunk[:, None]) & (k_chunk[None, :] >= q_chunk[:, None] - B_LEFT_CHUNKS)
    rb = jnp.pad(rel_bias, ((0, 0), (0, n_pad - n_rel)))
    shape = jax.ShapeDtypeStruct((B_HEADS, BAND_TQ, BAND_KEYS), F32)
    return pl.pallas_call(
        _band_table_kernel,
        out_shape=[shape, shape],
        compiler_params=pltpu.CompilerParams(vmem_limit_bytes=VMEM_LIMIT),
        name="band_table",
    )(rb, jnp.asarray(onehot), jnp.asarray(allowed.astype(np.float32)))


def _head_lane_masks(width):
    lane = lax.broadcasted_iota(jnp.int32, (1, width), 1)
    return [(lane < B_HDIM), (lane >= B_HDIM)]


def _band_prompt_kernel(q_ref, kp_ref, kc_ref, vp_ref, vc_ref, t_ref, o_ref, *, tq, nsub):
    qi = pl.program_id(2)
    nprev = B_WINDOW // tq
    nkb = nprev + 1

    def window(prev_ref, cur_ref, w):
        if w < nprev:
            return prev_ref[0, w * tq:(w + 1) * tq, :]
        return cur_ref[0, (w - nprev) * tq:(w - nprev + 1) * tq, :]

    masks = _head_lane_masks(SLAB)
    units = [(t, e) for t in range(nsub) for e in range(2)]
    scores = {}
    for t, e in units:
        q = q_ref[0, t * tq:(t + 1) * tq, :]
        qm = jnp.where(masks[e], q, jnp.zeros_like(q))
        ss = []
        for j in range(nkb):
            s = _dot_nt(qm, window(kp_ref, kc_ref, t + j)) + t_ref[e, :, j * tq:(j + 1) * tq]
            if t + j < nprev:
                s = jnp.where(qi > 0, s, NEG_INF)
            ss.append(s)
        scores[t, e] = ss
    outs = {}
    for t, e in units:
        ss = scores.pop((t, e))
        m = functools.reduce(jnp.maximum, [jnp.max(s, axis=-1, keepdims=True) for s in ss])
        ps = [jnp.exp2(s - m) for s in ss]
        l = sum(jnp.sum(p, axis=-1, keepdims=True) for p in ps)
        o = sum(_dot(p.astype(BF16), window(vp_ref, vc_ref, t + j)) for j, p in enumerate(ps))
        outs[t, e] = o / l
    for t in range(nsub):
        o_ref[0, t * tq:(t + 1) * tq, :] = _select_heads(outs[t, 0], outs[t, 1]).astype(o_ref.dtype)


def _band_prompt(qb, kb, vb, table, *, tq, nsub):
    b, s, _ = qb.shape
    tqq = nsub * tq
    per_step = tqq // B_WINDOW
    assert tqq % B_WINDOW == 0 and B_WINDOW % tq == 0
    cur = pl.BlockSpec((1, tqq, SLAB), lambda hp, bi, qi: (bi, qi, hp))
    prev = pl.BlockSpec((1, B_WINDOW, SLAB),
                        lambda hp, bi, qi: (bi, jnp.maximum(qi * per_step - 1, 0), hp))
    return pl.pallas_call(
        functools.partial(_band_prompt_kernel, tq=tq, nsub=nsub),
        out_shape=jax.ShapeDtypeStruct((b, s, B_WIDTH), BF16),
        grid=(B_HEADS // 2, b, s // tqq),
        in_specs=[cur, prev, cur, prev, cur,
                  pl.BlockSpec((2, tq, B_WINDOW + tq), lambda hp, bi, qi: (hp, 0, 0))],
        out_specs=cur,
        compiler_params=pltpu.CompilerParams(
            dimension_semantics=("arbitrary", "arbitrary", "arbitrary"),
            vmem_limit_bytes=VMEM_LIMIT),
        name="band_prompt",
    )(qb, kb, kb, vb, vb, table)


def _band_sample_kernel(q_ref, kc_ref, vc_ref, kn_ref, vn_ref, t_ref, o_ref):
    masks = _head_lane_masks(SLAB)
    for hp in range(B_HEADS // 2):
        lanes = slice(hp * SLAB, (hp + 1) * SLAB)
        q = q_ref[0, :, lanes]
        keys = [kc_ref[0, :, lanes].astype(BF16), kn_ref[0, :, lanes]]
        values = [vc_ref[0, :, lanes].astype(BF16), vn_ref[0, :, lanes]]
        outs = []
        for e in range(2):
            qm = jnp.where(masks[e], q, jnp.zeros_like(q))
            scores, col = [], 0
            for k in keys:
                scores.append(_dot_nt(qm, k) + t_ref[2 * hp + e, :, col:col + k.shape[0]])
                col += k.shape[0]
            outs.append(_softmax_over_parts(scores, values))
        o_ref[0, :, lanes] = _select_heads(outs[0], outs[1]).astype(o_ref.dtype)


def _band_sample(qb, kb_cache, vb_cache, kb_new, vb_new, table):
    b, t, _ = qb.shape
    n = kb_cache.shape[1]
    spec = lambda rows: pl.BlockSpec((1, rows, B_WIDTH), lambda bi: (bi, 0, 0))
    return pl.pallas_call(
        _band_sample_kernel,
        out_shape=jax.ShapeDtypeStruct((b, t, B_WIDTH), BF16),
        grid=(b,),
        in_specs=[spec(t), spec(n), spec(n), spec(t), spec(t),
                  pl.BlockSpec((B_HEADS, t, BAND_KEYS), lambda bi: (0, 0, 0))],
        out_specs=spec(t),
        compiler_params=pltpu.CompilerParams(dimension_semantics=("arbitrary",),
                                             vmem_limit_bytes=VMEM_LIMIT),
        name="band_sample",
    )(qb, kb_cache, vb_cache, kb_new, vb_new, table)


def _back_kernel(x_ref, gate_ref, aa_ref, sga_ref, ab_ref, sgb_ref, sma_ref, smb_ref,
                 woa_ref, wob_ref, wout_ref, o_ref, *, per_row):
    gate = gate_ref[...] if per_row else gate_ref[0]
    ua = _dot(aa_ref[...] * sga_ref[...], woa_ref[...])
    ub = _dot(ab_ref[...] * sgb_ref[...], wob_ref[...])
    merged = sma_ref[...].astype(F32) * ua + smb_ref[...].astype(F32) * ub
    o_ref[...] = x_ref[...] + gate * _dot(merged.astype(BF16), wout_ref[...])


def _back(x2d, gate, aa, sga, ab, sgb, sma, smb, woa, wob, wout, *, tm, rows_per_batch, per_row):
    rows = x2d.shape[0]
    row = lambda i: (i, 0)
    fixed = lambda i: (0, 0)
    if per_row:
        gate_spec = pl.BlockSpec((tm, D_MODEL), row)
    else:
        tpb = rows_per_batch // tm
        gate_spec = pl.BlockSpec((1, 1, D_MODEL), lambda i: (i // tpb, 0, 0))
    return pl.pallas_call(
        functools.partial(_back_kernel, per_row=per_row),
        out_shape=jax.ShapeDtypeStruct((rows, D_MODEL), F32),
        grid=(rows // tm,),
        in_specs=[pl.BlockSpec((tm, D_MODEL), row), gate_spec,
                  pl.BlockSpec((tm, A_WIDTH), row), pl.BlockSpec((tm, A_WIDTH), row),
                  pl.BlockSpec((tm, B_WIDTH), row), pl.BlockSpec((tm, B_WIDTH), row),
                  pl.BlockSpec((tm, D_MODEL), row), pl.BlockSpec((tm, D_MODEL), row),
                  pl.BlockSpec(woa.shape, fixed), pl.BlockSpec(wob.shape, fixed),
                  pl.BlockSpec(wout.shape, fixed)],
        out_specs=pl.BlockSpec((tm, D_MODEL), row),
        compiler_params=pltpu.CompilerParams(dimension_semantics=("arbitrary",),
                                             vmem_limit_bytes=VMEM_LIMIT),
        name="back",
    )(x2d, gate, aa, sga, ab, sgb, sma, smb, woa, wob, wout)


def _seg_matrix(group_of_lane, sizes):
    lane = np.arange(2 * SLAB)
    slab = lane // SLAB
    grp = group_of_lane[lane % SLAB]
    same = (slab[:, None] == slab[None, :]) & (grp[:, None] == grp[None, :])
    return jnp.asarray(np.where(same, 1.0 / sizes[grp][None, :], 0.0), dtype=BF16)


def _rope_tables(pos):
    inv = ROPE_BASE ** (-np.arange(0, A_ROPE, 2, dtype=np.float64) / A_ROPE)
    ang = np.asarray(pos, np.float64)[:, None] * inv[None, :]
    cos, sin = np.cos(ang), np.sin(ang)
    n = ang.shape[0]
    ones = np.ones((n, A_NOPE))
    zeros = np.zeros((n, A_NOPE))
    z16 = np.zeros((n, HALF_ROPE))
    pad1 = np.ones((n, SLAB - A_QK))
    pad0 = np.zeros((n, SLAB - A_QK))
    cos_t = np.concatenate([ones, cos, cos, pad1], axis=1)
    sin_lo = np.concatenate([zeros, -sin, z16, pad0], axis=1)
    sin_hi = np.concatenate([zeros, z16, sin, pad0], axis=1)
    return tuple(jnp.asarray(a, dtype=F32) for a in (cos_t, sin_lo, sin_hi))


def kernel(x_prompt, x_sample, cache_mla_latent, cache_mla_krope, cache_band_k, cache_band_v,
           c_prompt, c_sample, g_norm, w_ada, b_ada, w_in, g_q_lat, w_uq, g_kv_lat, w_uk, w_uv,
           g_qn_a, g_qr_a, g_kn_a, g_kr_a, g_q_b, g_k_b, rel_bias_b, w_oa, w_ob, w_out):
    bp, s, _ = x_prompt.shape
    bs, t, _ = x_sample.shape
    past = cache_mla_latent.shape[2]
    win_s = cache_band_k.shape[2]
    win_p = min(B_WINDOW, s)
    depth = g_norm.shape[0]
    assert depth == 1

    seg_q = _seg_matrix(np.where(np.arange(SLAB) < A_NOPE, 0, np.where(np.arange(SLAB) < A_QK, 1, 2)),
                        np.array([A_NOPE, A_ROPE, SLAB - A_QK], np.float64))
    seg_64 = _seg_matrix(np.arange(SLAB) // B_HDIM, np.array([B_HDIM, B_HDIM], np.float64))
    seg_k = _seg_matrix(np.where(np.arange(SLAB) < A_NOPE, 0, 1),
                        np.array([A_NOPE, SLAB - A_NOPE], np.float64))

    assert win_s == B_WINDOW and t <= BAND_TQ
    pos_s = past + np.arange(t)

    (g_norm, w_ada, b_ada, w_in, g_q_lat, w_uq, g_kv_lat, w_uk, w_uv, g_qn_a, g_qr_a, g_kn_a,
     g_kr_a, g_q_b, g_k_b, rel_bias_b, w_oa, w_ob, w_out) = [
        a.reshape(a.shape[1:]) for a in
        (g_norm, w_ada, b_ada, w_in, g_q_lat, w_uq, g_kv_lat, w_uk, w_uv, g_qn_a, g_qr_a, g_kn_a,
         g_kr_a, g_q_b, g_k_b, rel_bias_b, w_oa, w_ob, w_out)]
    zpad = jnp.zeros((D_MODEL, SLAB - A_QK), F32)
    z64 = jnp.zeros((D_MODEL, A_NOPE), F32)
    c_kr0 = A_Q_RANK + A_KV_RANK
    w_low = jnp.concatenate([w_in[:, :c_kr0], z64, w_in[:, c_kr0:c_kr0 + A_ROPE], zpad],
                            axis=1).astype(BF16)
    w_rest = w_in[:, c_kr0 + A_ROPE:].astype(BF16)
    wuq_p = jnp.pad(w_uq, ((0, 0), (0, 0), (0, SLAB - A_QK))).reshape(A_Q_RANK, -1).astype(BF16)
    wuk_p = jnp.pad(w_uk, ((0, 0), (0, 0), (0, SLAB - A_NOPE))).reshape(A_KV_RANK, -1).astype(BF16)
    wuv_b = w_uv.reshape(A_KV_RANK, -1).astype(BF16)
    wuv_t = wuv_b.T
    qscale = A_SCALE * LOG2E
    gq = jnp.tile(jnp.concatenate([g_qn_a, g_qr_a, jnp.zeros((SLAB - A_QK,), F32)]) * qscale,
                  A_HEADS)[None]
    gk = jnp.tile(jnp.concatenate([g_kn_a, jnp.zeros((SLAB - A_NOPE,), F32)]), A_HEADS)[None]
    gkr = jnp.concatenate([jnp.zeros((A_NOPE,), F32), g_kr_a, jnp.zeros((SLAB - A_QK,), F32)])[None]
    gqb = jnp.tile(g_q_b * (B_SCALE * LOG2E), B_HEADS)[None]
    gkb = jnp.tile(g_k_b, B_HEADS)[None]
    consts = (g_norm[None], w_low, w_rest, g_q_lat[None], wuq_p, g_kv_lat[None], gq, gkr,
              gqb, gkb, seg_q, seg_64)
    woa, wob, wout = w_oa.astype(BF16), w_ob.astype(BF16), w_out.astype(BF16)

    c_rows = jnp.concatenate([c_prompt, jnp.zeros((8 - bp, D_MODEL), F32),
                              jnp.repeat(c_sample, t, axis=0)], axis=0)
    mod = _adaln(c_rows, w_ada, b_ada)
    shift, scale, gate = mod[:, :D_MODEL], mod[:, D_MODEL:2 * D_MODEL], mod[:, 2 * D_MODEL:]

    xp2d = x_prompt.reshape(bp * s, D_MODEL)
    (qa, lat, kr, sga, qb, kb, vb, kbf, vbf, sgb, sma, smb, ka, vt) = _front(
        xp2d, scale[:bp, None], shift[:bp, None], _rope_tables(np.arange(s)),
        consts + (wuk_p, wuv_t, gk, seg_k),
        tm=FRONT_TM, rows_per_batch=s, per_row=False, tail_rows=win_p, v_transposed=True)
    attn_a = _mla_prompt(qa.reshape(bp, s, -1), ka.reshape(bp, s, -1), vt, tq=MLA_TQ, nsub=MLA_NSUB)
    table_plain, table_band = _band_tables(rel_bias_b)
    attn_b = _band_prompt(qb.reshape(bp, s, -1), kb.reshape(bp, s, -1), vb.reshape(bp, s, -1),
                          table_band, tq=BAND_TQ, nsub=BAND_NSUB)
    y_prompt = _back(xp2d, gate[:bp, None], attn_a.reshape(bp * s, -1), sga,
                     attn_b.reshape(bp * s, -1), sgb, sma, smb, woa, wob, wout,
                     tm=BACK_TM, rows_per_batch=s, per_row=False).reshape(bp, s, D_MODEL)

    xs2d = x_sample.reshape(bs * t, D_MODEL)
    rows_s = bs * t
    pos_tab = _rope_tables(np.tile(pos_s, bs))
    (qa2, lat2, kr2, sga2, qb2, kb2, vb2, kbf2, vbf2, sgb2, sma2, smb2, ka_n, va_n) = _front(
        xs2d, scale[8:], shift[8:], pos_tab, consts + (wuk_p, wuv_b, gk, seg_k),
        tm=rows_s, rows_per_batch=t, per_row=True, tail_rows=t, v_transposed=False)
    krs_cache = jnp.pad(cache_mla_krope.reshape(bs * past, A_ROPE),
                        ((0, 0), (A_NOPE, SLAB - A_QK))).astype(BF16)
    ka_c, va_c = _expand(cache_mla_latent.reshape(bs * past, A_KV_RANK), krs_cache,
                         wuk_p, wuv_b, gk, seg_k, tm=past // 2)
    attn_a2 = _mla_sample(qa2.reshape(bs, t, -1), ka_c.reshape(bs, past, -1), va_c.reshape(bs, past, -1),
                          ka_n.reshape(bs, t, -1), va_n.reshape(bs, t, -1))
    attn_b2 = _band_sample(qb2.reshape(bs, t, -1), cache_band_k.reshape(bs, win_s, B_WIDTH),
                           cache_band_v.reshape(bs, win_s, B_WIDTH), kb2.reshape(bs, t, -1),
                           vb2.reshape(bs, t, -1), table_plain)
    y_sample = _back(xs2d, gate[8:], attn_a2.reshape(rows_s, -1), sga2, attn_b2.reshape(rows_s, -1),
                     sgb2, sma2, smb2, woa, wob, wout,
                     tm=rows_s, rows_per_batch=t, per_row=True).reshape(bs, t, D_MODEL)

    return (y_prompt, y_sample,
            lat.reshape(1, bp, s, A_KV_RANK), kr.reshape(1, bp, s, A_ROPE),
            kbf.reshape(1, bp, win_p, B_HEADS, B_HDIM), vbf.reshape(1, bp, win_p, B_HEADS, B_HDIM),
            lat2.reshape(1, bs, t, A_KV_RANK), kr2.reshape(1, bs, t, A_ROPE),
            kbf2.reshape(1, bs, t, B_HEADS, B_HDIM), vbf2.reshape(1, bs, t, B_HEADS, B_HDIM))
```

```python
import functools
import math

import jax
import jax.numpy as jnp
import numpy as np
from jax import lax
from jax.experimental import pallas as pl
from jax.experimental.pallas import tpu as pltpu

F32 = jnp.float32
BF16 = jnp.bfloat16

D_MODEL = 1024
CHUNK = 64
A_HEADS = 8
A_NOPE = 64
A_ROPE = 32
A_VDIM = 64
A_QK = A_NOPE + A_ROPE
A_Q_RANK = 384
A_KV_RANK = 256
A_WIDTH = A_HEADS * A_VDIM
A_SCALE = A_QK ** -0.5
B_HEADS = 8
B_HDIM = 64
B_WIDTH = B_HEADS * B_HDIM
B_LEFT_CHUNKS = 8
B_WINDOW = B_LEFT_CHUNKS * CHUNK
B_MAX_REL = 256
B_SCALE = B_HDIM ** -0.5
ROPE_BASE = 10000.0
NORM_EPS = 1e-6
NEG_INF = -1e30
LOG2E = math.log2(math.e)
MAX_EXP2_DRIFT = 60.0

LANES = 128
SLAB = LANES
HALF_ROPE = A_ROPE // 2
ROPE_LANE0 = A_NOPE
VMEM_LIMIT = 56 * 1024 * 1024

FRONT_TM = 512
MLA_TQ = 512
MLA_NSUB = 4
BAND_NSUB = 4
BACK_TM = 1024

C_QLAT = 0
C_KVLAT = C_QLAT + A_Q_RANK
C_KR = C_KVLAT + A_KV_RANK
C_GA = C_KR + SLAB
C_QB = C_GA + A_WIDTH
C_KB = C_QB + B_WIDTH
C_VB = C_KB + B_WIDTH
C_GB = C_VB + B_WIDTH
C_MA = C_GB + B_WIDTH
C_MB = C_MA + D_MODEL
C_END = C_MB + D_MODEL

NT = (((1,), (1,)), ((), ()))


def _dot(a, b):
    return jnp.dot(a, b, preferred_element_type=F32)


def _dot_nt(a, b):
    return lax.dot_general(a, b, NT, preferred_element_type=F32)


def _rms_full(x, g):
    ms = jnp.mean(x * x, axis=-1, keepdims=True)
    return x * lax.rsqrt(ms + NORM_EPS) * g


def _rope_slab(x, cos_t, sin_lo, sin_hi):
    return (x * cos_t
            + pltpu.roll(x, SLAB - HALF_ROPE, axis=1) * sin_lo
            + pltpu.roll(x, HALF_ROPE, axis=1) * sin_hi)


def _adaln_kernel(c_ref, w_ref, b_ref, o_ref):
    c = c_ref[...]
    sc = c * jax.nn.sigmoid(c)
    o_ref[...] = jnp.dot(sc, w_ref[...], preferred_element_type=F32,
                         precision=lax.Precision.HIGHEST) + b_ref[...]


def _adaln(c_rows, w_ada, b_ada):
    n = c_rows.shape[0]
    tn = 1024
    return pl.pallas_call(
        _adaln_kernel,
        out_shape=jax.ShapeDtypeStruct((n, 3 * D_MODEL), F32),
        grid=(3 * D_MODEL // tn,),
        in_specs=[pl.BlockSpec((n, D_MODEL), lambda j: (0, 0)),
                  pl.BlockSpec((D_MODEL, tn), lambda j: (0, j)),
                  pl.BlockSpec((1, tn), lambda j: (0, j))],
        out_specs=pl.BlockSpec((n, tn), lambda j: (0, j)),
        compiler_params=pltpu.CompilerParams(dimension_semantics=("arbitrary",),
                                             vmem_limit_bytes=VMEM_LIMIT),
        name="adaln",
    )(c_rows, w_ada, b_ada.reshape(1, -1))


def _finish_keys(p, kn, ms, *, ka_ref, krs, gk_ref):
    pair = 2 * SLAB
    kn = kn * lax.rsqrt(ms + NORM_EPS) * gk_ref[:, p * pair:(p + 1) * pair]
    for s in range(2):
        ka_ref[:, p * pair + s * SLAB:p * pair + (s + 1) * SLAB] = (
            kn[:, s * SLAB:(s + 1) * SLAB] + krs).astype(BF16)


def _store_values(latb, wuv_ref, va_ref, v_transposed):
    if v_transposed:
        vt = _dot_nt(wuv_ref[...], latb).astype(BF16)
        for hp in range(A_HEADS // 2):
            va_ref[0, hp, 0] = vt[hp * SLAB:(hp + 1) * SLAB, :]
    else:
        va_ref[...] = _dot(latb, wuv_ref[...]).astype(BF16)


def _front_kernel(x_ref, scale_ref, shift_ref, cos_ref, slo_ref, shi_ref,
                  gnorm_ref, wlow_ref, win_ref, gql_ref, wuq_ref, gkv_ref, gq_ref, gkr_ref,
                  gqb_ref, gkb_ref, segq_ref, seg64_ref, wuk_ref, wuv_ref, gk_ref, segk_ref,
                  qa_ref, lat_ref, kr_ref, sga_ref, qb_ref, kb_ref, vb_ref,
                  kbf_ref, vbf_ref, sgb_ref, sma_ref, smb_ref, ka_ref, va_ref,
                  *, per_row, v_transposed):
    x = x_ref[...]
    if per_row:
        scale, shift = scale_ref[...], shift_ref[...]
    else:
        scale, shift = scale_ref[0], shift_ref[0]
    h = _rms_full(x, gnorm_ref[...]) * (1.0 + scale) + shift
    hb = h.astype(BF16)
    cos_t, sin_lo, sin_hi = cos_ref[...], slo_ref[...], shi_ref[...]

    def proj(c0, c1):
        if c1 <= C_GA:
            return _dot(hb, wlow_ref[:, c0:c1])
        return _dot(hb, win_ref[:, c0 - C_GA:c1 - C_GA])

    segq, seg64 = segq_ref[...], seg64_ref[...]
    pair = 2 * SLAB

    def finish_qa(p, qa, ms):
        qa = qa * lax.rsqrt(ms + NORM_EPS) * gq_ref[:, p * pair:(p + 1) * pair]
        for s in range(2):
            slab = _rope_slab(qa[:, s * SLAB:(s + 1) * SLAB], cos_t, sin_lo, sin_hi)
            qa_ref[:, p * pair + s * SLAB:p * pair + (s + 1) * SLAB] = slab.astype(BF16)

    def finish_qb(p, qb, ms):
        qb = qb * lax.rsqrt(ms + NORM_EPS) * gqb_ref[:, p * pair:(p + 1) * pair]
        qb_ref[:, p * pair:(p + 1) * pair] = qb.astype(BF16)

    def finish_kb(p, kb, ms):
        kb = kb * lax.rsqrt(ms + NORM_EPS) * gkb_ref[:, p * pair:(p + 1) * pair]
        kbf_ref[:, p * pair:(p + 1) * pair] = kb
        kb_ref[:, p * pair:(p + 1) * pair] = kb.astype(BF16)

    def seg_ms(x, seg):
        return _dot((x * x).astype(BF16), seg)

    low_rank = proj(C_QLAT, C_GA)
    ga = proj(C_GA, C_QB)
    sga_ref[...] = (ga * jax.nn.sigmoid(ga)).astype(BF16)

    lat = _rms_full(low_rank[:, C_KVLAT:C_KR], gkv_ref[...])
    lat_ref[...] = lat
    latb = lat.astype(BF16)
    krs = low_rank[:, C_KR:C_GA]
    ms = jnp.sum(krs * krs, axis=-1, keepdims=True) * (1.0 / A_ROPE)
    krs = _rope_slab(krs * lax.rsqrt(ms + NORM_EPS) * gkr_ref[...], cos_t, sin_lo, sin_hi)
    kr_ref[...] = pltpu.roll(krs, SLAB - ROPE_LANE0, axis=1)[:, :A_ROPE]

    _expand_keys = functools.partial(_finish_keys, ka_ref=ka_ref, krs=krs, gk_ref=gk_ref)
    segk = segk_ref[...]

    def kn_dot(p):
        return _dot(latb, wuk_ref[:, p * pair:(p + 1) * pair])

    r = _rms_full(low_rank[:, C_QLAT:C_KVLAT], gql_ref[...]).astype(BF16)
    qa = [None] * (A_HEADS // 2)
    kn = [None] * (A_HEADS // 2)
    qa[0] = _dot(r, wuq_ref[:, 0:pair])
    qb0 = proj(C_QB, C_QB + pair)
    qa[1] = _dot(r, wuq_ref[:, pair:2 * pair])
    finish_qa(0, qa[0], seg_ms(qa[0], segq))
    kb0 = proj(C_KB, C_KB + pair)
    qa[2] = _dot(r, wuq_ref[:, 2 * pair:3 * pair])
    finish_qa(1, qa[1], seg_ms(qa[1], segq))
    kn[0] = kn_dot(0)
    gb = proj(C_GB, C_MA)
    sgb_ref[...] = (gb * jax.nn.sigmoid(gb)).astype(BF16)
    qa[3] = _dot(r, wuq_ref[:, 3 * pair:4 * pair])
    finish_qa(2, qa[2], seg_ms(qa[2], segq))
    finish_qb(0, qb0, seg_ms(qb0, seg64))
    kn[1] = kn_dot(1)
    sma_ref[...] = jax.nn.sigmoid(proj(C_MA, C_MB)).astype(BF16)
    finish_qa(3, qa[3], seg_ms(qa[3], segq))
    finish_kb(0, kb0, seg_ms(kb0, seg64))
    _expand_keys(0, kn[0], seg_ms(kn[0], segk))
    qb1 = proj(C_QB + pair, C_QB + 2 * pair)
    kb1 = proj(C_KB + pair, C_KB + 2 * pair)
    kn[2] = kn_dot(2)
    smb_ref[...] = jax.nn.sigmoid(proj(C_MB, C_END)).astype(BF16)
    finish_qb(1, qb1, seg_ms(qb1, seg64))
    finish_kb(1, kb1, seg_ms(kb1, seg64))
    _expand_keys(1, kn[1], seg_ms(kn[1], segk))
    kn[3] = kn_dot(3)
    vb = proj(C_VB, C_GB)
    vbf_ref[...] = vb
    vb_ref[...] = vb.astype(BF16)
    _expand_keys(2, kn[2], seg_ms(kn[2], segk))
    _store_values(latb, wuv_ref, va_ref, v_transposed)
    _expand_keys(3, kn[3], seg_ms(kn[3], segk))


def _front(x2d, scale, shift, rope_tabs, consts, *, tm, rows_per_batch, per_row, tail_rows,
           v_transposed):
    rows = x2d.shape[0]
    nt = rows // tm
    row = lambda i: (i, 0)
    fixed = lambda i: (0, 0)
    tpb = rows_per_batch // tm
    if per_row:
        mod_spec = pl.BlockSpec((tm, D_MODEL), row)
        tab_spec = pl.BlockSpec((tm, SLAB), row)
    else:
        mod_spec = pl.BlockSpec((1, 1, D_MODEL), lambda i: (i // tpb, 0, 0))
        tab_spec = pl.BlockSpec((tm, SLAB), lambda i: (i % tpb, 0))

    def full(a):
        return pl.BlockSpec(a.shape, fixed)

    def out(width, dtype):
        return jax.ShapeDtypeStruct((rows, width), dtype), pl.BlockSpec((tm, width), row)

    if tail_rows == rows_per_batch:
        tail = out(B_WIDTH, F32)
    else:
        ntail = tail_rows // tm
        tail = (jax.ShapeDtypeStruct((rows // rows_per_batch * tail_rows, B_WIDTH), F32),
                pl.BlockSpec((tm, B_WIDTH),
                             lambda i: (i // tpb * ntail + jnp.maximum(i % tpb - (tpb - ntail), 0), 0)))

    if v_transposed:
        values = (jax.ShapeDtypeStruct((rows // rows_per_batch, A_HEADS // 2, tpb, SLAB, tm), BF16),
                  pl.BlockSpec((1, A_HEADS // 2, 1, SLAB, tm), lambda i: (i // tpb, 0, i % tpb, 0, 0)))
    else:
        values = out(A_WIDTH, BF16)

    outs = [out(A_HEADS * SLAB, BF16),
            out(A_KV_RANK, F32),
            out(A_ROPE, F32),
            out(A_WIDTH, BF16),
            out(B_WIDTH, BF16),
            out(B_WIDTH, BF16),
            out(B_WIDTH, BF16),
            tail,
            tail,
            out(B_WIDTH, BF16),
            out(D_MODEL, BF16),
            out(D_MODEL, BF16),
            out(A_HEADS * SLAB, BF16),
            values]
    return pl.pallas_call(
        functools.partial(_front_kernel, per_row=per_row, v_transposed=v_transposed),
        out_shape=[o[0] for o in outs],
        grid=(nt,),
        in_specs=[pl.BlockSpec((tm, D_MODEL), row), mod_spec, mod_spec,
                  tab_spec, tab_spec, tab_spec] + [full(a) for a in consts],
        out_specs=[o[1] for o in outs],
        compiler_params=pltpu.CompilerParams(dimension_semantics=("arbitrary",),
                                             vmem_limit_bytes=VMEM_LIMIT),
        name="front",
    )(x2d, scale, shift, *rope_tabs, *consts)


def _expand_kernel(lat_ref, krs_ref, wuk_ref, wuv_ref, gk_ref, segk_ref, ka_ref, va_ref):
    latb = lat_ref[...].astype(BF16)
    krs = krs_ref[...].astype(F32)
    segk = segk_ref[...]
    pair = 2 * SLAB
    for p in range(A_HEADS // 2):
        kn = _dot(latb, wuk_ref[:, p * pair:(p + 1) * pair])
        _finish_keys(p, kn, _dot((kn * kn).astype(BF16), segk), ka_ref=ka_ref, krs=krs, gk_ref=gk_ref)
    _store_values(latb, wuv_ref, va_ref, False)


def _expand(lat2d, krs2d, wuk_p, wuv, gk, segk, *, tm):
    rows = lat2d.shape[0]
    row = lambda i: (i, 0)
    fixed = lambda i: (0, 0)
    return pl.pallas_call(
        _expand_kernel,
        out_shape=[jax.ShapeDtypeStruct((rows, A_HEADS * SLAB), BF16),
                   jax.ShapeDtypeStruct((rows, A_WIDTH), BF16)],
        grid=(rows // tm,),
        in_specs=[pl.BlockSpec((tm, A_KV_RANK), row), pl.BlockSpec((tm, SLAB), row),
                  pl.BlockSpec(wuk_p.shape, fixed), pl.BlockSpec(wuv.shape, fixed),
                  pl.BlockSpec(gk.shape, fixed), pl.BlockSpec(segk.shape, fixed)],
        out_specs=[pl.BlockSpec((tm, A_HEADS * SLAB), row), pl.BlockSpec((tm, A_WIDTH), row)],
        compiler_params=pltpu.CompilerParams(dimension_semantics=("arbitrary",),
                                             vmem_limit_bytes=VMEM_LIMIT),
        name="expand",
    )(lat2d, krs2d, wuk_p, wuv, gk, segk)


def _select_heads(o_even, o_odd):
    lane = lax.broadcasted_iota(jnp.int32, o_even.shape, 1)
    return jnp.where(lane < A_VDIM, o_even, o_odd)


def _mla_prompt_kernel(q_ref, k_ref, vt_ref, o_ref, *, tq, nsub):
    qi = pl.program_id(2)
    tk = vt_ref.shape[-1]
    ratio = tq // tk
    n_full = nsub * ratio * qi
    chunk_delta = (lax.broadcasted_iota(jnp.int32, (tk, tq), 0) // CHUNK
                   - lax.broadcasted_iota(jnp.int32, (tk, tq), 1) // CHUNK)
    ones = jnp.ones((16, tk), BF16)
    chains = [(e, h) for h in range(nsub) for e in range(2)]

    def step(j, carry, work, ref):
        k0 = pl.multiple_of(j * tk, tk)
        scores = {}
        new = list(carry)

        def qk(c):
            e, h = chains[c]
            k = k_ref[0, pl.ds(k0, tk), e * SLAB:(e + 1) * SLAB]
            q = q_ref[0, h * tq:(h + 1) * tq, e * SLAB:(e + 1) * SLAB]
            scores[c] = _dot_nt(k, q)

        def update(c, max_delta):
            e, h = chains[c]
            m, acc, drift = carry[c]
            s = scores.pop(c)
            if max_delta is not None:
                s = jnp.where(chunk_delta <= max_delta, s, NEG_INF)
            vt1 = jnp.concatenate([vt_ref[0, 0, j, e * A_VDIM:(e + 1) * A_VDIM, :], ones], axis=0)
            bmax = jnp.max(s, axis=0, keepdims=True)
            m_new = jnp.maximum(m, bmax)
            if ref == "exact":
                p = jnp.exp2(s - m_new).astype(BF16)
                new[c] = (m_new, jnp.exp2(m - m_new) * acc + _dot(vt1, p), drift)
            else:
                m_ref = m
                if ref == "probe":
                    m_ref = jnp.maximum(m, jnp.max(s[:CHUNK], axis=0, keepdims=True))
                    acc = jnp.exp2(m - m_ref) * acc
                p = jnp.exp2(s - m_ref).astype(BF16)
                new[c] = (m_new, jnp.exp2(m_ref - m_new) * (acc + _dot(vt1, p)),
                          jnp.maximum(drift, bmax - m_ref))

        for c, _ in work:
            qk(c)
        for c, max_delta in work:
            update(c, max_delta)
        return tuple(new)

    everyone = [(c, None) for c in range(len(chains))]
    diagonal = []
    for d in range(nsub * ratio):
        work = []
        for c, (e, h) in enumerate(chains):
            if d * tk >= (h + 1) * tq:
                continue
            before = (d + 1) * tk <= h * tq
            work.append((c, None if before else (h * tq - d * tk) // CHUNK))
        diagonal.append(work)

    def attend(streaming):
        carry = tuple((jnp.full((1, tq), NEG_INF, F32), jnp.zeros((A_VDIM + 16, tq), F32),
                       jnp.full((1, tq), NEG_INF, F32)) for _ in chains)
        if streaming:
            first = jnp.minimum(n_full, 1)
            carry = lax.fori_loop(0, first, lambda j, cr: step(j, cr, everyone, "probe"), carry)
            carry = lax.fori_loop(first, n_full, lambda j, cr: step(j, cr, everyone, "lagged"), carry)
        else:
            carry = lax.fori_loop(0, n_full, lambda j, cr: step(j, cr, everyone, "exact"), carry)
        for d, work in enumerate(diagonal):
            carry = step(n_full + d, carry, work,
                         "exact" if not streaming else ("probe" if d == 0 else "lagged"))
        for h in range(nsub):
            outs = [carry[2 * h + e][1] for e in range(2)]
            outs = [acc[:A_VDIM] / acc[A_VDIM:A_VDIM + 1] for acc in outs]
            o_ref[0, h * tq:(h + 1) * tq, :] = jnp.concatenate(outs, axis=0).T.astype(o_ref.dtype)
        return functools.reduce(jnp.maximum, [jnp.max(cr[2]) for cr in carry])

    worst_drift = attend(True)

    @pl.when(worst_drift > MAX_EXP2_DRIFT)
    def _():
        attend(False)


def _mla_prompt(qa, ka, vt, *, tq, nsub):
    b, s, _ = qa.shape
    tk = vt.shape[-1]
    assert vt.shape == (b, A_HEADS // 2, s // tk, SLAB, tk) and tq % tk == 0
    tqq = nsub * tq
    return pl.pallas_call(
        functools.partial(_mla_prompt_kernel, tq=tq, nsub=nsub),
        out_shape=jax.ShapeDtypeStruct((b, s, A_WIDTH), BF16),
        grid=(b, A_HEADS // 2, s // tqq),
        in_specs=[pl.BlockSpec((1, tqq, 2 * SLAB), lambda bi, hp, qi: (bi, qi, hp)),
                  pl.BlockSpec((1, s, 2 * SLAB), lambda bi, hp, qi: (bi, 0, hp)),
                  pl.BlockSpec((1, 1, s // tk, SLAB, tk), lambda bi, hp, qi: (bi, hp, 0, 0, 0))],
        out_specs=pl.BlockSpec((1, tqq, SLAB), lambda bi, hp, qi: (bi, qi, hp)),
        compiler_params=pltpu.CompilerParams(
            dimension_semantics=("arbitrary", "arbitrary", "arbitrary"),
            vmem_limit_bytes=VMEM_LIMIT),
        name="mla_prompt",
    )(qa, ka, vt)


def _softmax_over_parts(scores, values):
    m = functools.reduce(jnp.maximum, [jnp.max(s, axis=-1, keepdims=True) for s in scores])
    ps = [jnp.exp2(s - m) for s in scores]
    l = sum(jnp.sum(p, axis=-1, keepdims=True) for p in ps)
    return sum(_dot(p.astype(BF16), v) for p, v in zip(ps, values)) / l


def _mla_sample_kernel(q_ref, kc_ref, vc_ref, kn_ref, vn_ref, o_ref):
    for hp in range(A_HEADS // 2):
        values = [v_ref[0, :, hp * SLAB:(hp + 1) * SLAB] for v_ref in (vc_ref, vn_ref)]
        outs = []
        for e in range(2):
            lanes = slice((2 * hp + e) * SLAB, (2 * hp + e + 1) * SLAB)
            q = q_ref[0, :, lanes]
            scores = [_dot_nt(q, k_ref[0, :, lanes]) for k_ref in (kc_ref, kn_ref)]
            outs.append(_softmax_over_parts(scores, values))
        o_ref[0, :, hp * SLAB:(hp + 1) * SLAB] = _select_heads(outs[0], outs[1]).astype(o_ref.dtype)


def _mla_sample(qa, ka_cache, va_cache, ka_new, va_new):
    b, t, _ = qa.shape
    n = ka_cache.shape[1]
    spec = lambda rows, width: pl.BlockSpec((1, rows, width), lambda bi: (bi, 0, 0))
    return pl.pallas_call(
        _mla_sample_kernel,
        out_shape=jax.ShapeDtypeStruct((b, t, A_WIDTH), BF16),
        grid=(b,),
        in_specs=[spec(t, A_HEADS * SLAB), spec(n, A_HEADS * SLAB), spec(n, A_WIDTH),
                  spec(t, A_HEADS * SLAB), spec(t, A_WIDTH)],
        out_specs=spec(t, A_WIDTH),
        compiler_params=pltpu.CompilerParams(dimension_semantics=("arbitrary",),
                                             vmem_limit_bytes=VMEM_LIMIT),
        name="mla_sample",
    )(qa, ka_cache, va_cache, ka_new, va_new)


BAND_TQ = 256
BAND_KEYS = B_WINDOW + BAND_TQ
TOEPLITZ_W = 1024


def _band_table_kernel(rb_ref, onehot_ref, allowed_ref, tb_ref, tp_ref):
    g = jnp.dot(rb_ref[...], onehot_ref[...], preferred_element_type=F32,
                precision=lax.Precision.HIGHEST) * LOG2E
    allowed = allowed_ref[...] > 0.0
    for h in range(B_HEADS):
        row = jnp.broadcast_to(g[h:h + 1, :], (BAND_TQ, TOEPLITZ_W))
        t = pltpu.roll(row, 0, axis=1, stride=1, stride_axis=0)[:, :BAND_KEYS]
        tb_ref[h] = t
        tp_ref[h] = jnp.where(allowed, t, NEG_INF)


def _band_tables(rel_bias):
    n_rel = rel_bias.shape[1]
    n_pad = -(-n_rel // LANES) * LANES
    x = np.arange(TOEPLITZ_W)
    key_minus_query = np.where(x < BAND_KEYS, x, x - TOEPLITZ_W)
    dist = B_WINDOW - key_minus_query
    idx = np.clip(dist, -B_MAX_REL, B_MAX_REL) + B_MAX_REL
    onehot = np.zeros((n_pad, TOEPLITZ_W), np.float32)
    onehot[idx, x] = 1.0
    q_chunk = np.arange(BAND_TQ) // CHUNK + B_LEFT_CHUNKS
    k_chunk = np.arange(BAND_KEYS) // CHUNK
    allowed = (k_chunk[None, :] <= q_chunk[:, None]) & (k_chunk[None, :] >= q_chunk[:, None] - B_LEFT_CHUNKS)
    rb = jnp.pad(rel_bias, ((0, 0), (0, n_pad - n_rel)))
    shape = jax.ShapeDtypeStruct((B_HEADS, BAND_TQ, BAND_KEYS), F32)
    return pl.pallas_call(
        _band_table_kernel,
        out_shape=[shape, shape],
        compiler_params=pltpu.CompilerParams(vmem_limit_bytes=VMEM_LIMIT),
        name="band_table",
    )(rb, jnp.asarray(onehot), jnp.asarray(allowed.astype(np.float32)))


def _head_lane_masks(width):
    lane = lax.broadcasted_iota(jnp.int32, (1, width), 1)
    return [(lane < B_HDIM), (lane >= B_HDIM)]


def _band_prompt_kernel(q_ref, kp_ref, kc_ref, vp_ref, vc_ref, t_ref, o_ref, *, tq, nsub):
    qi = pl.program_id(2)
    nprev = B_WINDOW // tq
    nkb = nprev + 1

    def window(prev_ref, cur_ref, w):
        if w < nprev:
            return prev_ref[0, w * tq:(w + 1) * tq, :]
        return cur_ref[0, (w - nprev) * tq:(w - nprev + 1) * tq, :]

    masks = _head_lane_masks(SLAB)
    table = t_ref[...].reshape(2 * tq, nkb * tq)
    scores = []
    for t in range(nsub):
        q = q_ref[0, t * tq:(t + 1) * tq, :]
        qm = jnp.concatenate([jnp.where(mask, q, jnp.zeros_like(q)) for mask in masks], axis=0)
        ss = []
        for j in range(nkb):
            s = _dot_nt(qm, window(kp_ref, kc_ref, t + j)) + table[:, j * tq:(j + 1) * tq]
            if t + j < nprev:
                s = jnp.where(qi > 0, s, NEG_INF)
            ss.append(s)
        scores.append(ss)
    for t in range(nsub):
        o = _softmax_over_parts(scores[t], [window(vp_ref, vc_ref, t + j) for j in range(nkb)])
        o_ref[0, t * tq:(t + 1) * tq, :] = _select_heads(o[:tq], o[tq:]).astype(o_ref.dtype)


def _band_prompt(qb, kb, vb, table, *, tq, nsub):
    b, s, _ = qb.shape
    tqq = nsub * tq
    per_step = tqq // B_WINDOW
    assert tqq % B_WINDOW == 0 and B_WINDOW % tq == 0
    cur = pl.BlockSpec((1, tqq, SLAB), lambda hp, bi, qi: (bi, qi, hp))
    prev = pl.BlockSpec((1, B_WINDOW, SLAB),
                        lambda hp, bi, qi: (bi, jnp.maximum(qi * per_step - 1, 0), hp))
    return pl.pallas_call(
        functools.partial(_band_prompt_kernel, tq=tq, nsub=nsub),
        out_shape=jax.ShapeDtypeStruct((b, s, B_WIDTH), BF16),
        grid=(B_HEADS // 2, b, s // tqq),
        in_specs=[cur, prev, cur, prev, cur,
                  pl.BlockSpec((2, tq, B_WINDOW + tq), lambda hp, bi, qi: (hp, 0, 0))],
        out_specs=cur,
        compiler_params=pltpu.CompilerParams(
            dimension_semantics=("arbitrary", "arbitrary", "arbitrary"),
            vmem_limit_bytes=VMEM_LIMIT),
        name="band_prompt",
    )(qb, kb, kb, vb, vb, table)


def _band_sample_kernel(q_ref, kc_ref, vc_ref, kn_ref, vn_ref, t_ref, o_ref):
    masks = _head_lane_masks(SLAB)
    for hp in range(B_HEADS // 2):
        lanes = slice(hp * SLAB, (hp + 1) * SLAB)
        q = q_ref[0, :, lanes]
        keys = [kc_ref[0, :, lanes].astype(BF16), kn_ref[0, :, lanes]]
        values = [vc_ref[0, :, lanes].astype(BF16), vn_ref[0, :, lanes]]
        outs = []
        for e in range(2):
            qm = jnp.where(masks[e], q, jnp.zeros_like(q))
            scores, col = [], 0
            for k in keys:
                scores.append(_dot_nt(qm, k) + t_ref[2 * hp + e, :, col:col + k.shape[0]])
                col += k.shape[0]
            outs.append(_softmax_over_parts(scores, values))
        o_ref[0, :, lanes] = _select_heads(outs[0], outs[1]).astype(o_ref.dtype)


def _band_sample(qb, kb_cache, vb_cache, kb_new, vb_new, table):
    b, t, _ = qb.shape
    n = kb_cache.shape[1]
    spec = lambda rows: pl.BlockSpec((1, rows, B_WIDTH), lambda bi: (bi, 0, 0))
    return pl.pallas_call(
        _band_sample_kernel,
        out_shape=jax.ShapeDtypeStruct((b, t, B_WIDTH), BF16),
        grid=(b,),
        in_specs=[spec(t), spec(n), spec(n), spec(t), spec(t),
                  pl.BlockSpec((B_HEADS, t, BAND_KEYS), lambda bi: (0, 0, 0))],
        out_specs=spec(t),
        compiler_params=pltpu.CompilerParams(dimension_semantics=("arbitrary",),
                                             vmem_limit_bytes=VMEM_LIMIT),
        name="band_sample",
    )(qb, kb_cache, vb_cache, kb_new, vb_new, table)


def _back_kernel(x_ref, gate_ref, aa_ref, sga_ref, ab_ref, sgb_ref, sma_ref, smb_ref,
                 woa_ref, wob_ref, wout_ref, o_ref, *, per_row):
    gate = gate_ref[...] if per_row else gate_ref[0]
    ua = _dot(aa_ref[...] * sga_ref[...], woa_ref[...])
    ub = _dot(ab_ref[...] * sgb_ref[...], wob_ref[...])
    merged = sma_ref[...].astype(F32) * ua + smb_ref[...].astype(F32) * ub
    o_ref[...] = x_ref[...] + gate * _dot(merged.astype(BF16), wout_ref[...])


def _back(x2d, gate, aa, sga, ab, sgb, sma, smb, woa, wob, wout, *, tm, rows_per_batch, per_row):
    rows = x2d.shape[0]
    row = lambda i: (i, 0)
    fixed = lambda i: (0, 0)
    if per_row:
        gate_spec = pl.BlockSpec((tm, D_MODEL), row)
    else:
        tpb = rows_per_batch // tm
        gate_spec = pl.BlockSpec((1, 1, D_MODEL), lambda i: (i // tpb, 0, 0))
    return pl.pallas_call(
        functools.partial(_back_kernel, per_row=per_row),
        out_shape=jax.ShapeDtypeStruct((rows, D_MODEL), F32),
        grid=(rows // tm,),
        in_specs=[pl.BlockSpec((tm, D_MODEL), row), gate_spec,
                  pl.BlockSpec((tm, A_WIDTH), row), pl.BlockSpec((tm, A_WIDTH), row),
                  pl.BlockSpec((tm, B_WIDTH), row), pl.BlockSpec((tm, B_WIDTH), row),
                  pl.BlockSpec((tm, D_MODEL), row), pl.BlockSpec((tm, D_MODEL), row),
                  pl.BlockSpec(woa.shape, fixed), pl.BlockSpec(wob.shape, fixed),
                  pl.BlockSpec(wout.shape, fixed)],
        out_specs=pl.BlockSpec((tm, D_MODEL), row),
        compiler_params=pltpu.CompilerParams(dimension_semantics=("arbitrary",),
                                             vmem_limit_bytes=VMEM_LIMIT),
        name="back",
    )(x2d, gate, aa, sga, ab, sgb, sma, smb, woa, wob, wout)


def _seg_matrix(group_of_lane, sizes):
    lane = np.arange(2 * SLAB)
    slab = lane // SLAB
    grp = group_of_lane[lane % SLAB]
    same = (slab[:, None] == slab[None, :]) & (grp[:, None] == grp[None, :])
    return jnp.asarray(np.where(same, 1.0 / sizes[grp][None, :], 0.0), dtype=BF16)


def _rope_tables(pos):
    inv = ROPE_BASE ** (-np.arange(0, A_ROPE, 2, dtype=np.float64) / A_ROPE)
    ang = np.asarray(pos, np.float64)[:, None] * inv[None, :]
    cos, sin = np.cos(ang), np.sin(ang)
    n = ang.shape[0]
    ones = np.ones((n, A_NOPE))
    zeros = np.zeros((n, A_NOPE))
    z16 = np.zeros((n, HALF_ROPE))
    pad1 = np.ones((n, SLAB - A_QK))
    pad0 = np.zeros((n, SLAB - A_QK))
    cos_t = np.concatenate([ones, cos, cos, pad1], axis=1)
    sin_lo = np.concatenate([zeros, -sin, z16, pad0], axis=1)
    sin_hi = np.concatenate([zeros, z16, sin, pad0], axis=1)
    return tuple(jnp.asarray(a, dtype=F32) for a in (cos_t, sin_lo, sin_hi))


def kernel(x_prompt, x_sample, cache_mla_latent, cache_mla_krope, cache_band_k, cache_band_v,
           c_prompt, c_sample, g_norm, w_ada, b_ada, w_in, g_q_lat, w_uq, g_kv_lat, w_uk, w_uv,
           g_qn_a, g_qr_a, g_kn_a, g_kr_a, g_q_b, g_k_b, rel_bias_b, w_oa, w_ob, w_out):
    bp, s, _ = x_prompt.shape
    bs, t, _ = x_sample.shape
    past = cache_mla_latent.shape[2]
    win_s = cache_band_k.shape[2]
    win_p = min(B_WINDOW, s)
    depth = g_norm.shape[0]
    assert depth == 1

    seg_q = _seg_matrix(np.where(np.arange(SLAB) < A_NOPE, 0, np.where(np.arange(SLAB) < A_QK, 1, 2)),
                        np.array([A_NOPE, A_ROPE, SLAB - A_QK], np.float64))
    seg_64 = _seg_matrix(np.arange(SLAB) // B_HDIM, np.array([B_HDIM, B_HDIM], np.float64))
    seg_k = _seg_matrix(np.where(np.arange(SLAB) < A_NOPE, 0, 1),
                        np.array([A_NOPE, SLAB - A_NOPE], np.float64))

    assert win_s == B_WINDOW and t <= BAND_TQ
    pos_s = past + np.arange(t)

    (g_norm, w_ada, b_ada, w_in, g_q_lat, w_uq, g_kv_lat, w_uk, w_uv, g_qn_a, g_qr_a, g_kn_a,
     g_kr_a, g_q_b, g_k_b, rel_bias_b, w_oa, w_ob, w_out) = [
        a.reshape(a.shape[1:]) for a in
        (g_norm, w_ada, b_ada, w_in, g_q_lat, w_uq, g_kv_lat, w_uk, w_uv, g_qn_a, g_qr_a, g_kn_a,
         g_kr_a, g_q_b, g_k_b, rel_bias_b, w_oa, w_ob, w_out)]
    zpad = jnp.zeros((D_MODEL, SLAB - A_QK), F32)
    z64 = jnp.zeros((D_MODEL, A_NOPE), F32)
    c_kr0 = A_Q_RANK + A_KV_RANK
    w_low = jnp.concatenate([w_in[:, :c_kr0], z64, w_in[:, c_kr0:c_kr0 + A_ROPE], zpad],
                            axis=1).astype(BF16)
    w_rest = w_in[:, c_kr0 + A_ROPE:].astype(BF16)
    wuq_p = jnp.pad(w_uq, ((0, 0), (0, 0), (0, SLAB - A_QK))).reshape(A_Q_RANK, -1).astype(BF16)
    wuk_p = jnp.pad(w_uk, ((0, 0), (0, 0), (0, SLAB - A_NOPE))).reshape(A_KV_RANK, -1).astype(BF16)
    wuv_b = w_uv.reshape(A_KV_RANK, -1).astype(BF16)
    wuv_t = wuv_b.T
    qscale = A_SCALE * LOG2E
    gq = jnp.tile(jnp.concatenate([g_qn_a, g_qr_a, jnp.zeros((SLAB - A_QK,), F32)]) * qscale,
                  A_HEADS)[None]
    gk = jnp.tile(jnp.concatenate([g_kn_a, jnp.zeros((SLAB - A_NOPE,), F32)]), A_HEADS)[None]
    gkr = jnp.concatenate([jnp.zeros((A_NOPE,), F32), g_kr_a, jnp.zeros((SLAB - A_QK,), F32)])[None]
    gqb = jnp.tile(g_q_b * (B_SCALE * LOG2E), B_HEADS)[None]
    gkb = jnp.tile(g_k_b, B_HEADS)[None]
    consts = (g_norm[None], w_low, w_rest, g_q_lat[None], wuq_p, g_kv_lat[None], gq, gkr,
              gqb, gkb, seg_q, seg_64)
    woa, wob, wout = w_oa.astype(BF16), w_ob.astype(BF16), w_out.astype(BF16)

    c_rows = jnp.concatenate([c_prompt, jnp.zeros((8 - bp, D_MODEL), F32),
                              jnp.repeat(c_sample, t, axis=0)], axis=0)
    mod = _adaln(c_rows, w_ada, b_ada)
    shift, scale, gate = mod[:, :D_MODEL], mod[:, D_MODEL:2 * D_MODEL], mod[:, 2 * D_MODEL:]

    xp2d = x_prompt.reshape(bp * s, D_MODEL)
    (qa, lat, kr, sga, qb, kb, vb, kbf, vbf, sgb, sma, smb, ka, vt) = _front(
        xp2d, scale[:bp, None], shift[:bp, None], _rope_tables(np.arange(s)),
        consts + (wuk_p, wuv_t, gk, seg_k),
        tm=FRONT_TM, rows_per_batch=s, per_row=False, tail_rows=win_p, v_transposed=True)
    attn_a = _mla_prompt(qa.reshape(bp, s, -1), ka.reshape(bp, s, -1), vt, tq=MLA_TQ, nsub=MLA_NSUB)
    table_plain, table_band = _band_tables(rel_bias_b)
    attn_b = _band_prompt(qb.reshape(bp, s, -1), kb.reshape(bp, s, -1), vb.reshape(bp, s, -1),
                          table_band, tq=BAND_TQ, nsub=BAND_NSUB)
    y_prompt = _back(xp2d, gate[:bp, None], attn_a.reshape(bp * s, -1), sga,
                     attn_b.reshape(bp * s, -1), sgb, sma, smb, woa, wob, wout,
                     tm=BACK_TM, rows_per_batch=s, per_row=False).reshape(bp, s, D_MODEL)

    xs2d = x_sample.reshape(bs * t, D_MODEL)
    rows_s = bs * t
    pos_tab = _rope_tables(np.tile(pos_s, bs))
    (qa2, lat2, kr2, sga2, qb2, kb2, vb2, kbf2, vbf2, sgb2, sma2, smb2, ka_n, va_n) = _front(
        xs2d, scale[8:], shift[8:], pos_tab, consts + (wuk_p, wuv_b, gk, seg_k),
        tm=rows_s, rows_per_batch=t, per_row=True, tail_rows=t, v_transposed=False)
    krs_cache = jnp.pad(cache_mla_krope.reshape(bs * past, A_ROPE),
                        ((0, 0), (A_NOPE, SLAB - A_QK))).astype(BF16)
    ka_c, va_c = _expand(cache_mla_latent.reshape(bs * past, A_KV_RANK), krs_cache,
                         wuk_p, wuv_b, gk, seg_k, tm=past // 2)
    attn_a2 = _mla_sample(qa2.reshape(bs, t, -1), ka_c.reshape(bs, past, -1), va_c.reshape(bs, past, -1),
                          ka_n.reshape(bs, t, -1), va_n.reshape(bs, t, -1))
    attn_b2 = _band_sample(qb2.reshape(bs, t, -1), cache_band_k.reshape(bs, win_s, B_WIDTH),
                           cache_band_v.reshape(bs, win_s, B_WIDTH), kb2.reshape(bs, t, -1),
                           vb2.reshape(bs, t, -1), table_plain)
    y_sample = _back(xs2d, gate[8:], attn_a2.reshape(rows_s, -1), sga2, attn_b2.reshape(rows_s, -1),
                     sgb2, sma2, smb2, woa, wob, wout,
                     tm=rows_s, rows_per_batch=t, per_row=True).reshape(bs, t, D_MODEL)

    return (y_prompt, y_sample,
            lat.reshape(1, bp, s, A_KV_RANK), kr.reshape(1, bp, s, A_ROPE),
            kbf.reshape(1, bp, win_p, B_HEADS, B_HDIM), vbf.reshape(1, bp, win_p, B_HEADS, B_HDIM),
            lat2.reshape(1, bs, t, A_KV_RANK), kr2.reshape(1, bs, t, A_ROPE),
            kbf2.reshape(1, bs, t, B_HEADS, B_HDIM), vbf2.reshape(1, bs, t, B_HEADS, B_HDIM))
```

```python
import functools
import math

import jax
import jax.numpy as jnp
import numpy as np
from jax import lax
from jax.experimental import pallas as pl
from jax.experimental.pallas import tpu as pltpu

F32 = jnp.float32
BF16 = jnp.bfloat16

D_MODEL = 1024
CHUNK = 64
A_HEADS = 8
A_NOPE = 64
A_ROPE = 32
A_VDIM = 64
A_QK = A_NOPE + A_ROPE
A_Q_RANK = 384
A_KV_RANK = 256
A_WIDTH = A_HEADS * A_VDIM
A_SCALE = A_QK ** -0.5
B_HEADS = 8
B_HDIM = 64
B_WIDTH = B_HEADS * B_HDIM
B_LEFT_CHUNKS = 8
B_WINDOW = B_LEFT_CHUNKS * CHUNK
B_MAX_REL = 256
B_SCALE = B_HDIM ** -0.5
ROPE_BASE = 10000.0
NORM_EPS = 1e-6
NEG_INF = -1e30
LOG2E = math.log2(math.e)
MAX_EXP2_DRIFT = 60.0

LANES = 128
SLAB = LANES
HALF_ROPE = A_ROPE // 2
ROPE_LANE0 = A_NOPE
VMEM_LIMIT = 56 * 1024 * 1024

FRONT_TM = 512
MLA_TQ = 512
MLA_NSUB = 4
BAND_NSUB = 8
BACK_TM = 1024

C_QLAT = 0
C_KVLAT = C_QLAT + A_Q_RANK
C_KR = C_KVLAT + A_KV_RANK
C_GA = C_KR + SLAB
C_QB = C_GA + A_WIDTH
C_KB = C_QB + B_WIDTH
C_VB = C_KB + B_WIDTH
C_GB = C_VB + B_WIDTH
C_MA = C_GB + B_WIDTH
C_MB = C_MA + D_MODEL
C_END = C_MB + D_MODEL

NT = (((1,), (1,)), ((), ()))


def _dot(a, b):
    return jnp.dot(a, b, preferred_element_type=F32)


def _dot_nt(a, b):
    return lax.dot_general(a, b, NT, preferred_element_type=F32)


def _rms_full(x, g):
    ms = jnp.mean(x * x, axis=-1, keepdims=True)
    return x * lax.rsqrt(ms + NORM_EPS) * g


def _rope_slab(x, cos_t, sin_lo, sin_hi):
    return (x * cos_t
            + pltpu.roll(x, SLAB - HALF_ROPE, axis=1) * sin_lo
            + pltpu.roll(x, HALF_ROPE, axis=1) * sin_hi)


def _adaln_kernel(c_ref, w_ref, b_ref, o_ref):
    c = c_ref[...]
    sc = c * jax.nn.sigmoid(c)
    o_ref[...] = jnp.dot(sc, w_ref[...], preferred_element_type=F32,
                         precision=lax.Precision.HIGHEST) + b_ref[...]


def _adaln(c_rows, w_ada, b_ada):
    n = c_rows.shape[0]
    tn = 1024
    return pl.pallas_call(
        _adaln_kernel,
        out_shape=jax.ShapeDtypeStruct((n, 3 * D_MODEL), F32),
        grid=(3 * D_MODEL // tn,),
        in_specs=[pl.BlockSpec((n, D_MODEL), lambda j: (0, 0)),
                  pl.BlockSpec((D_MODEL, tn), lambda j: (0, j)),
                  pl.BlockSpec((1, tn), lambda j: (0, j))],
        out_specs=pl.BlockSpec((n, tn), lambda j: (0, j)),
        compiler_params=pltpu.CompilerParams(dimension_semantics=("arbitrary",),
                                             vmem_limit_bytes=VMEM_LIMIT),
        name="adaln",
    )(c_rows, w_ada, b_ada.reshape(1, -1))


def _finish_keys(p, kn, ms, *, ka_ref, krs, gk_ref):
    pair = 2 * SLAB
    kn = kn * lax.rsqrt(ms + NORM_EPS) * gk_ref[:, p * pair:(p + 1) * pair]
    for s in range(2):
        ka_ref[:, p * pair + s * SLAB:p * pair + (s + 1) * SLAB] = (
            kn[:, s * SLAB:(s + 1) * SLAB] + krs).astype(BF16)


def _store_values(latb, wuv_ref, va_ref, v_transposed):
    if v_transposed:
        vt = _dot_nt(wuv_ref[...], latb).astype(BF16)
        for hp in range(A_HEADS // 2):
            va_ref[0, hp, 0] = vt[hp * SLAB:(hp + 1) * SLAB, :]
    else:
        va_ref[...] = _dot(latb, wuv_ref[...]).astype(BF16)


def _front_kernel(x_ref, scale_ref, shift_ref, cos_ref, slo_ref, shi_ref,
                  gnorm_ref, wlow_ref, win_ref, gql_ref, wuq_ref, gkv_ref, gq_ref, gkr_ref,
                  gqb_ref, gkb_ref, segq_ref, seg64_ref, wuk_ref, wuv_ref, gk_ref, segk_ref,
                  qa_ref, lat_ref, kr_ref, sga_ref, qb_ref, kb_ref, vb_ref,
                  kbf_ref, vbf_ref, sgb_ref, sma_ref, smb_ref, ka_ref, va_ref,
                  *, per_row, v_transposed):
    x = x_ref[...]
    if per_row:
        scale, shift = scale_ref[...], shift_ref[...]
    else:
        scale, shift = scale_ref[0], shift_ref[0]
    h = _rms_full(x, gnorm_ref[...]) * (1.0 + scale) + shift
    hb = h.astype(BF16)
    cos_t, sin_lo, sin_hi = cos_ref[...], slo_ref[...], shi_ref[...]

    def proj(c0, c1):
        if c1 <= C_GA:
            return _dot(hb, wlow_ref[:, c0:c1])
        return _dot(hb, win_ref[:, c0 - C_GA:c1 - C_GA])

    segq, seg64 = segq_ref[...], seg64_ref[...]
    pair = 2 * SLAB

    def finish_qa(p, qa, ms):
        qa = qa * lax.rsqrt(ms + NORM_EPS) * gq_ref[:, p * pair:(p + 1) * pair]
        for s in range(2):
            slab = _rope_slab(qa[:, s * SLAB:(s + 1) * SLAB], cos_t, sin_lo, sin_hi)
            qa_ref[:, p * pair + s * SLAB:p * pair + (s + 1) * SLAB] = slab.astype(BF16)

    def finish_qb(p, qb, ms):
        qb = qb * lax.rsqrt(ms + NORM_EPS) * gqb_ref[:, p * pair:(p + 1) * pair]
        qb_ref[:, p * pair:(p + 1) * pair] = qb.astype(BF16)

    def finish_kb(p, kb, ms):
        kb = kb * lax.rsqrt(ms + NORM_EPS) * gkb_ref[:, p * pair:(p + 1) * pair]
        kbf_ref[:, p * pair:(p + 1) * pair] = kb
        kb_ref[:, p * pair:(p + 1) * pair] = kb.astype(BF16)

    def seg_ms(x, seg):
        return _dot((x * x).astype(BF16), seg)

    low_rank = proj(C_QLAT, C_GA)
    ga = proj(C_GA, C_QB)
    sga_ref[...] = (ga * jax.nn.sigmoid(ga)).astype(BF16)

    lat = _rms_full(low_rank[:, C_KVLAT:C_KR], gkv_ref[...])
    lat_ref[...] = lat
    latb = lat.astype(BF16)
    krs = low_rank[:, C_KR:C_GA]
    ms = jnp.sum(krs * krs, axis=-1, keepdims=True) * (1.0 / A_ROPE)
    krs = _rope_slab(krs * lax.rsqrt(ms + NORM_EPS) * gkr_ref[...], cos_t, sin_lo, sin_hi)
    if v_transposed:
        kr_ref[0] = krs.T[ROPE_LANE0:ROPE_LANE0 + A_ROPE, :]
    else:
        kr_ref[...] = pltpu.roll(krs, SLAB - ROPE_LANE0, axis=1)[:, :A_ROPE]

    _expand_keys = functools.partial(_finish_keys, ka_ref=ka_ref, krs=krs, gk_ref=gk_ref)
    segk = segk_ref[...]

    def kn_dot(p):
        return _dot(latb, wuk_ref[:, p * pair:(p + 1) * pair])

    r = _rms_full(low_rank[:, C_QLAT:C_KVLAT], gql_ref[...]).astype(BF16)
    qa = [None] * (A_HEADS // 2)
    kn = [None] * (A_HEADS // 2)
    qa[0] = _dot(r, wuq_ref[:, 0:pair])
    qb0 = proj(C_QB, C_QB + pair)
    qa[1] = _dot(r, wuq_ref[:, pair:2 * pair])
    finish_qa(0, qa[0], seg_ms(qa[0], segq))
    kb0 = proj(C_KB, C_KB + pair)
    qa[2] = _dot(r, wuq_ref[:, 2 * pair:3 * pair])
    finish_qa(1, qa[1], seg_ms(qa[1], segq))
    kn[0] = kn_dot(0)
    gb = proj(C_GB, C_MA)
    sgb_ref[...] = (gb * jax.nn.sigmoid(gb)).astype(BF16)
    qa[3] = _dot(r, wuq_ref[:, 3 * pair:4 * pair])
    finish_qa(2, qa[2], seg_ms(qa[2], segq))
    finish_qb(0, qb0, seg_ms(qb0, seg64))
    kn[1] = kn_dot(1)
    sma_ref[...] = jax.nn.sigmoid(proj(C_MA, C_MB)).astype(BF16)
    finish_qa(3, qa[3], seg_ms(qa[3], segq))
    finish_kb(0, kb0, seg_ms(kb0, seg64))
    _expand_keys(0, kn[0], seg_ms(kn[0], segk))
    qb1 = proj(C_QB + pair, C_QB + 2 * pair)
    kb1 = proj(C_KB + pair, C_KB + 2 * pair)
    kn[2] = kn_dot(2)
    smb_ref[...] = jax.nn.sigmoid(proj(C_MB, C_END)).astype(BF16)
    finish_qb(1, qb1, seg_ms(qb1, seg64))
    finish_kb(1, kb1, seg_ms(kb1, seg64))
    _expand_keys(1, kn[1], seg_ms(kn[1], segk))
    kn[3] = kn_dot(3)
    vb = proj(C_VB, C_GB)
    vbf_ref[...] = vb
    vb_ref[...] = vb.astype(BF16)
    _expand_keys(2, kn[2], seg_ms(kn[2], segk))
    _store_values(latb, wuv_ref, va_ref, v_transposed)
    _expand_keys(3, kn[3], seg_ms(kn[3], segk))


def _front(x2d, scale, shift, rope_tabs, consts, *, tm, rows_per_batch, per_row, tail_rows,
           v_transposed):
    rows = x2d.shape[0]
    nt = rows // tm
    row = lambda i: (i, 0)
    fixed = lambda i: (0, 0)
    tpb = rows_per_batch // tm
    if per_row:
        mod_spec = pl.BlockSpec((tm, D_MODEL), row)
        tab_spec = pl.BlockSpec((tm, SLAB), row)
    else:
        mod_spec = pl.BlockSpec((1, 1, D_MODEL), lambda i: (i // tpb, 0, 0))
        tab_spec = pl.BlockSpec((tm, SLAB), lambda i: (i % tpb, 0))

    def full(a):
        return pl.BlockSpec(a.shape, fixed)

    def out(width, dtype):
        return jax.ShapeDtypeStruct((rows, width), dtype), pl.BlockSpec((tm, width), row)

    if tail_rows == rows_per_batch:
        tail = out(B_WIDTH, F32)
    else:
        ntail = tail_rows // tm
        tail = (jax.ShapeDtypeStruct((rows // rows_per_batch * tail_rows, B_WIDTH), F32),
                pl.BlockSpec((tm, B_WIDTH),
                             lambda i: (i // tpb * ntail + jnp.maximum(i % tpb - (tpb - ntail), 0), 0)))

    if v_transposed:
        values = (jax.ShapeDtypeStruct((rows // rows_per_batch, A_HEADS // 2, tpb, SLAB, tm), BF16),
                  pl.BlockSpec((1, A_HEADS // 2, 1, SLAB, tm), lambda i: (i // tpb, 0, i % tpb, 0, 0)))
        rope_key = (jax.ShapeDtypeStruct((rows // rows_per_batch, A_ROPE, rows_per_batch), F32),
                    pl.BlockSpec((1, A_ROPE, tm), lambda i: (i // tpb, 0, i % tpb)))
    else:
        values = out(A_WIDTH, BF16)
        rope_key = out(A_ROPE, F32)

    outs = [out(A_HEADS * SLAB, BF16),
            out(A_KV_RANK, F32),
            rope_key,
            out(A_WIDTH, BF16),
            out(B_WIDTH, BF16),
            out(B_WIDTH, BF16),
            out(B_WIDTH, BF16),
            tail,
            tail,
            out(B_WIDTH, BF16),
            out(D_MODEL, BF16),
            out(D_MODEL, BF16),
            out(A_HEADS * SLAB, BF16),
            values]
    return pl.pallas_call(
        functools.partial(_front_kernel, per_row=per_row, v_transposed=v_transposed),
        out_shape=[o[0] for o in outs],
        grid=(nt,),
        in_specs=[pl.BlockSpec((tm, D_MODEL), row), mod_spec, mod_spec,
                  tab_spec, tab_spec, tab_spec] + [full(a) for a in consts],
        out_specs=[o[1] for o in outs],
        compiler_params=pltpu.CompilerParams(dimension_semantics=("arbitrary",),
                                             vmem_limit_bytes=VMEM_LIMIT),
        name="front",
    )(x2d, scale, shift, *rope_tabs, *consts)


def _select_heads(o_even, o_odd):
    lane = lax.broadcasted_iota(jnp.int32, o_even.shape, 1)
    return jnp.where(lane < A_VDIM, o_even, o_odd)


def _mla_prompt_kernel(q_ref, k_ref, vt_ref, o_ref, *, tq, nsub):
    qi = pl.program_id(2)
    tk = vt_ref.shape[-1]
    ratio = tq // tk
    n_full = nsub * ratio * qi
    chunk_delta = (lax.broadcasted_iota(jnp.int32, (tk, tq), 0) // CHUNK
                   - lax.broadcasted_iota(jnp.int32, (tk, tq), 1) // CHUNK)
    ones = jnp.ones((16, tk), BF16)
    chains = [(e, h) for h in range(nsub) for e in range(2)]

    def step(j, carry, work, ref):
        k0 = pl.multiple_of(j * tk, tk)
        scores = {}
        new = list(carry)

        def qk(c):
            e, h = chains[c]
            k = k_ref[0, pl.ds(k0, tk), e * SLAB:(e + 1) * SLAB]
            q = q_ref[0, h * tq:(h + 1) * tq, e * SLAB:(e + 1) * SLAB]
            scores[c] = _dot_nt(k, q)

        def update(c, max_delta):
            e, h = chains[c]
            m, acc, drift = carry[c]
            s = scores.pop(c)
            if max_delta is not None:
                s = jnp.where(chunk_delta <= max_delta, s, NEG_INF)
            vt1 = jnp.concatenate([vt_ref[0, 0, j, e * A_VDIM:(e + 1) * A_VDIM, :], ones], axis=0)
            bmax = jnp.max(s, axis=0, keepdims=True)
            m_new = jnp.maximum(m, bmax)
            if ref == "exact":
                p = jnp.exp2(s - m_new).astype(BF16)
                new[c] = (m_new, jnp.exp2(m - m_new) * acc + _dot(vt1, p), drift)
            else:
                m_ref = m
                if ref == "probe":
                    m_ref = jnp.maximum(m, jnp.max(s[:CHUNK], axis=0, keepdims=True))
                    acc = jnp.exp2(m - m_ref) * acc
                p = jnp.exp2(s - m_ref).astype(BF16)
                new[c] = (m_new, jnp.exp2(m_ref - m_new) * (acc + _dot(vt1, p)),
                          jnp.maximum(drift, bmax - m_ref))

        for c, _ in work:
            qk(c)
        for c, max_delta in work:
            update(c, max_delta)
        return tuple(new)

    everyone = [(c, None) for c in range(len(chains))]
    diagonal = []
    for d in range(nsub * ratio):
        work = []
        for c, (e, h) in enumerate(chains):
            if d * tk >= (h + 1) * tq:
                continue
            before = (d + 1) * tk <= h * tq
            work.append((c, None if before else (h * tq - d * tk) // CHUNK))
        diagonal.append(work)

    def attend(streaming):
        carry = tuple((jnp.full((1, tq), NEG_INF, F32), jnp.zeros((A_VDIM + 16, tq), F32),
                       jnp.full((1, tq), NEG_INF, F32)) for _ in chains)
        if streaming:
            first = jnp.minimum(n_full, 1)
            carry = lax.fori_loop(0, first, lambda j, cr: step(j, cr, everyone, "probe"), carry)
            carry = lax.fori_loop(first, n_full, lambda j, cr: step(j, cr, everyone, "lagged"), carry)
        else:
            carry = lax.fori_loop(0, n_full, lambda j, cr: step(j, cr, everyone, "exact"), carry)
        for d, work in enumerate(diagonal):
            carry = step(n_full + d, carry, work,
                         "exact" if not streaming else ("probe" if d == 0 else "lagged"))
        for h in range(nsub):
            outs = [carry[2 * h + e][1] for e in range(2)]
            outs = [acc[:A_VDIM] / acc[A_VDIM:A_VDIM + 1] for acc in outs]
            o_ref[0, h * tq:(h + 1) * tq, :] = jnp.concatenate(outs, axis=0).T.astype(o_ref.dtype)
        return functools.reduce(jnp.maximum, [jnp.max(cr[2]) for cr in carry])

    worst_drift = attend(True)

    @pl.when(worst_drift > MAX_EXP2_DRIFT)
    def _():
        attend(False)


def _mla_prompt(qa, ka, vt, *, tq, nsub):
    b, s, _ = qa.shape
    tk = vt.shape[-1]
    assert vt.shape == (b, A_HEADS // 2, s // tk, SLAB, tk) and tq % tk == 0
    tqq = nsub * tq
    return pl.pallas_call(
        functools.partial(_mla_prompt_kernel, tq=tq, nsub=nsub),
        out_shape=jax.ShapeDtypeStruct((b, s, A_WIDTH), BF16),
        grid=(b, A_HEADS // 2, s // tqq),
        in_specs=[pl.BlockSpec((1, tqq, 2 * SLAB), lambda bi, hp, qi: (bi, qi, hp)),
                  pl.BlockSpec((1, s, 2 * SLAB), lambda bi, hp, qi: (bi, 0, hp)),
                  pl.BlockSpec((1, 1, s // tk, SLAB, tk), lambda bi, hp, qi: (bi, hp, 0, 0, 0))],
        out_specs=pl.BlockSpec((1, tqq, SLAB), lambda bi, hp, qi: (bi, qi, hp)),
        compiler_params=pltpu.CompilerParams(
            dimension_semantics=("arbitrary", "arbitrary", "arbitrary"),
            vmem_limit_bytes=VMEM_LIMIT),
        name="mla_prompt",
    )(qa, ka, vt)


def _softmax_over_parts(scores, values):
    m = functools.reduce(jnp.maximum, [jnp.max(s, axis=-1, keepdims=True) for s in scores])
    ps = [jnp.exp2(s - m) for s in scores]
    l = sum(jnp.sum(p, axis=-1, keepdims=True) for p in ps)
    return sum(_dot(p.astype(BF16), v) for p, v in zip(ps, values)) / l


def _mla_sample_kernel(q_ref, latc_ref, krc_ref, kn_ref, vn_ref, wuk_ref, wuv_ref, gk_ref, segk_ref,
                       o_ref):
    latb = latc_ref[0].astype(BF16)
    krs = krc_ref[0].astype(F32)
    segk = segk_ref[...]
    pair = 2 * SLAB
    vc = _dot(latb, wuv_ref[...]).astype(BF16)
    for hp in range(A_HEADS // 2):
        kc = _dot(latb, wuk_ref[:, hp * pair:(hp + 1) * pair])
        ms = _dot((kc * kc).astype(BF16), segk)
        kc = kc * lax.rsqrt(ms + NORM_EPS) * gk_ref[:, hp * pair:(hp + 1) * pair]
        values = [vc[:, hp * SLAB:(hp + 1) * SLAB], vn_ref[0, :, hp * SLAB:(hp + 1) * SLAB]]
        outs = []
        for e in range(2):
            lanes = slice((2 * hp + e) * SLAB, (2 * hp + e + 1) * SLAB)
            q = q_ref[0, :, lanes]
            keys_cached = (kc[:, e * SLAB:(e + 1) * SLAB] + krs).astype(BF16)
            scores = [_dot_nt(q, keys_cached), _dot_nt(q, kn_ref[0, :, lanes])]
            outs.append(_softmax_over_parts(scores, values))
        o_ref[0, :, hp * SLAB:(hp + 1) * SLAB] = _select_heads(outs[0], outs[1]).astype(o_ref.dtype)


def _mla_sample(qa, lat_cache, krs_cache, ka_new, va_new, wuk_p, wuv, gk, segk):
    b, t, _ = qa.shape
    n = lat_cache.shape[1]
    spec = lambda rows, width: pl.BlockSpec((1, rows, width), lambda bi: (bi, 0, 0))
    full = lambda a: pl.BlockSpec(a.shape, lambda bi: (0, 0))
    return pl.pallas_call(
        _mla_sample_kernel,
        out_shape=jax.ShapeDtypeStruct((b, t, A_WIDTH), BF16),
        grid=(b,),
        in_specs=[spec(t, A_HEADS * SLAB), spec(n, A_KV_RANK), spec(n, SLAB),
                  spec(t, A_HEADS * SLAB), spec(t, A_WIDTH),
                  full(wuk_p), full(wuv), full(gk), full(segk)],
        out_specs=spec(t, A_WIDTH),
        compiler_params=pltpu.CompilerParams(dimension_semantics=("arbitrary",),
                                             vmem_limit_bytes=VMEM_LIMIT),
        name="mla_sample",
    )(qa, lat_cache, krs_cache, ka_new, va_new, wuk_p, wuv, gk, segk)


BAND_TQ = 256
BAND_KEYS = B_WINDOW + BAND_TQ
TOEPLITZ_W = 1024


def _band_table_kernel(rb_ref, onehot_ref, allowed_ref, tb_ref, tp_ref):
    g = jnp.dot(rb_ref[...], onehot_ref[...], preferred_element_type=F32,
                precision=lax.Precision.HIGHEST) * LOG2E
    allowed = allowed_ref[...] > 0.0
    for h in range(B_HEADS):
        row = jnp.broadcast_to(g[h:h + 1, :], (BAND_TQ, TOEPLITZ_W))
        t = pltpu.roll(row, 0, axis=1, stride=1, stride_axis=0)[:, :BAND_KEYS]
        tb_ref[h] = t
        tp_ref[h] = jnp.where(allowed, t, NEG_INF)


def _band_tables(rel_bias):
    n_rel = rel_bias.shape[1]
    n_pad = -(-n_rel // LANES) * LANES
    x = np.arange(TOEPLITZ_W)
    key_minus_query = np.where(x < BAND_KEYS, x, x - TOEPLITZ_W)
    dist = B_WINDOW - key_minus_query
    idx = np.clip(dist, -B_MAX_REL, B_MAX_REL) + B_MAX_REL
    onehot = np.zeros((n_pad, TOEPLITZ_W), np.float32)
    onehot[idx, x] = 1.0
    q_chunk = np.arange(BAND_TQ) // CHUNK + B_LEFT_CHUNKS
    k_chunk = np.arange(BAND_KEYS) // CHUNK
    allowed = (k_chunk[None, :] <= q_chunk[:, None]) & (k_chunk[None, :] >= q_chunk[:, None] - B_LEFT_CHUNKS)
    rb = jnp.pad(rel_bias, ((0, 0), (0, n_pad - n_rel)))
    shape = jax.ShapeDtypeStruct((B_HEADS, BAND_TQ, BAND_KEYS), F32)
    return pl.pallas_call(
        _band_table_kernel,
        out_shape=[shape, shape],
        compiler_params=pltpu.CompilerParams(vmem_limit_bytes=VMEM_LIMIT),
        name="band_table",
    )(rb, jnp.asarray(onehot), jnp.asarray(allowed.astype(np.float32)))


def _head_lane_masks(width):
    lane = lax.broadcasted_iota(jnp.int32, (1, width), 1)
    return [(lane < B_HDIM), (lane >= B_HDIM)]


def _band_prompt_kernel(q_ref, kp_ref, kc_ref, vp_ref, vc_ref, t_ref, o_ref, *, tq, nsub):
    qi = pl.program_id(2)
    nprev = B_WINDOW // tq
    nkb = nprev + 1

    def window(prev_ref, cur_ref, w):
        if w < nprev:
            return prev_ref[0, w * tq:(w + 1) * tq, :]
        return cur_ref[0, (w - nprev) * tq:(w - nprev + 1) * tq, :]

    masks = _head_lane_masks(SLAB)
    table = t_ref[...].reshape(2 * tq, nkb * tq)
    scores = []
    for t in range(nsub):
        q = q_ref[0, t * tq:(t + 1) * tq, :]
        qm = jnp.concatenate([jnp.where(mask, q, jnp.zeros_like(q)) for mask in masks], axis=0)
        ss = []
        for j in range(nkb):
            s = _dot_nt(qm, window(kp_ref, kc_ref, t + j)) + table[:, j * tq:(j + 1) * tq]
            if t + j < nprev:
                s = jnp.where(qi > 0, s, NEG_INF)
            ss.append(s)
        scores.append(ss)
    for t in range(nsub):
        o = _softmax_over_parts(scores[t], [window(vp_ref, vc_ref, t + j) for j in range(nkb)])
        o_ref[0, t * tq:(t + 1) * tq, :] = _select_heads(o[:tq], o[tq:]).astype(o_ref.dtype)


def _band_prompt(qb, kb, vb, table, *, tq, nsub):
    b, s, _ = qb.shape
    tqq = nsub * tq
    per_step = tqq // B_WINDOW
    assert tqq % B_WINDOW == 0 and B_WINDOW % tq == 0
    cur = pl.BlockSpec((1, tqq, SLAB), lambda hp, bi, qi: (bi, qi, hp))
    prev = pl.BlockSpec((1, B_WINDOW, SLAB),
                        lambda hp, bi, qi: (bi, jnp.maximum(qi * per_step - 1, 0), hp))
    return pl.pallas_call(
        functools.partial(_band_prompt_kernel, tq=tq, nsub=nsub),
        out_shape=jax.ShapeDtypeStruct((b, s, B_WIDTH), BF16),
        grid=(B_HEADS // 2, b, s // tqq),
        in_specs=[cur, prev, cur, prev, cur,
                  pl.BlockSpec((2, tq, B_WINDOW + tq), lambda hp, bi, qi: (hp, 0, 0))],
        out_specs=cur,
        compiler_params=pltpu.CompilerParams(
            dimension_semantics=("arbitrary", "arbitrary", "arbitrary"),
            vmem_limit_bytes=VMEM_LIMIT),
        name="band_prompt",
    )(qb, kb, kb, vb, vb, table)


def _band_sample_kernel(q_ref, kc_ref, vc_ref, kn_ref, vn_ref, t_ref, o_ref):
    masks = _head_lane_masks(SLAB)
    n = kc_ref.shape[-1]
    for hp in range(B_HEADS // 2):
        lanes = slice(hp * SLAB, (hp + 1) * SLAB)
        q = q_ref[0, :, lanes]
        kct = jnp.concatenate([kc_ref[0, 2 * hp], kc_ref[0, 2 * hp + 1]], axis=0).astype(BF16)
        vct = jnp.concatenate([vc_ref[0, 2 * hp], vc_ref[0, 2 * hp + 1]], axis=0).astype(BF16)
        kn, vn = kn_ref[0, :, lanes], vn_ref[0, :, lanes]
        outs = []
        for e in range(2):
            qm = jnp.where(masks[e], q, jnp.zeros_like(q))
            s_c = _dot(qm, kct) + t_ref[2 * hp + e, :, :n]
            s_n = _dot_nt(qm, kn) + t_ref[2 * hp + e, :, n:n + kn.shape[0]]
            m = jnp.maximum(jnp.max(s_c, axis=-1, keepdims=True), jnp.max(s_n, axis=-1, keepdims=True))
            p_c, p_n = jnp.exp2(s_c - m), jnp.exp2(s_n - m)
            l = jnp.sum(p_c, axis=-1, keepdims=True) + jnp.sum(p_n, axis=-1, keepdims=True)
            outs.append((_dot_nt(p_c.astype(BF16), vct) + _dot(p_n.astype(BF16), vn)) / l)
        o_ref[0, :, lanes] = _select_heads(outs[0], outs[1]).astype(o_ref.dtype)


def _band_sample(qb, kb_cache_t, vb_cache_t, kb_new, vb_new, table):
    b, t, _ = qb.shape
    n = kb_cache_t.shape[-1]
    spec = lambda rows: pl.BlockSpec((1, rows, B_WIDTH), lambda bi: (bi, 0, 0))
    cache = pl.BlockSpec((1, B_HEADS, B_HDIM, n), lambda bi: (bi, 0, 0, 0))
    return pl.pallas_call(
        _band_sample_kernel,
        out_shape=jax.ShapeDtypeStruct((b, t, B_WIDTH), BF16),
        grid=(b,),
        in_specs=[spec(t), cache, cache, spec(t), spec(t),
                  pl.BlockSpec((B_HEADS, t, BAND_KEYS), lambda bi: (0, 0, 0))],
        out_specs=spec(t),
        compiler_params=pltpu.CompilerParams(dimension_semantics=("arbitrary",),
                                             vmem_limit_bytes=VMEM_LIMIT),
        name="band_sample",
    )(qb, kb_cache_t, vb_cache_t, kb_new, vb_new, table)


def _back_kernel(x_ref, gate_ref, aa_ref, sga_ref, ab_ref, sgb_ref, sma_ref, smb_ref,
                 woa_ref, wob_ref, wout_ref, o_ref, *, per_row):
    gate = gate_ref[...] if per_row else gate_ref[0]
    ua = _dot(aa_ref[...] * sga_ref[...], woa_ref[...])
    ub = _dot(ab_ref[...] * sgb_ref[...], wob_ref[...])
    merged = sma_ref[...].astype(F32) * ua + smb_ref[...].astype(F32) * ub
    o_ref[...] = x_ref[...] + gate * _dot(merged.astype(BF16), wout_ref[...])


def _back(x2d, gate, aa, sga, ab, sgb, sma, smb, woa, wob, wout, *, tm, rows_per_batch, per_row):
    rows = x2d.shape[0]
    row = lambda i: (i, 0)
    fixed = lambda i: (0, 0)
    if per_row:
        gate_spec = pl.BlockSpec((tm, D_MODEL), row)
    else:
        tpb = rows_per_batch // tm
        gate_spec = pl.BlockSpec((1, 1, D_MODEL), lambda i: (i // tpb, 0, 0))
    return pl.pallas_call(
        functools.partial(_back_kernel, per_row=per_row),
        out_shape=jax.ShapeDtypeStruct((rows, D_MODEL), F32),
        grid=(rows // tm,),
        in_specs=[pl.BlockSpec((tm, D_MODEL), row), gate_spec,
                  pl.BlockSpec((tm, A_WIDTH), row), pl.BlockSpec((tm, A_WIDTH), row),
                  pl.BlockSpec((tm, B_WIDTH), row), pl.BlockSpec((tm, B_WIDTH), row),
                  pl.BlockSpec((tm, D_MODEL), row), pl.BlockSpec((tm, D_MODEL), row),
                  pl.BlockSpec(woa.shape, fixed), pl.BlockSpec(wob.shape, fixed),
                  pl.BlockSpec(wout.shape, fixed)],
        out_specs=pl.BlockSpec((tm, D_MODEL), row),
        compiler_params=pltpu.CompilerParams(dimension_semantics=("arbitrary",),
                                             vmem_limit_bytes=VMEM_LIMIT),
        name="back",
    )(x2d, gate, aa, sga, ab, sgb, sma, smb, woa, wob, wout)


def _seg_matrix(group_of_lane, sizes):
    lane = np.arange(2 * SLAB)
    slab = lane // SLAB
    grp = group_of_lane[lane % SLAB]
    same = (slab[:, None] == slab[None, :]) & (grp[:, None] == grp[None, :])
    return jnp.asarray(np.where(same, 1.0 / sizes[grp][None, :], 0.0), dtype=BF16)


def _rope_tables(pos):
    inv = ROPE_BASE ** (-np.arange(0, A_ROPE, 2, dtype=np.float64) / A_ROPE)
    ang = np.asarray(pos, np.float64)[:, None] * inv[None, :]
    cos, sin = np.cos(ang), np.sin(ang)
    n = ang.shape[0]
    ones = np.ones((n, A_NOPE))
    zeros = np.zeros((n, A_NOPE))
    z16 = np.zeros((n, HALF_ROPE))
    pad1 = np.ones((n, SLAB - A_QK))
    pad0 = np.zeros((n, SLAB - A_QK))
    cos_t = np.concatenate([ones, cos, cos, pad1], axis=1)
    sin_lo = np.concatenate([zeros, -sin, z16, pad0], axis=1)
    sin_hi = np.concatenate([zeros, z16, sin, pad0], axis=1)
    return tuple(jnp.asarray(a, dtype=F32) for a in (cos_t, sin_lo, sin_hi))


def kernel(x_prompt, x_sample, cache_mla_latent, cache_mla_krope, cache_band_k, cache_band_v,
           c_prompt, c_sample, g_norm, w_ada, b_ada, w_in, g_q_lat, w_uq, g_kv_lat, w_uk, w_uv,
           g_qn_a, g_qr_a, g_kn_a, g_kr_a, g_q_b, g_k_b, rel_bias_b, w_oa, w_ob, w_out):
    bp, s, _ = x_prompt.shape
    bs, t, _ = x_sample.shape
    past = cache_mla_latent.shape[2]
    win_s = cache_band_k.shape[2]
    win_p = min(B_WINDOW, s)
    depth = g_norm.shape[0]
    assert depth == 1

    seg_q = _seg_matrix(np.where(np.arange(SLAB) < A_NOPE, 0, np.where(np.arange(SLAB) < A_QK, 1, 2)),
                        np.array([A_NOPE, A_ROPE, SLAB - A_QK], np.float64))
    seg_64 = _seg_matrix(np.arange(SLAB) // B_HDIM, np.array([B_HDIM, B_HDIM], np.float64))
    seg_k = _seg_matrix(np.where(np.arange(SLAB) < A_NOPE, 0, 1),
                        np.array([A_NOPE, SLAB - A_NOPE], np.float64))

    assert win_s == B_WINDOW and t <= BAND_TQ
    pos_s = past + np.arange(t)

    (g_norm, w_ada, b_ada, w_in, g_q_lat, w_uq, g_kv_lat, w_uk, w_uv, g_qn_a, g_qr_a, g_kn_a,
     g_kr_a, g_q_b, g_k_b, rel_bias_b, w_oa, w_ob, w_out) = [
        a.reshape(a.shape[1:]) for a in
        (g_norm, w_ada, b_ada, w_in, g_q_lat, w_uq, g_kv_lat, w_uk, w_uv, g_qn_a, g_qr_a, g_kn_a,
         g_kr_a, g_q_b, g_k_b, rel_bias_b, w_oa, w_ob, w_out)]
    zpad = jnp.zeros((D_MODEL, SLAB - A_QK), F32)
    z64 = jnp.zeros((D_MODEL, A_NOPE), F32)
    c_kr0 = A_Q_RANK + A_KV_RANK
    w_low = jnp.concatenate([w_in[:, :c_kr0], z64, w_in[:, c_kr0:c_kr0 + A_ROPE], zpad],
                            axis=1).astype(BF16)
    w_rest = w_in[:, c_kr0 + A_ROPE:].astype(BF16)
    wuq_p = jnp.pad(w_uq, ((0, 0), (0, 0), (0, SLAB - A_QK))).reshape(A_Q_RANK, -1).astype(BF16)
    wuk_p = jnp.pad(w_uk, ((0, 0), (0, 0), (0, SLAB - A_NOPE))).reshape(A_KV_RANK, -1).astype(BF16)
    wuv_b = w_uv.reshape(A_KV_RANK, -1).astype(BF16)
    wuv_t = wuv_b.T
    qscale = A_SCALE * LOG2E
    gq = jnp.tile(jnp.concatenate([g_qn_a, g_qr_a, jnp.zeros((SLAB - A_QK,), F32)]) * qscale,
                  A_HEADS)[None]
    gk = jnp.tile(jnp.concatenate([g_kn_a, jnp.zeros((SLAB - A_NOPE,), F32)]), A_HEADS)[None]
    gkr = jnp.concatenate([jnp.zeros((A_NOPE,), F32), g_kr_a, jnp.zeros((SLAB - A_QK,), F32)])[None]
    gqb = jnp.tile(g_q_b * (B_SCALE * LOG2E), B_HEADS)[None]
    gkb = jnp.tile(g_k_b, B_HEADS)[None]
    consts = (g_norm[None], w_low, w_rest, g_q_lat[None], wuq_p, g_kv_lat[None], gq, gkr,
              gqb, gkb, seg_q, seg_64)
    woa, wob, wout = w_oa.astype(BF16), w_ob.astype(BF16), w_out.astype(BF16)

    c_rows = jnp.concatenate([c_prompt, jnp.zeros((8 - bp, D_MODEL), F32),
                              jnp.repeat(c_sample, t, axis=0)], axis=0)
    mod = _adaln(c_rows, w_ada, b_ada)
    shift, scale, gate = mod[:, :D_MODEL], mod[:, D_MODEL:2 * D_MODEL], mod[:, 2 * D_MODEL:]

    xp2d = x_prompt.reshape(bp * s, D_MODEL)
    (qa, lat, kr, sga, qb, kb, vb, kbf, vbf, sgb, sma, smb, ka, vt) = _front(
        xp2d, scale[:bp, None], shift[:bp, None], _rope_tables(np.arange(s)),
        consts + (wuk_p, wuv_t, gk, seg_k),
        tm=FRONT_TM, rows_per_batch=s, per_row=False, tail_rows=win_p, v_transposed=True)
    attn_a = _mla_prompt(qa.reshape(bp, s, -1), ka.reshape(bp, s, -1), vt, tq=MLA_TQ, nsub=MLA_NSUB)
    table_plain, table_band = _band_tables(rel_bias_b)
    attn_b = _band_prompt(qb.reshape(bp, s, -1), kb.reshape(bp, s, -1), vb.reshape(bp, s, -1),
                          table_band, tq=BAND_TQ, nsub=BAND_NSUB)
    y_prompt = _back(xp2d, gate[:bp, None], attn_a.reshape(bp * s, -1), sga,
                     attn_b.reshape(bp * s, -1), sgb, sma, smb, woa, wob, wout,
                     tm=BACK_TM, rows_per_batch=s, per_row=False).reshape(bp, s, D_MODEL)

    xs2d = x_sample.reshape(bs * t, D_MODEL)
    rows_s = bs * t
    pos_tab = _rope_tables(np.tile(pos_s, bs))
    (qa2, lat2, kr2, sga2, qb2, kb2, vb2, kbf2, vbf2, sgb2, sma2, smb2, ka_n, va_n) = _front(
        xs2d, scale[8:], shift[8:], pos_tab, consts + (wuk_p, wuv_b, gk, seg_k),
        tm=rows_s, rows_per_batch=t, per_row=True, tail_rows=t, v_transposed=False)
    krs_cache = jnp.pad(cache_mla_krope.reshape(bs, past, A_ROPE),
                        ((0, 0), (0, 0), (A_NOPE, SLAB - A_QK))).astype(BF16)
    attn_a2 = _mla_sample(qa2.reshape(bs, t, -1), cache_mla_latent.reshape(bs, past, A_KV_RANK),
                          krs_cache, ka_n.reshape(bs, t, -1), va_n.reshape(bs, t, -1),
                          wuk_p, wuv_b, gk, seg_k)
    cache_kt = jnp.transpose(cache_band_k.reshape(bs, win_s, B_HEADS, B_HDIM), (0, 2, 3, 1))
    cache_vt = jnp.transpose(cache_band_v.reshape(bs, win_s, B_HEADS, B_HDIM), (0, 2, 3, 1))
    attn_b2 = _band_sample(qb2.reshape(bs, t, -1), cache_kt, cache_vt, kb2.reshape(bs, t, -1),
                           vb2.reshape(bs, t, -1), table_plain)
    y_sample = _back(xs2d, gate[8:], attn_a2.reshape(rows_s, -1), sga2, attn_b2.reshape(rows_s, -1),
                     sgb2, sma2, smb2, woa, wob, wout,
                     tm=rows_s, rows_per_batch=t, per_row=True).reshape(bs, t, D_MODEL)

    return (y_prompt, y_sample,
            lat.reshape(1, bp, s, A_KV_RANK), jnp.swapaxes(kr, 1, 2)[None],
            kbf.reshape(1, bp, win_p, B_HEADS, B_HDIM), vbf.reshape(1, bp, win_p, B_HEADS, B_HDIM),
            lat2.reshape(1, bs, t, A_KV_RANK), kr2.reshape(1, bs, t, A_ROPE),
            kbf2.reshape(1, bs, t, B_HEADS, B_HDIM), vbf2.reshape(1, bs, t, B_HEADS, B_HDIM))
```
